```python
import math
import jax, jax.numpy as jnp
from jax import lax
import numpy as np

D_MODEL = 1024
BATCH = 8
SEQ = 2048
DEPTH = 4
DEC_BATCH = 32
DEC_SEQ = 1
PAST_LEN = 16384
PAGE_SIZE = 128

D_MIX = 2 * D_MODEL
GROUP_W = D_MIX // 4
SSD_HEAD_DIM = 64
SSD_HEADS = GROUP_W // SSD_HEAD_DIM
SSD_GROUPS = 2
SSD_HPG = SSD_HEADS // SSD_GROUPS
SSD_STATE = 128
CONV_W = 4
SSD_CONV_CH = GROUP_W + 2 * SSD_GROUPS * SSD_STATE
ML_HEADS = 4
ML_HEAD_DIM = GROUP_W // ML_HEADS
MLA_HEADS = 4
MLA_NOPE = 128
MLA_ROPE = 64
MLA_V = GROUP_W // MLA_HEADS
Q_RANK = 3 * D_MODEL // 8
KV_RANK = D_MODEL // 4
ROPE_BASE = 10000.0
X_HEADS = 4
X_HEAD_DIM = GROUP_W // X_HEADS
MEM_LEN = 256
CHUNK = 128
Q_BLOCK = 128
EPS = 1e-6
IN_SIZES = (D_MIX, SSD_CONV_CH, SSD_HEADS, GROUP_W, ML_HEADS, ML_HEADS, GROUP_W, Q_RANK, KV_RANK, MLA_ROPE, GROUP_W)
D_IN = sum(IN_SIZES)

kernel_name = 'hymba_ssd_mlstm_mla_memory_decoder_step'


def rms_norm(x, g):
    xf = x.astype(jnp.float32)
    y = xf * lax.rsqrt(jnp.mean(xf * xf, axis=-1, keepdims=True) + EPS)
    return (y * g.astype(jnp.float32)).astype(x.dtype)


def rope(x, pos):
    half = MLA_ROPE // 2
    inv = ROPE_BASE ** (-jnp.arange(half, dtype=jnp.float32) * 2.0 / MLA_ROPE)
    ang = pos.astype(jnp.float32)[:, None] * inv[None, :]
    shape = (pos.shape[0],) + (1,) * (x.ndim - 3) + (half,)
    cos = jnp.cos(ang).reshape(shape)
    sin = jnp.sin(ang).reshape(shape)
    xf = x.astype(jnp.float32)
    x1, x2 = xf[..., :half], xf[..., half:]
    return jnp.concatenate([x1 * cos - x2 * sin, x1 * sin + x2 * cos], axis=-1).astype(x.dtype)


def to_chunks(t, c):
    return t.reshape(t.shape[0], t.shape[1] // c, c, *t.shape[2:]).swapaxes(0, 1)


def from_chunks(t):
    t = t.swapaxes(0, 1)
    return t.reshape(t.shape[0], t.shape[1] * t.shape[2], *t.shape[3:])


def causal_conv(u, buf, w, b):
    seq_len = u.shape[1]
    full = jnp.concatenate([buf.astype(u.dtype), u], axis=1)
    y = b + sum(full[:, j:j + seq_len] * w[j] for j in range(CONV_W))
    return y, full[:, full.shape[1] - (CONV_W - 1):]


def ssd_scan(x, dt, a, bm, cm, h0):
    c = math.gcd(x.shape[1], CHUNK)
    tri = jnp.tril(jnp.ones((c, c), dtype=bool))[None, :, :, None, None]

    def step(h, inp):
        xc, dtc, bc, cc = inp
        acum = jnp.cumsum(dtc * a, axis=1)
        seg = jnp.where(tri, acum[:, :, None] - acum[:, None, :], -jnp.inf)
        w = jnp.einsum('bign,bjgn->bijg', cc, bc)[..., None] * jnp.exp(seg) * dtc[:, None]
        y = (jnp.einsum('bijgh,bjghp->bighp', w, xc)
             + jnp.einsum('bign,bghpn->bighp', cc, h) * jnp.exp(acum)[..., None])
        w_end = jnp.exp(acum[:, -1:] - acum) * dtc
        h_new = (jnp.exp(acum[:, -1])[..., None, None] * h
                 + jnp.einsum('bjghp,bjgn->bghpn', w_end[..., None] * xc, bc))
        return h_new, y

    h_fin, ys = lax.scan(step, h0, (to_chunks(x, c), to_chunks(dt, c), to_chunks(bm, c), to_chunks(cm, c)))
    return from_chunks(ys), h_fin


def mlstm_scan(q, k, v, log_i, log_f, c0, n0, m0):
    c = math.gcd(q.shape[1], CHUNK)
    tri = jnp.tril(jnp.ones((c, c), dtype=bool))[None, :, :, None]

    def step(carry, inp):
        cmat, nvec, mval = carry
        qc, kc, vc, lic, lfc = inp
        bcum = jnp.cumsum(lfc, axis=1)
        lw = jnp.where(tri, bcum[:, :, None] - bcum[:, None, :] + lic[:, None], -jnp.inf)
        l_prev = bcum + mval[:, None]
        m_t = jnp.maximum(l_prev, jnp.max(lw, axis=2))
        sc = jnp.einsum('bthd,bshd->btsh', qc, kc) * jnp.exp(lw - m_t[:, :, None])
        g_prev = jnp.exp(l_prev - m_t)
        num = (jnp.einsum('btsh,bshv->bthv', sc, vc)
               + g_prev[..., None] * jnp.einsum('bhvd,bthd->bthv', cmat, qc))
        den = jnp.sum(sc, axis=2) + g_prev * jnp.einsum('bhd,bthd->bth', nvec, qc)
        h = num / jnp.maximum(jnp.abs(den), jnp.exp(-m_t))[..., None]
        m_new = m_t[:, -1]
        w_end = jnp.exp(bcum[:, -1:] - bcum + lic - m_new[:, None])
        decay = jnp.exp(bcum[:, -1] + mval - m_new)
        c_new = decay[..., None, None] * cmat + jnp.einsum('bshv,bshd->bhvd', w_end[..., None] * vc, kc)
        n_new = decay[..., None] * nvec + jnp.einsum('bsh,bshd->bhd', w_end, kc)
        return (c_new, n_new, m_new), h

    xs = tuple(to_chunks(t, c) for t in (q, k, v, log_i, log_f))
    (c_fin, n_fin, m_fin), hs = lax.scan(step, (c0, n0, m0), xs)
    return from_chunks(hs), c_fin, n_fin, m_fin


def mla_attention(q_nope, q_rope, q_pos, k_nope, k_rope, lat, k_pos, w_uv):
    scale = (MLA_NOPE + MLA_ROPE) ** -0.5

    def block(args):
        qn, qr, qp = args
        s = (jnp.einsum('bqhd,bkhd->bhqk', qn, k_nope)
             + jnp.einsum('bqhr,bkr->bhqk', qr, k_rope)).astype(jnp.float32) * scale
        s = jnp.where(k_pos[None, None, None, :] <= qp[None, None, :, None], s, -jnp.inf)
        p = jax.nn.softmax(s, axis=-1).astype(lat.dtype)
        ctx = jnp.einsum('bhqk,bkc->bqhc', p, lat)
        return jnp.einsum('bqhc,chv->bqhv', ctx, w_uv)

    seq_len = q_nope.shape[1]
    if seq_len > Q_BLOCK and seq_len % Q_BLOCK == 0:
        nb = seq_len // Q_BLOCK
        blk = lambda t: t.reshape(t.shape[0], nb, Q_BLOCK, *t.shape[2:]).swapaxes(0, 1)
        out = lax.map(block, (blk(q_nope), blk(q_rope), q_pos.reshape(nb, Q_BLOCK)))
        return out.swapaxes(0, 1).reshape(q_nope.shape[0], seq_len, MLA_HEADS, MLA_V)
    return block((q_nope, q_rope, q_pos))


def mixer_layer(lp, x, pos, k_pos, mem_k, mem_v, past_lat, past_kr,
                ssm_h0, ssm_buf0, ml_c0, ml_n0, ml_m0, ml_buf0):
    bsz, seq_len, _ = x.shape
    act = x.dtype
    u = rms_norm(x, lp['norm_g'])
    proj = u @ lp['w_in']
    split_at = [int(i) for i in np.cumsum(IN_SIZES)[:-1]]
    (z, xbc_raw, dt_raw, xm_raw, i_raw, f_raw, o_raw,
     cq_raw, ckv_raw, kr_raw, xq_raw) = jnp.split(proj, split_at, axis=-1)
    z_a, z_b, z_c, z_d = jnp.split(z, 4, axis=-1)

    xbc, ssm_buf = causal_conv(xbc_raw, ssm_buf0, lp['conv_a_w'], lp['conv_a_b'])
    xbc = jax.nn.silu(xbc).astype(jnp.float32)
    xa, ba, ca = jnp.split(xbc, [GROUP_W, GROUP_W + SSD_GROUPS * SSD_STATE], axis=-1)
    xa = xa.reshape(bsz, seq_len, SSD_GROUPS, SSD_HPG, SSD_HEAD_DIM)
    ba = ba.reshape(bsz, seq_len, SSD_GROUPS, SSD_STATE)
    ca = ca.reshape(bsz, seq_len, SSD_GROUPS, SSD_STATE)
    dt = jax.nn.softplus((dt_raw + lp['dt_bias']).astype(jnp.float32)).reshape(bsz, seq_len, SSD_GROUPS, SSD_HPG)
    a = -jnp.exp(lp['a_log'].astype(jnp.float32)).reshape(SSD_GROUPS, SSD_HPG)
    h0 = ssm_h0.astype(jnp.float32).reshape(bsz, SSD_GROUPS, SSD_HPG, SSD_HEAD_DIM, SSD_STATE)
    ya, ssm_h = ssd_scan(xa, dt, a, ba, ca, h0)
    ya = ya + lp['d_skip'].astype(jnp.float32).reshape(SSD_GROUPS, SSD_HPG, 1) * xa
    ya = ya.reshape(bsz, seq_len, GROUP_W).astype(act) * jax.nn.silu(z_a)
    gw = GROUP_W // SSD_GROUPS
    ya = rms_norm(ya.reshape(bsz, seq_len, SSD_GROUPS, gw),
                  lp['ssd_norm_g'].reshape(SSD_GROUPS, gw)).reshape(bsz, seq_len, GROUP_W)

    xmc, ml_buf = causal_conv(xm_raw, ml_buf0, lp['conv_b_w'], lp['conv_b_b'])
    xmc = jax.nn.silu(xmc).reshape(bsz, seq_len, ML_HEADS, ML_HEAD_DIM)
    xmv = xm_raw.reshape(bsz, seq_len, ML_HEADS, ML_HEAD_DIM)
    q_m = jnp.einsum('blhd,hde->blhe', xmc, lp['w_mq']).astype(jnp.float32)
    k_m = (jnp.einsum('blhd,hde->blhe', xmc, lp['w_mk']) * ML_HEAD_DIM ** -0.5).astype(jnp.float32)
    v_m = jnp.einsum('blhd,hde->blhe', xmv, lp['w_mv']).astype(jnp.float32)
    log_i = (i_raw + lp['b_i']).astype(jnp.float32)
    log_f = jax.nn.log_sigmoid((f_raw + lp['b_f']).astype(jnp.float32))
    hb, ml_c, ml_n, ml_m = mlstm_scan(q_m, k_m, v_m, log_i, log_f, ml_c0.astype(jnp.float32),
                                      ml_n0.astype(jnp.float32), ml_m0.astype(jnp.float32))
    hb = jax.nn.sigmoid(o_raw).reshape(bsz, seq_len, ML_HEADS, ML_HEAD_DIM) * hb.astype(act)
    yb = rms_norm(hb, lp['ml_norm_g'].reshape(ML_HEADS, ML_HEAD_DIM)).reshape(bsz, seq_len, GROUP_W) * jax.nn.silu(z_b)

    c_q = rms_norm(cq_raw, lp['q_norm_g'])
    qf = jnp.einsum('blr,rhd->blhd', c_q, lp['w_uq'])
    q_nope = rms_norm(qf[..., :MLA_NOPE], lp['qn_nope_g'])
    q_rope = rope(rms_norm(qf[..., MLA_NOPE:], lp['qn_rope_g']), pos)
    c_kv = rms_norm(ckv_raw, lp['kv_norm_g'])
    k_rope = rope(rms_norm(kr_raw, lp['kn_rope_g']), pos)
    if past_lat is None:
        all_lat, all_kr = c_kv, k_rope
    else:
        all_lat = jnp.concatenate([past_lat.astype(act), c_kv], axis=1)
        all_kr = jnp.concatenate([past_kr.astype(act), k_rope], axis=1)
    k_nope = rms_norm(jnp.einsum('bkr,rhd->bkhd', all_lat, lp['w_uk']), lp['kn_nope_g'])
    yc = mla_attention(q_nope, q_rope, pos, k_nope, all_kr, all_lat, k_pos, lp['w_uv'])
    yc = yc.reshape(bsz, seq_len, GROUP_W) * jax.nn.silu(z_c)

    xq = rms_norm(xq_raw.reshape(bsz, seq_len, X_HEADS, X_HEAD_DIM), lp['xq_norm_g'])
    s = jnp.einsum('blhd,bmhd->bhlm', xq, mem_k.astype(act)).astype(jnp.float32) * X_HEAD_DIM ** -0.5
    p = jax.nn.softmax(s, axis=-1).astype(act)
    yd = jnp.einsum('bhlm,bmhd->blhd', p, mem_v.astype(act)).reshape(bsz, seq_len, GROUP_W) * jax.nn.silu(z_d)

    y = jnp.concatenate([ya, yb, yc, yd], axis=-1) @ lp['w_out']
    new_state = (c_kv, k_rope,
                 ssm_h.reshape(bsz, SSD_HEADS, SSD_HEAD_DIM, SSD_STATE).astype(act), ssm_buf,
                 ml_c.astype(act), ml_n.astype(act), ml_m.astype(act), ml_buf)
    return x + y, new_state


def setup_inputs(seed: int = 0) -> dict:
    key = jax.random.key(seed)
    keys = jax.random.split(key, 64)
    counter = [0]

    def nxt():
        counter[0] += 1
        return keys[counter[0] - 1]

    def nrm(shape, scale=1.0):
        return jax.random.normal(nxt(), shape, jnp.float32) * scale

    def gain(shape):
        return 1.0 + nrm(shape, 0.02)

    n_pages = PAST_LEN // PAGE_SIZE
    n_used = DEC_BATCH * n_pages
    n_pool = n_used + max(1, n_used // 4)

    x_prompt = nrm((BATCH, SEQ, D_MODEL))
    x_sample = nrm((DEC_BATCH, DEC_SEQ, D_MODEL))
    mem_prompt = nrm((BATCH, MEM_LEN, D_MODEL))
    cache_mla_latent = nrm((DEPTH, n_pool, PAGE_SIZE, KV_RANK))
    cache_mla_krope = nrm((DEPTH, n_pool, PAGE_SIZE, MLA_ROPE))
    cache_mem_k = nrm((DEPTH, DEC_BATCH, MEM_LEN, X_HEADS, X_HEAD_DIM))
    cache_mem_v = nrm((DEPTH, DEC_BATCH, MEM_LEN, X_HEADS, X_HEAD_DIM))
    state_ssm = nrm((DEPTH, DEC_BATCH, SSD_HEADS, SSD_HEAD_DIM, SSD_STATE), 0.3)
    state_ssm_conv = nrm((DEPTH, DEC_BATCH, CONV_W - 1, SSD_CONV_CH))
    state_mlstm_C = nrm((DEPTH, DEC_BATCH, ML_HEADS, ML_HEAD_DIM, ML_HEAD_DIM), 0.1)
    state_mlstm_n = nrm((DEPTH, DEC_BATCH, ML_HEADS, ML_HEAD_DIM), 0.1)
    state_mlstm_m = nrm((DEPTH, DEC_BATCH, ML_HEADS), 0.5)
    state_mlstm_conv = nrm((DEPTH, DEC_BATCH, CONV_W - 1, GROUP_W))
    page_table = jax.random.permutation(nxt(), n_pool)[:n_used].reshape(DEC_BATCH, n_pages).astype(jnp.int32)

    dt0 = jnp.exp(jax.random.uniform(nxt(), (DEPTH, SSD_HEADS), jnp.float32, math.log(1e-3), math.log(1e-1)))
    dt_bias = dt0 + jnp.log(-jnp.expm1(-dt0))
    a_log = jnp.log(jax.random.uniform(nxt(), (DEPTH, SSD_HEADS), jnp.float32, 1.0, 16.0))

    return {
        'x_prompt': x_prompt,
        'x_sample': x_sample,
        'mem_prompt': mem_prompt,
        'cache_mla_latent': cache_mla_latent,
        'cache_mla_krope': cache_mla_krope,
        'cache_mem_k': cache_mem_k,
        'cache_mem_v': cache_mem_v,
        'state_ssm': state_ssm,
        'state_ssm_conv': state_ssm_conv,
        'state_mlstm_C': state_mlstm_C,
        'state_mlstm_n': state_mlstm_n,
        'state_mlstm_m': state_mlstm_m,
        'state_mlstm_conv': state_mlstm_conv,
        'page_table': page_table,
        'norm_g': gain((DEPTH, D_MODEL)),
        'w_in': nrm((DEPTH, D_MODEL, D_IN), D_MODEL ** -0.5),
        'conv_a_w': nrm((DEPTH, CONV_W, SSD_CONV_CH), CONV_W ** -0.5),
        'conv_a_b': nrm((DEPTH, SSD_CONV_CH), 0.02),
        'dt_bias': dt_bias,
        'a_log': a_log,
        'd_skip': 1.0 + nrm((DEPTH, SSD_HEADS), 0.1),
        'ssd_norm_g': gain((DEPTH, GROUP_W)),
        'conv_b_w': nrm((DEPTH, CONV_W, GROUP_W), CONV_W ** -0.5),
        'conv_b_b': nrm((DEPTH, GROUP_W), 0.02),
        'w_mq': nrm((DEPTH, ML_HEADS, ML_HEAD_DIM, ML_HEAD_DIM), ML_HEAD_DIM ** -0.5),
        'w_mk': nrm((DEPTH, ML_HEADS, ML_HEAD_DIM, ML_HEAD_DIM), ML_HEAD_DIM ** -0.5),
        'w_mv': nrm((DEPTH, ML_HEADS, ML_HEAD_DIM, ML_HEAD_DIM), ML_HEAD_DIM ** -0.5),
        'b_i': nrm((DEPTH, ML_HEADS), 0.1),
        'b_f': jnp.linspace(3.0, 6.0, ML_HEADS, dtype=jnp.float32)[None, :] + nrm((DEPTH, ML_HEADS), 0.1),
        'ml_norm_g': gain((DEPTH, GROUP_W)),
        'q_norm_g': gain((DEPTH, Q_RANK)),
        'kv_norm_g': gain((DEPTH, KV_RANK)),
        'w_uq': nrm((DEPTH, Q_RANK, MLA_HEADS, MLA_NOPE + MLA_ROPE), Q_RANK ** -0.5),
        'w_uk': nrm((DEPTH, KV_RANK, MLA_HEADS, MLA_NOPE), KV_RANK ** -0.5),
        'w_uv': nrm((DEPTH, KV_RANK, MLA_HEADS, MLA_V), KV_RANK ** -0.5),
        'qn_nope_g': gain((DEPTH, MLA_NOPE)),
        'qn_rope_g': gain((DEPTH, MLA_ROPE)),
        'kn_nope_g': gain((DEPTH, MLA_NOPE)),
        'kn_rope_g': gain((DEPTH, MLA_ROPE)),
        'mem_norm_g': gain((DEPTH, D_MODEL)),
        'w_mem_k': nrm((DEPTH, D_MODEL, GROUP_W), D_MODEL ** -0.5),
        'w_mem_v': nrm((DEPTH, D_MODEL, GROUP_W), D_MODEL ** -0.5),
        'xk_norm_g': gain((DEPTH, X_HEAD_DIM)),
        'xq_norm_g': gain((DEPTH, X_HEAD_DIM)),
        'w_out': nrm((DEPTH, D_MIX, D_MODEL), D_MIX ** -0.5),
        'final_norm_g': gain((D_MODEL,)),
    }


def reference(x_prompt, x_sample, mem_prompt, cache_mla_latent, cache_mla_krope, cache_mem_k, cache_mem_v,
              state_ssm, state_ssm_conv, state_mlstm_C, state_mlstm_n, state_mlstm_m, state_mlstm_conv,
              page_table, norm_g, w_in, conv_a_w, conv_a_b, dt_bias, a_log, d_skip, ssd_norm_g,
              conv_b_w, conv_b_b, w_mq, w_mk, w_mv, b_i, b_f, ml_norm_g, q_norm_g, kv_norm_g,
              w_uq, w_uk, w_uv, qn_nope_g, qn_rope_g, kn_nope_g, kn_rope_g, mem_norm_g,
              w_mem_k, w_mem_v, xk_norm_g, xq_norm_g, w_out, final_norm_g):
    act = x_prompt.dtype
    bp, lp_len, _ = x_prompt.shape
    bs, ls_len, _ = x_sample.shape
    n_pages = page_table.shape[1]
    past_len = n_pages * cache_mla_latent.shape[2]
    mem_len = mem_prompt.shape[1]
    pos_p = jnp.arange(lp_len, dtype=jnp.int32)
    pos_s = past_len + jnp.arange(ls_len, dtype=jnp.int32)
    kpos_s = jnp.arange(past_len + ls_len, dtype=jnp.int32)

    p_h0 = jnp.zeros((bp, SSD_HEADS, SSD_HEAD_DIM, SSD_STATE), act)
    p_sbuf0 = jnp.zeros((bp, CONV_W - 1, SSD_CONV_CH), act)
    p_c0 = jnp.zeros((bp, ML_HEADS, ML_HEAD_DIM, ML_HEAD_DIM), act)
    p_n0 = jnp.zeros((bp, ML_HEADS, ML_HEAD_DIM), act)
    p_m0 = jnp.zeros((bp, ML_HEADS), act)
    p_mbuf0 = jnp.zeros((bp, CONV_W - 1, GROUP_W), act)

    hp, hs = x_prompt, x_sample
    p_states, s_states, p_mks, p_mvs = [], [], [], []
    for l in range(DEPTH):
        lp = {
            'norm_g': norm_g[l], 'w_in': w_in[l], 'conv_a_w': conv_a_w[l], 'conv_a_b': conv_a_b[l],
            'dt_bias': dt_bias[l], 'a_log': a_log[l], 'd_skip': d_skip[l], 'ssd_norm_g': ssd_norm_g[l],
            'conv_b_w': conv_b_w[l], 'conv_b_b': conv_b_b[l], 'w_mq': w_mq[l], 'w_mk': w_mk[l],
            'w_mv': w_mv[l], 'b_i': b_i[l], 'b_f': b_f[l], 'ml_norm_g': ml_norm_g[l],
            'q_norm_g': q_norm_g[l], 'kv_norm_g': kv_norm_g[l], 'w_uq': w_uq[l], 'w_uk': w_uk[l],
            'w_uv': w_uv[l], 'qn_nope_g': qn_nope_g[l], 'qn_rope_g': qn_rope_g[l],
            'kn_nope_g': kn_nope_g[l], 'kn_rope_g': kn_rope_g[l], 'xq_norm_g': xq_norm_g[l],
            'w_out': w_out[l],
        }
        mem_n = rms_norm(mem_prompt, mem_norm_g[l])
        mk = rms_norm((mem_n @ w_mem_k[l]).reshape(bp, mem_len, X_HEADS, X_HEAD_DIM), xk_norm_g[l])
        mv = (mem_n @ w_mem_v[l]).reshape(bp, mem_len, X_HEADS, X_HEAD_DIM)
        hp, st_p = mixer_layer(lp, hp, pos_p, pos_p, mk, mv, None, None,
                               p_h0, p_sbuf0, p_c0, p_n0, p_m0, p_mbuf0)
        p_states.append(st_p)
        p_mks.append(mk)
        p_mvs.append(mv)
        past_lat = cache_mla_latent[l, page_table].reshape(bs, past_len, KV_RANK)
        past_kr = cache_mla_krope[l, page_table].reshape(bs, past_len, MLA_ROPE)
        hs, st_s = mixer_layer(lp, hs, pos_s, kpos_s, cache_mem_k[l], cache_mem_v[l], past_lat, past_kr,
                               state_ssm[l], state_ssm_conv[l], state_mlstm_C[l], state_mlstm_n[l],
                               state_mlstm_m[l], state_mlstm_conv[l])
        s_states.append(st_s)

    y_prompt = rms_norm(hp, final_norm_g)
    y_sample = rms_norm(hs, final_norm_g)
    (p_lat, p_kr, p_ssm, p_sconv, p_mc, p_mn, p_mm, p_mconv) = [
        jnp.stack([st[i] for st in p_states]) for i in range(8)]
    (s_lat, s_kr, s_ssm, s_sconv, s_mc, s_mn, s_mm, s_mconv) = [
        jnp.stack([st[i] for st in s_states]) for i in range(8)]
    p_mem_k = jnp.stack(p_mks)
    p_mem_v = jnp.stack(p_mvs)
    return (y_prompt, y_sample, p_lat, p_kr, p_mem_k, p_mem_v, p_ssm, p_sconv, p_mc, p_mn, p_mm, p_mconv,
            s_lat, s_kr, s_ssm, s_sconv, s_mc, s_mn, s_mm, s_mconv)
```

```python
import functools
import math

import jax
import jax.numpy as jnp
from jax import lax
from jax.experimental import pallas as pl
from jax.experimental.pallas import tpu as pltpu

F32 = jnp.float32
BF16 = jnp.bfloat16
HIGHEST = lax.Precision.HIGHEST

D_MODEL = 1024
D_MIX = 2 * D_MODEL
GROUP_W = D_MIX // 4
SSD_HEAD_DIM = 64
SSD_HEADS = GROUP_W // SSD_HEAD_DIM
SSD_GROUPS = 2
SSD_HPG = SSD_HEADS // SSD_GROUPS
SSD_STATE = 128
CONV_W = 4
SSD_CONV_CH = GROUP_W + 2 * SSD_GROUPS * SSD_STATE
ML_HEADS = 4
ML_HEAD_DIM = GROUP_W // ML_HEADS
MLA_HEADS = 4
MLA_NOPE = 128
MLA_ROPE = 64
MLA_QK = MLA_NOPE + MLA_ROPE
MLA_V = GROUP_W // MLA_HEADS
Q_RANK = 3 * D_MODEL // 8
KV_RANK = D_MODEL // 4
ROPE_BASE = 10000.0
X_HEADS = 4
X_HEAD_DIM = GROUP_W // X_HEADS
CHUNK = 128
EPS = 1e-6
IN_SIZES = (D_MIX, SSD_CONV_CH, SSD_HEADS, GROUP_W, ML_HEADS, ML_HEADS, GROUP_W, Q_RANK, KV_RANK, MLA_ROPE, GROUP_W)

LANES = 128
SUBLANES = 8
VMEM_LIMIT = 56 * 1024 * 1024

MISC_W = LANES
MISC_KR = 0
MISC_DT = MLA_ROPE
MISC_I = MISC_DT + SSD_HEADS
MISC_F = MISC_I + ML_HEADS
PACK_SEGS = (("z", D_MIX), ("xbc", SSD_CONV_CH), ("xm", GROUP_W), ("o", GROUP_W), ("cq", Q_RANK),
             ("ckv", KV_RANK), ("xq", GROUP_W), ("misc", MISC_W))
PACK_W = sum(w for _, w in PACK_SEGS)


def _cparams(sem):
    return pltpu.CompilerParams(dimension_semantics=sem, vmem_limit_bytes=VMEM_LIMIT)


def _dot(a, b):
    return jnp.dot(a, b, preferred_element_type=F32)


def _dot_nt(a, b, precision=None):
    return lax.dot_general(a, b, (((1,), (1,)), ((), ())), preferred_element_type=F32, precision=precision)


def _dot_tn(a, b):
    return lax.dot_general(a, b, (((0,), (0,)), ((), ())), preferred_element_type=F32)


def _rms(x, g):
    return x * lax.rsqrt(jnp.mean(x * x, axis=-1, keepdims=True) + EPS) * g


def _silu(x):
    return x * jax.nn.sigmoid(x)


def _softplus(x):
    return jnp.maximum(x, 0.0) + jnp.log1p(jnp.exp(-jnp.abs(x)))


def _const_spec(shape):
    nd = len(shape)
    return pl.BlockSpec(shape, lambda *_: (0,) * nd)


def _inproj_kernel(x_ref, g_ref, w_ref, *out_refs):
    x = x_ref[...]
    u = _rms(x, g_ref[...]).astype(BF16)
    off = 0
    for ref, (_, width) in zip(out_refs, PACK_SEGS):
        ref[...] = _dot(u, w_ref[:, off:off + width])
        off += width


def _inproj(x2d, g, w_packed, tm):
    n = x2d.shape[0]
    out_shape = [jax.ShapeDtypeStruct((n, w), F32) for _, w in PACK_SEGS]
    out_specs = [pl.BlockSpec((tm, w), lambda i: (i, 0)) for _, w in PACK_SEGS]
    return pl.pallas_call(
        _inproj_kernel,
        grid=(n // tm,),
        in_specs=[pl.BlockSpec((tm, D_MODEL), lambda i: (i, 0)),
                  _const_spec((1, D_MODEL)),
                  pl.BlockSpec((D_MODEL, PACK_W), lambda i: (0, 0), pipeline_mode=pl.Buffered(1))],
        out_specs=out_specs,
        out_shape=out_shape,
        compiler_params=_cparams(("parallel",)),
        name="inproj",
    )(x2d, g, w_packed)


def _tri_masks():
    r = lax.broadcasted_iota(jnp.int32, (CHUNK, CHUNK), 0)
    c = lax.broadcasted_iota(jnp.int32, (CHUNK, CHUNK), 1)
    return r >= c, c >= r


def _ssd_kernel(xbc_ref, misc_ref, za_ref, cw_ref, cb_ref, dtb_row_ref, alog_row_ref, dtb_col_ref,
                alog_col_ref, dskip_ref, ng_ref, ya_ref, hfin_ref, conv_ref, h_scr, ext_scr, y_scr):
    c = pl.program_id(1)
    nc = pl.num_programs(1)

    @pl.when(c == 0)
    def _():
        h_scr[...] = jnp.zeros_like(h_scr)
        ext_scr[0:SUBLANES, :] = jnp.zeros((SUBLANES, SSD_CONV_CH), F32)

    u = xbc_ref[0]
    ext_scr[SUBLANES:SUBLANES + CHUNK, :] = u
    cw = cw_ref[...]
    conv = cb_ref[...] + cw[3:4] * u
    for j in range(CONV_W - 1):
        back = CONV_W - 1 - j
        conv = conv + cw[j:j + 1] * ext_scr[SUBLANES - back:SUBLANES - back + CHUNK, :]
    ext_scr[0:SUBLANES, :] = u[CHUNK - SUBLANES:CHUNK, :]
    xbcs = _silu(conv)
    xa = xbcs[:, 0:GROUP_W]

    misc = misc_ref[0]
    misc_t = misc.T
    lower, upper = _tri_masks()
    lower_f = lower.astype(F32)
    upper_f = upper.astype(F32)
    dt_col = _softplus(misc + dtb_row_ref[...])
    dt_row = _softplus(misc_t[MISC_DT:MISC_DT + SSD_HEADS, :] + dtb_col_ref[...])
    a_row = -jnp.exp(alog_row_ref[...])
    a_col = -jnp.exp(alog_col_ref[...])
    acum_col = jnp.dot(lower_f, dt_col * a_row, precision=HIGHEST, preferred_element_type=F32)
    acum_row = jnp.dot(dt_row * a_col, upper_f, precision=HIGHEST, preferred_element_type=F32)

    for g in range(SSD_GROUPS):
        b_g = xbcs[:, GROUP_W + g * SSD_STATE:GROUP_W + (g + 1) * SSD_STATE].astype(BF16)
        c_off = GROUP_W + SSD_GROUPS * SSD_STATE
        c_g = xbcs[:, c_off + g * SSD_STATE:c_off + (g + 1) * SSD_STATE].astype(BF16)
        cb = _dot_nt(c_g, b_g)
        for hh in range(SSD_HPG):
            h = g * SSD_HPG + hh
            ac_col = acum_col[:, MISC_DT + h:MISC_DT + h + 1]
            ac_row = acum_row[h:h + 1, :]
            seg = jnp.where(lower, ac_col - ac_row, -jnp.inf)
            w = cb * jnp.exp(seg) * dt_row[h:h + 1, :]
            x_h = xa[:, h * SSD_HEAD_DIM:(h + 1) * SSD_HEAD_DIM]
            hs = h_scr[h]
            y_h = _dot(w.astype(BF16), x_h.astype(BF16)) + _dot_nt(c_g, hs.astype(BF16)) * jnp.exp(ac_col)
            y_scr[:, h * SSD_HEAD_DIM:(h + 1) * SSD_HEAD_DIM] = y_h
            ac_last = ac_col[CHUNK - 1:CHUNK, :]
            wend = jnp.exp(ac_last - ac_col) * dt_col[:, MISC_DT + h:MISC_DT + h + 1]
            xw = (x_h * wend).astype(BF16)
            h_scr[h] = jnp.exp(ac_last) * hs + _dot_tn(xw, b_g)

    y = y_scr[...] + dskip_ref[...] * xa
    y = y * _silu(za_ref[0])
    gw = GROUP_W // SSD_GROUPS
    ng = ng_ref[...]
    for g in range(SSD_GROUPS):
        ya_ref[0, :, g * gw:(g + 1) * gw] = _rms(y[:, g * gw:(g + 1) * gw], ng[:, g * gw:(g + 1) * gw])

    @pl.when(c == nc - 1)
    def _():
        hfin_ref[0] = h_scr[...]
        conv_ref[0] = xbc_ref[0, CHUNK - (CONV_W - 1):CHUNK, :]


def _lane_pad(v, offset):
    return jnp.zeros((1, LANES), F32).at[0, offset:offset + v.shape[0]].set(v)


def _ssd_prompt(xbc, misc, z, lp):
    b, l, _ = xbc.shape
    nc = l // CHUNK
    row = lambda bi, ci: (bi, ci, 0)
    return pl.pallas_call(
        _ssd_kernel,
        grid=(b, nc),
        in_specs=[pl.BlockSpec((1, CHUNK, SSD_CONV_CH), row),
                  pl.BlockSpec((1, CHUNK, MISC_W), row),
                  pl.BlockSpec((1, CHUNK, GROUP_W), lambda bi, ci: (bi, ci, 0)),
                  _const_spec((CONV_W, SSD_CONV_CH)), _const_spec((1, SSD_CONV_CH)),
                  _const_spec((1, LANES)), _const_spec((1, LANES)),
                  _const_spec((SSD_HEADS, 1)), _const_spec((SSD_HEADS, 1)),
                  _const_spec((1, GROUP_W)), _const_spec((1, GROUP_W))],
        out_specs=[pl.BlockSpec((1, CHUNK, GROUP_W), row),
                   pl.BlockSpec((1, SSD_HEADS, SSD_HEAD_DIM, SSD_STATE), lambda bi, ci: (bi, 0, 0, 0)),
                   pl.BlockSpec((1, CONV_W - 1, SSD_CONV_CH), lambda bi, ci: (bi, 0, 0))],
        out_shape=[jax.ShapeDtypeStruct((b, l, GROUP_W), F32),
                   jax.ShapeDtypeStruct((b, SSD_HEADS, SSD_HEAD_DIM, SSD_STATE), F32),
                   jax.ShapeDtypeStruct((b, CONV_W - 1, SSD_CONV_CH), F32)],
        scratch_shapes=[pltpu.VMEM((SSD_HEADS, SSD_HEAD_DIM, SSD_STATE), F32),
                        pltpu.VMEM((SUBLANES + CHUNK, SSD_CONV_CH), F32),
                        pltpu.VMEM((CHUNK, GROUP_W), F32)],
        compiler_params=_cparams(("parallel", "arbitrary")),
        name="ssd_prompt",
    )(xbc, misc, z, lp["conv_a_w"], lp["conv_a_b"], lp["dtb_row"], lp["alog_row"], lp["dtb_col"],
      lp["alog_col"], lp["dskip_row"], lp["ssd_norm_g"])


def _mlstm_kernel(xm_ref, o_ref, zb_ref, misc_ref, cw_ref, cb_ref, wq_ref, wk_ref, wv_ref, gb_row_ref,
                  gb_col_ref, ng_ref, yb_ref, cfin_ref, nfin_ref, mfin_ref, conv_ref,
                  c_scr, n_scr, m_scr, ext_scr):
    c = pl.program_id(1)
    nc = pl.num_programs(1)

    @pl.when(c == 0)
    def _():
        c_scr[...] = jnp.zeros_like(c_scr)
        n_scr[...] = jnp.zeros_like(n_scr)
        m_scr[...] = jnp.zeros_like(m_scr)
        ext_scr[0:SUBLANES, :] = jnp.zeros((SUBLANES, GROUP_W), F32)

    u = xm_ref[0]
    ext_scr[SUBLANES:SUBLANES + CHUNK, :] = u
    cw = cw_ref[...]
    conv = cb_ref[...] + cw[3:4] * u
    for j in range(CONV_W - 1):
        back = CONV_W - 1 - j
        conv = conv + cw[j:j + 1] * ext_scr[SUBLANES - back:SUBLANES - back + CHUNK, :]
    ext_scr[0:SUBLANES, :] = u[CHUNK - SUBLANES:CHUNK, :]
    xmc = _silu(conv)

    misc = misc_ref[0]
    misc_t = misc.T
    lower, upper = _tri_masks()
    gates_col = misc + gb_row_ref[...]
    gates_row = misc_t[MISC_I:MISC_I + 2 * ML_HEADS, :] + gb_col_ref[...]
    lf_col = -_softplus(-gates_col)
    lf_row = -_softplus(-gates_row)
    bcum_col = jnp.dot(lower.astype(F32), lf_col, precision=HIGHEST, preferred_element_type=F32)
    bcum_row = jnp.dot(lf_row, upper.astype(F32), precision=HIGHEST, preferred_element_type=F32)
    ng = ng_ref[...]
    o_all = o_ref[0]
    zb = zb_ref[0]

    for h in range(ML_HEADS):
        sl = slice(h * ML_HEAD_DIM, (h + 1) * ML_HEAD_DIM)
        li_r = gates_row[h:h + 1, :]
        bc_r = bcum_row[ML_HEADS + h:ML_HEADS + h + 1, :]
        bc_c = bcum_col[:, MISC_F + h:MISC_F + h + 1]
        li_c = gates_col[:, MISC_I + h:MISC_I + h + 1]
        m_prev = m_scr[h:h + 1, 0:1]
        lw = jnp.where(lower, bc_c - bc_r + li_r, -jnp.inf)
        l_prev = bc_c + m_prev
        m_t = jnp.maximum(l_prev, jnp.max(lw, axis=1, keepdims=True))
        xmc_h = xmc[:, sl].astype(BF16)
        q = _dot(xmc_h, wq_ref[h])
        k = _dot(xmc_h, wk_ref[h]) * (ML_HEAD_DIM ** -0.5)
        v = _dot(u[:, sl].astype(BF16), wv_ref[h])
        qb = q.astype(BF16)
        kb = k.astype(BF16)
        sc = _dot_nt(qb, kb) * jnp.exp(lw - m_t)
        g_prev = jnp.exp(l_prev - m_t)
        cs = c_scr[h]
        ns = n_scr[h:h + 1, :]
        num = _dot(sc.astype(BF16), v.astype(BF16)) + g_prev * _dot_nt(qb, cs.astype(BF16))
        den = jnp.sum(sc, axis=1, keepdims=True) + g_prev * jnp.sum(q * ns, axis=1, keepdims=True)
        hout = num / jnp.maximum(jnp.abs(den), jnp.exp(-m_t))
        m_new = m_t[CHUNK - 1:CHUNK, :]
        bc_last = bc_c[CHUNK - 1:CHUNK, :]
        wend = jnp.exp(bc_last - bc_c + li_c - m_new)
        decay = jnp.exp(bc_last + m_prev - m_new)
        c_scr[h] = decay * cs + _dot_tn((wend * v).astype(BF16), kb)
        n_scr[h:h + 1, :] = decay * ns + jnp.sum(wend * k, axis=0, keepdims=True)
        m_scr[h:h + 1, :] = jnp.broadcast_to(m_new, (1, LANES))
        hb = jax.nn.sigmoid(o_all[:, sl]) * hout
        yb_ref[0, :, sl] = _rms(hb, ng[:, sl]) * _silu(zb[:, sl])

    @pl.when(c == nc - 1)
    def _():
        cfin_ref[0] = c_scr[...]
        nfin_ref[0] = n_scr[...]
        mfin_ref[0] = m_scr[...]
        conv_ref[0] = xm_ref[0, CHUNK - (CONV_W - 1):CHUNK, :]


def _mlstm_prompt(xm, o, z, misc, lp):
    b, l, _ = xm.shape
    nc = l // CHUNK
    row = lambda bi, ci: (bi, ci, 0)
    outs = pl.pallas_call(
        _mlstm_kernel,
        grid=(b, nc),
        in_specs=[pl.BlockSpec((1, CHUNK, GROUP_W), row),
                  pl.BlockSpec((1, CHUNK, GROUP_W), row),
                  pl.BlockSpec((1, CHUNK, GROUP_W), lambda bi, ci: (bi, ci, 1)),
                  pl.BlockSpec((1, CHUNK, MISC_W), row),
                  _const_spec((CONV_W, GROUP_W)), _const_spec((1, GROUP_W)),
                  _const_spec((ML_HEADS, ML_HEAD_DIM, ML_HEAD_DIM)),
                  _const_spec((ML_HEADS, ML_HEAD_DIM, ML_HEAD_DIM)),
                  _const_spec((ML_HEADS, ML_HEAD_DIM, ML_HEAD_DIM)),
                  _const_spec((1, LANES)), _const_spec((2 * ML_HEADS, 1)), _const_spec((1, GROUP_W))],
        out_specs=[pl.BlockSpec((1, CHUNK, GROUP_W), row),
                   pl.BlockSpec((1, ML_HEADS, ML_HEAD_DIM, ML_HEAD_DIM), lambda bi, ci: (bi, 0, 0, 0)),
                   pl.BlockSpec((1, SUBLANES, LANES), lambda bi, ci: (bi, 0, 0)),
                   pl.BlockSpec((1, SUBLANES, LANES), lambda bi, ci: (bi, 0, 0)),
                   pl.BlockSpec((1, CONV_W - 1, GROUP_W), lambda bi, ci: (bi, 0, 0))],
        out_shape=[jax.ShapeDtypeStruct((b, l, GROUP_W), F32),
                   jax.ShapeDtypeStruct((b, ML_HEADS, ML_HEAD_DIM, ML_HEAD_DIM), F32),
                   jax.ShapeDtypeStruct((b, SUBLANES, LANES), F32),
                   jax.ShapeDtypeStruct((b, SUBLANES, LANES), F32),
                   jax.ShapeDtypeStruct((b, CONV_W - 1, GROUP_W), F32)],
        scratch_shapes=[pltpu.VMEM((ML_HEADS, ML_HEAD_DIM, ML_HEAD_DIM), F32),
                        pltpu.VMEM((SUBLANES, LANES), F32),
                        pltpu.VMEM((SUBLANES, LANES), F32),
                        pltpu.VMEM((SUBLANES + CHUNK, GROUP_W), F32)],
        compiler_params=_cparams(("parallel", "arbitrary")),
        name="mlstm_prompt",
    )(xm, o, z, misc, lp["conv_b_w"], lp["conv_b_b"], lp["w_mq"], lp["w_mk"], lp["w_mv"],
      lp["gb_row"], lp["gb_col"], lp["ml_norm_g"])
    yb, cfin, nfin, mfin, mconv = outs
    return yb, cfin, nfin[:, :ML_HEADS, :], mfin[:, :ML_HEADS, 0], mconv


def _rope_rows(r, cos2, sinsgn, swap):
    r_sw = jnp.dot(r, swap, precision=HIGHEST, preferred_element_type=F32)
    return r * cos2 + r_sw * sinsgn


def _mla_prep_kernel(cq_ref, ckv_ref, misc_ref, cos_ref, sin_ref, swap_ref, qg_ref, kvg_ref, wqn_ref, wqr_ref,
                     wuk_ref, qnn_ref, qnr_ref, knn_ref, knr_ref, q_ref, k_ref, latb_ref, lat_ref, kr_ref):
    cos2 = cos_ref[...]
    sinsgn = sin_ref[...]
    swap = swap_ref[...]
    c_q = _rms(cq_ref[0], qg_ref[...]).astype(BF16)
    qn_all = _dot(c_q, wqn_ref[...])
    c_kv = _rms(ckv_ref[0], kvg_ref[...])
    lat_ref[0] = c_kv
    c_kv_b = c_kv.astype(BF16)
    latb_ref[0] = c_kv_b
    kr = _rope_rows(_rms(misc_ref[0][:, MISC_KR:MISC_KR + MLA_ROPE], knr_ref[...]), cos2, sinsgn, swap)
    kr_ref[0] = kr
    kr_b = kr.astype(BF16)
    kn_all = _dot(c_kv_b, wuk_ref[...])
    for h in range(MLA_HEADS):
        sl = slice(h * MLA_NOPE, (h + 1) * MLA_NOPE)
        q_ref[0, h, :, 0:MLA_NOPE] = _rms(qn_all[:, sl], qnn_ref[...]).astype(BF16)
        qr = _rope_rows(_rms(_dot(c_q, wqr_ref[h]), qnr_ref[...]), cos2, sinsgn, swap)
        q_ref[0, h, :, MLA_NOPE:MLA_QK] = qr.astype(BF16)
        k_ref[0, h, :, 0:MLA_NOPE] = _rms(kn_all[:, sl], knn_ref[...]).astype(BF16)
        k_ref[0, h, :, MLA_NOPE:MLA_QK] = kr_b


def _mla_prep(cq, ckv, misc, cos2, sinsgn, lp, tm):
    b, l, _ = cq.shape
    row = lambda bi, ti: (bi, ti, 0)
    hrow = lambda bi, ti: (bi, 0, ti, 0)
    return pl.pallas_call(
        _mla_prep_kernel,
        grid=(b, l // tm),
        in_specs=[pl.BlockSpec((1, tm, Q_RANK), row), pl.BlockSpec((1, tm, KV_RANK), row),
                  pl.BlockSpec((1, tm, MISC_W), row),
                  pl.BlockSpec((tm, MLA_ROPE), lambda bi, ti: (ti, 0)),
                  pl.BlockSpec((tm, MLA_ROPE), lambda bi, ti: (ti, 0)),
                  _const_spec((MLA_ROPE, MLA_ROPE)),
                  _const_spec((1, Q_RANK)), _const_spec((1, KV_RANK)),
                  _const_spec((Q_RANK, MLA_HEADS * MLA_NOPE)),
                  _const_spec((MLA_HEADS, Q_RANK, MLA_ROPE)),
                  _const_spec((KV_RANK, MLA_HEADS * MLA_NOPE)),
                  _const_spec((1, MLA_NOPE)), _const_spec((1, MLA_ROPE)),
                  _const_spec((1, MLA_NOPE)), _const_spec((1, MLA_ROPE))],
        out_specs=[pl.BlockSpec((1, MLA_HEADS, tm, MLA_QK), hrow),
                   pl.BlockSpec((1, MLA_HEADS, tm, MLA_QK), hrow),
                   pl.BlockSpec((1, tm, KV_RANK), row),
                   pl.BlockSpec((1, tm, KV_RANK), row),
                   pl.BlockSpec((1, tm, MLA_ROPE), row)],
        out_shape=[jax.ShapeDtypeStruct((b, MLA_HEADS, l, MLA_QK), BF16),
                   jax.ShapeDtypeStruct((b, MLA_HEADS, l, MLA_QK), BF16),
                   jax.ShapeDtypeStruct((b, l, KV_RANK), BF16),
                   jax.ShapeDtypeStruct((b, l, KV_RANK), F32),
                   jax.ShapeDtypeStruct((b, l, MLA_ROPE), F32)],
        compiler_params=_cparams(("parallel", "parallel")),
        name="mla_prep",
    )(cq, ckv, misc, cos2, sinsgn, lp["rope_swap"], lp["q_norm_g"], lp["kv_norm_g"], lp["w_uq_nope"],
      lp["w_uq_rope"], lp["w_uk"], lp["qn_nope_g"], lp["qn_rope_g"], lp["kn_nope_g"], lp["kn_rope_g"])


def _mla_attn_kernel(q_ref, k_ref, v_ref, zc_ref, wuv_ref, out_ref, m_scr, l_scr, acc_scr, *, tq, tk):
    qi = pl.program_id(2)
    q = q_ref[0, 0]
    scale = MLA_QK ** -0.5
    m_scr[...] = jnp.full_like(m_scr, -jnp.inf)
    l_scr[...] = jnp.zeros_like(l_scr)
    acc_scr[...] = jnp.zeros_like(acc_scr)
    row_ids = qi * tq + lax.broadcasted_iota(jnp.int32, (tq, tk), 0)
    col_base = lax.broadcasted_iota(jnp.int32, (tq, tk), 1)

    def body(j, carry):
        start = pl.multiple_of(j * tk, tk)
        k = k_ref[0, 0, pl.ds(start, tk), :]
        v = v_ref[0, pl.ds(start, tk), :]
        s = _dot_nt(q, k) * scale
        s = jnp.where(col_base + j * tk <= row_ids, s, -jnp.inf)
        m_prev = m_scr[...]
        m_new = jnp.maximum(m_prev, jnp.max(s, axis=1, keepdims=True))
        alpha = jnp.exp(m_prev - m_new)
        p = jnp.exp(s - m_new)
        l_scr[...] = alpha * l_scr[...] + jnp.sum(p, axis=1, keepdims=True)
        acc_scr[...] = alpha * acc_scr[...] + _dot(p.astype(BF16), v)
        m_scr[...] = m_new
        return carry

    n_kv = ((qi + 1) * tq + tk - 1) // tk
    lax.fori_loop(0, n_kv, body, 0)
    ctx = acc_scr[...] / l_scr[...]
    y = _dot(ctx.astype(BF16), wuv_ref[...])
    out_ref[0] = y * _silu(zc_ref[0])


def _mla_attn_prompt(q, k, latb, z, w_uv, tq=256, tk=256):
    b, hds, l, _ = q.shape
    kern = functools.partial(_mla_attn_kernel, tq=tq, tk=tk)
    zc_block = 2 * (GROUP_W // MLA_V)
    return pl.pallas_call(
        kern,
        grid=(b, hds, l // tq),
        in_specs=[pl.BlockSpec((1, 1, tq, MLA_QK), lambda bi, hi, qi: (bi, hi, qi, 0)),
                  pl.BlockSpec((1, 1, l, MLA_QK), lambda bi, hi, qi: (bi, hi, 0, 0)),
                  pl.BlockSpec((1, l, KV_RANK), lambda bi, hi, qi: (bi, 0, 0)),
                  pl.BlockSpec((1, tq, MLA_V), lambda bi, hi, qi: (bi, qi, zc_block + hi)),
                  pl.BlockSpec((KV_RANK, MLA_V), lambda bi, hi, qi: (0, hi))],
        out_specs=pl.BlockSpec((1, tq, MLA_V), lambda bi, hi, qi: (bi, qi, hi)),
        out_shape=jax.ShapeDtypeStruct((b, l, GROUP_W), F32),
        scratch_shapes=[pltpu.VMEM((tq, 1), F32), pltpu.VMEM((tq, 1), F32), pltpu.VMEM((tq, KV_RANK), F32)],
        compiler_params=_cparams(("parallel", "parallel", "arbitrary")),
        name="mla_attn",
    )(q, k, latb, z, w_uv)


def _memkv_kernel(mem_ref, g_ref, wk_ref, wv_ref, xkg_ref, mk_ref, mv_ref):
    mem_n = _rms(mem_ref[0], g_ref[0]).astype(BF16)
    kraw = _dot(mem_n, wk_ref[0])
    for h in range(X_HEADS):
        sl = slice(h * X_HEAD_DIM, (h + 1) * X_HEAD_DIM)
        mk_ref[0, 0, :, sl] = _rms(kraw[:, sl], xkg_ref[0])
    mv_ref[0, 0] = _dot(mem_n, wv_ref[0])


def _memkv(mem, mem_norm_g, w_mem_k, w_mem_v, xk_norm_g):
    b, m, _ = mem.shape
    depth = w_mem_k.shape[0]
    lsel = lambda li, bi: (li, 0, 0)
    return pl.pallas_call(
        _memkv_kernel,
        grid=(depth, b),
        in_specs=[pl.BlockSpec((1, m, D_MODEL), lambda li, bi: (bi, 0, 0)),
                  pl.BlockSpec((1, 1, D_MODEL), lsel),
                  pl.BlockSpec((1, D_MODEL, GROUP_W), lsel),
                  pl.BlockSpec((1, D_MODEL, GROUP_W), lsel),
                  pl.BlockSpec((1, 1, X_HEAD_DIM), lsel)],
        out_specs=[pl.BlockSpec((1, 1, m, GROUP_W), lambda li, bi: (li, bi, 0, 0)),
                   pl.BlockSpec((1, 1, m, GROUP_W), lambda li, bi: (li, bi, 0, 0))],
        out_shape=[jax.ShapeDtypeStruct((depth, b, m, GROUP_W), F32),
                   jax.ShapeDtypeStruct((depth, b, m, GROUP_W), F32)],
        compiler_params=_cparams(("parallel", "parallel")),
        name="memkv",
    )(mem, mem_norm_g, w_mem_k, w_mem_v, xk_norm_g)


def _xattn_kernel(xq_ref, mk_ref, mv_ref, zd_ref, g_ref, out_ref, *, rows):
    xq = xq_ref[0]
    zd = zd_ref[0]
    if rows < SUBLANES:
        xq = jnp.broadcast_to(xq[0:1], (SUBLANES, GROUP_W))
    for h in range(X_HEADS):
        sl = slice(h * X_HEAD_DIM, (h + 1) * X_HEAD_DIM)
        qn = _rms(xq[:, sl], g_ref[...]).astype(BF16)
        s = _dot_nt(qn, mk_ref[0, :, sl].astype(BF16)) * (X_HEAD_DIM ** -0.5)
        s = s - jnp.max(s, axis=1, keepdims=True)
        e = jnp.exp(s)
        p = e / jnp.sum(e, axis=1, keepdims=True)
        y = _dot(p.astype(BF16), mv_ref[0, :, sl].astype(BF16))
        out_ref[0, :, sl] = y[0:rows] * _silu(zd[:, sl])


def _xattn(xq, mk, mv, z, g, tm):
    b, l, _ = xq.shape
    m = mk.shape[1]
    kern = functools.partial(_xattn_kernel, rows=tm)
    return pl.pallas_call(
        kern,
        grid=(b, l // tm),
        in_specs=[pl.BlockSpec((1, tm, GROUP_W), lambda bi, ti: (bi, ti, 0)),
                  pl.BlockSpec((1, m, GROUP_W), lambda bi, ti: (bi, 0, 0)),
                  pl.BlockSpec((1, m, GROUP_W), lambda bi, ti: (bi, 0, 0)),
                  pl.BlockSpec((1, tm, GROUP_W), lambda bi, ti: (bi, ti, 3)),
                  _const_spec((1, X_HEAD_DIM))],
        out_specs=pl.BlockSpec((1, tm, GROUP_W), lambda bi, ti: (bi, ti, 0)),
        out_shape=jax.ShapeDtypeStruct((b, l, GROUP_W), F32),
        compiler_params=_cparams(("parallel", "parallel")),
        name="xattn",
    )(xq, mk, mv, z, g)


def _outproj_kernel(ya_ref, yb_ref, yc_ref, yd_ref, x_ref, w_ref, fg_ref, out_ref, *, final):
    acc = x_ref[...]
    for i, ref in enumerate((ya_ref, yb_ref, yc_ref, yd_ref)):
        acc = acc + _dot(ref[...].astype(BF16), w_ref[i * GROUP_W:(i + 1) * GROUP_W, :])
    if final:
        acc = _rms(acc, fg_ref[...])
    out_ref[...] = acc


def _outproj(ys, x2d, w_out, final_g, tm, final):
    n = x2d.shape[0]
    kern = functools.partial(_outproj_kernel, final=final)
    yspec = pl.BlockSpec((tm, GROUP_W), lambda i: (i, 0))
    return pl.pallas_call(
        kern,
        grid=(n // tm,),
        in_specs=[yspec, yspec, yspec, yspec,
                  pl.BlockSpec((tm, D_MODEL), lambda i: (i, 0)),
                  _const_spec((D_MIX, D_MODEL)), _const_spec((1, D_MODEL))],
        out_specs=pl.BlockSpec((tm, D_MODEL), lambda i: (i, 0)),
        out_shape=jax.ShapeDtypeStruct((n, D_MODEL), F32),
        compiler_params=_cparams(("parallel",)),
        name="outproj",
    )(*ys, x2d, w_out, final_g)


def _col_form(row, n):
    return jnp.broadcast_to(row, (LANES, n)).T


def _dec_state_kernel(xbc_ref, xm_ref, o_ref, z_ref, misc_ref, hs_ref, sbuf_ref, cm_ref, nm_ref, mm_ref, mbuf_ref,
                      cwa_ref, cba_ref, dtb_ref, alog_ref, hexp_ref, dskip_ref, nga_ref,
                      cwb_ref, cbb_ref, wq_ref, wk_ref, wv_ref, gb_ref, ngb_ref,
                      ya_ref, yb_ref, hs_out, sbuf_out, cm_out, nm_out, mm_out, mbuf_out):
    misc = misc_ref[0]
    z = z_ref[0]

    u = xbc_ref[0]
    buf = sbuf_ref[0]
    cw = cwa_ref[...]
    conv = cba_ref[...] + cw[3:4] * u
    for j in range(CONV_W - 1):
        conv = conv + cw[j:j + 1] * buf[j:j + 1]
    sbuf_out[0, 0:CONV_W - 2, :] = buf[1:CONV_W - 1]
    sbuf_out[0, CONV_W - 2:CONV_W - 1, :] = u
    xbcs = _silu(conv)
    xa = xbcs[:, 0:GROUP_W]
    dt = _softplus(misc + dtb_ref[...])
    a = -jnp.exp(alog_ref[...])
    hexp = hexp_ref[...]
    dt8 = jnp.broadcast_to(dt, (SUBLANES, LANES))
    da8 = jnp.broadcast_to(dt * a, (SUBLANES, LANES))
    dt_x = jnp.dot(dt8, hexp, precision=HIGHEST, preferred_element_type=F32)[0:1]
    decay_x = jnp.exp(jnp.dot(da8, hexp, precision=HIGHEST, preferred_element_type=F32)[0:1])
    hstate = hs_ref[0].reshape(SSD_HEADS * SSD_HEAD_DIM, SSD_STATE)
    gw = GROUP_W // SSD_GROUPS
    c_off = GROUP_W + SSD_GROUPS * SSD_STATE
    dtx_col = _col_form(dt_x * xa, GROUP_W)
    decay_col = _col_form(decay_x, GROUP_W)
    y_parts = []
    for g in range(SSD_GROUPS):
        b_g = xbcs[:, GROUP_W + g * SSD_STATE:GROUP_W + (g + 1) * SSD_STATE]
        c_g = xbcs[:, c_off + g * SSD_STATE:c_off + (g + 1) * SSD_STATE]
        cb = jnp.sum(c_g * b_g, axis=1, keepdims=True)
        rows = slice(g * gw, (g + 1) * gw)
        h_g = hstate[rows]
        c8 = jnp.broadcast_to(c_g, (SUBLANES, SSD_STATE)).astype(BF16)
        ch = _dot_nt(c8, h_g.astype(BF16))[0:1]
        y_parts.append(cb * dt_x[:, rows] * xa[:, rows] + ch * decay_x[:, rows])
        h_new = decay_col[rows] * h_g + dtx_col[rows] * b_g
        hs_out[0, g * SSD_HPG:(g + 1) * SSD_HPG] = h_new.reshape(SSD_HPG, SSD_HEAD_DIM, SSD_STATE)
    y = jnp.concatenate(y_parts, axis=1) + dskip_ref[...] * xa
    y = y * _silu(z[:, 0:GROUP_W])
    nga = nga_ref[...]
    for g in range(SSD_GROUPS):
        rows = slice(g * gw, (g + 1) * gw)
        ya_ref[0, :, rows] = _rms(y[:, rows], nga[:, rows])

    um = xm_ref[0]
    mbuf = mbuf_ref[0]
    cwb = cwb_ref[...]
    convb = cbb_ref[...] + cwb[3:4] * um
    for j in range(CONV_W - 1):
        convb = convb + cwb[j:j + 1] * mbuf[j:j + 1]
    mbuf_out[0, 0:CONV_W - 2, :] = mbuf[1:CONV_W - 1]
    mbuf_out[0, CONV_W - 2:CONV_W - 1, :] = um
    xmc = _silu(convb)
    gates = misc + gb_ref[...]
    lf_all = -_softplus(-gates)
    mm = mm_ref[0]
    ngb = ngb_ref[...]
    o_all = o_ref[0]
    for h in range(ML_HEADS):
        sl = slice(h * ML_HEAD_DIM, (h + 1) * ML_HEAD_DIM)
        li = gates[:, MISC_I + h:MISC_I + h + 1]
        lf = lf_all[:, MISC_F + h:MISC_F + h + 1]
        m_prev = mm[:, h:h + 1]
        l_prev = lf + m_prev
        m_t = jnp.maximum(l_prev, li)
        xmc8 = jnp.broadcast_to(xmc[:, sl], (SUBLANES, ML_HEAD_DIM)).astype(BF16)
        um8 = jnp.broadcast_to(um[:, sl], (SUBLANES, ML_HEAD_DIM)).astype(BF16)
        q8 = _dot(xmc8, wq_ref[h])
        q = q8[0:1]
        k = _dot(xmc8, wk_ref[h])[0:1] * (ML_HEAD_DIM ** -0.5)
        v = _dot(um8, wv_ref[h])[0:1]
        qk = jnp.sum(q * k, axis=1, keepdims=True)
        sc = qk * jnp.exp(li - m_t)
        g_prev = jnp.exp(l_prev - m_t)
        cs = cm_ref[0, h]
        ns = nm_ref[0, h:h + 1, :]
        cq = _dot_nt(q8.astype(BF16), cs.astype(BF16))[0:1]
        num = sc * v + g_prev * cq
        den = sc + g_prev * jnp.sum(ns * q, axis=1, keepdims=True)
        hout = num / jnp.maximum(jnp.abs(den), jnp.exp(-m_t))
        wend = jnp.exp(li - m_t)
        decay = jnp.exp(l_prev - m_t)
        cm_out[0, h] = decay * cs + _col_form(wend * v, ML_HEAD_DIM) * k
        nm_out[0, h:h + 1, :] = decay * ns + wend * k
        mm_out[0, :, h:h + 1] = m_t
        hb = jax.nn.sigmoid(o_all[:, sl]) * hout
        yb_ref[0, :, sl] = _rms(hb, ngb[:, sl]) * _silu(z[:, GROUP_W + h * ML_HEAD_DIM:GROUP_W + (h + 1) * ML_HEAD_DIM])


def _dec_state(xbc, xm, o, z, misc, st, lp):
    b = xbc.shape[0]
    hs, sbuf, cm, nm, mm, mbuf = st
    mm3 = mm.reshape(b, 1, ML_HEADS)
    tok = lambda w: pl.BlockSpec((1, 1, w), lambda bi: (bi, 0, 0))
    s4 = lambda s: pl.BlockSpec((1,) + s, lambda bi: (bi, 0, 0, 0))
    s3 = lambda s: pl.BlockSpec((1,) + s, lambda bi: (bi, 0, 0))
    state_specs = [s4((SSD_HEADS, SSD_HEAD_DIM, SSD_STATE)), s3((CONV_W - 1, SSD_CONV_CH)),
                   s4((ML_HEADS, ML_HEAD_DIM, ML_HEAD_DIM)), s3((ML_HEADS, ML_HEAD_DIM)),
                   s3((1, ML_HEADS)), s3((CONV_W - 1, GROUP_W))]
    state_shapes = [jax.ShapeDtypeStruct(t.shape, F32) for t in (hs, sbuf, cm, nm, mm3, mbuf)]
    outs = pl.pallas_call(
        _dec_state_kernel,
        grid=(b,),
        in_specs=[tok(SSD_CONV_CH), tok(GROUP_W), tok(GROUP_W), tok(D_MIX), tok(MISC_W)] + state_specs + [
            _const_spec((CONV_W, SSD_CONV_CH)), _const_spec((1, SSD_CONV_CH)),
            _const_spec((1, LANES)), _const_spec((1, LANES)), _const_spec((LANES, GROUP_W)),
            _const_spec((1, GROUP_W)), _const_spec((1, GROUP_W)),
            _const_spec((CONV_W, GROUP_W)), _const_spec((1, GROUP_W)),
            _const_spec((ML_HEADS, ML_HEAD_DIM, ML_HEAD_DIM)),
            _const_spec((ML_HEADS, ML_HEAD_DIM, ML_HEAD_DIM)),
            _const_spec((ML_HEADS, ML_HEAD_DIM, ML_HEAD_DIM)),
            _const_spec((1, LANES)), _const_spec((1, GROUP_W))],
        out_specs=[tok(GROUP_W), tok(GROUP_W)] + state_specs,
        out_shape=[jax.ShapeDtypeStruct((b, 1, GROUP_W), F32)] * 2 + state_shapes,
        compiler_params=_cparams(("parallel",)),
        name="dec_state",
    )(xbc, xm, o, z, misc, hs, sbuf, cm, nm, mm3, mbuf,
      lp["conv_a_w"], lp["conv_a_b"], lp["dtb_row"], lp["alog_row"], lp["head_expand"], lp["dskip_row"],
      lp["ssd_norm_g"], lp["conv_b_w"], lp["conv_b_b"], lp["w_mq"], lp["w_mk"], lp["w_mv"], lp["gb_row"],
      lp["ml_norm_g"])
    ya, yb, hs_n, sbuf_n, cm_n, nm_n, mm_n, mbuf_n = outs
    return ya, yb, (hs_n, sbuf_n, cm_n, nm_n, mm_n.reshape(b, ML_HEADS), mbuf_n)


def _mla_decode_kernel(pt_ref, lat_hbm, kr_hbm, qn_ref, qr_ref, qfull_ref, kfull_ref, vnew_ref, zc_ref,
                       wukt_ref, wuv_ref, kng_ref, out_ref,
                       lat_buf, kr_buf, sem, qa_scr, m_scr, l_scr, acc_scr, *, layer, pps, page):
    b = pl.program_id(0)
    s = pl.program_id(1)
    ns = pl.num_programs(1)
    nb = pl.num_programs(0)
    step = b * ns + s
    slot = step % 2
    t = pps * page

    def copies(bb, ss, sl):
        out = []
        for p in range(pps):
            pg = pt_ref[bb, ss * pps + p]
            out.append(pltpu.make_async_copy(lat_hbm.at[layer, pg], lat_buf.at[sl, pl.ds(p * page, page), :],
                                             sem.at[0, sl]))
            out.append(pltpu.make_async_copy(kr_hbm.at[layer, pg], kr_buf.at[sl, pl.ds(p * page, page), :],
                                             sem.at[1, sl]))
        return out

    @pl.when(step == 0)
    def _():
        for cp in copies(b, s, slot):
            cp.start()

    @pl.when(step + 1 < nb * ns)
    def _():
        nxt = step + 1
        for cp in copies(nxt // ns, nxt % ns, 1 - slot):
            cp.start()

    scale = MLA_QK ** -0.5

    @pl.when(s == 0)
    def _():
        qng = (qn_ref[0].astype(F32) * kng_ref[...]).astype(BF16)
        for h in range(MLA_HEADS):
            full = _dot(qng, wukt_ref[h * MLA_NOPE:(h + 1) * MLA_NOPE, :])
            qa_scr[h:h + 1, :] = full[h:h + 1]
            qa_scr[MLA_HEADS + h:MLA_HEADS + h + 1, :] = full[h:h + 1]
        m_scr[...] = jnp.full_like(m_scr, -jnp.inf)
        l_scr[...] = jnp.zeros_like(l_scr)
        acc_scr[...] = jnp.zeros_like(acc_scr)

    for cp in copies(b, s, slot):
        cp.wait()

    latb = lat_buf[slot].astype(BF16)
    krb = kr_buf[slot].astype(BF16)
    kraw_t = _dot_nt(wukt_ref[...], latb)
    sa = _dot_nt(qa_scr[...].astype(BF16), latb)
    sr = _dot_nt(qr_ref[0], krb)
    rinv_rows = []
    for h in range(MLA_HEADS):
        kh = kraw_t[h * MLA_NOPE:(h + 1) * MLA_NOPE, :]
        ssq = jnp.sum(kh * kh, axis=0, keepdims=True)
        rinv_rows.append(lax.rsqrt(ssq * (1.0 / MLA_NOPE) + EPS))
    rinv = jnp.concatenate(rinv_rows + rinv_rows, axis=0)
    sc = (sa * rinv + sr) * scale
    m_prev = m_scr[...]
    m_new = jnp.maximum(m_prev, jnp.max(sc, axis=1, keepdims=True))
    alpha = jnp.exp(m_prev - m_new)
    p = jnp.exp(sc - m_new)
    l_scr[...] = alpha * l_scr[...] + jnp.sum(p, axis=1, keepdims=True)
    acc_scr[...] = alpha * acc_scr[...] + _dot(p.astype(BF16), latb)
    m_scr[...] = m_new

    @pl.when(s == ns - 1)
    def _():
        s_new = jnp.sum(qfull_ref[0].astype(F32) * kfull_ref[0].astype(F32), axis=1, keepdims=True) * scale
        m_prev2 = m_scr[...]
        m_fin = jnp.maximum(m_prev2, s_new)
        alpha2 = jnp.exp(m_prev2 - m_fin)
        p_new = jnp.exp(s_new - m_fin)
        l_fin = alpha2 * l_scr[...] + p_new
        acc = alpha2 * acc_scr[...] + p_new * vnew_ref[0].astype(F32)
        ctx = (acc / l_fin).astype(BF16)
        yfull = _dot(ctx, wuv_ref[...])
        parts = [yfull[h:h + 1, h * MLA_V:(h + 1) * MLA_V] for h in range(MLA_HEADS)]
        out_ref[0] = jnp.concatenate(parts, axis=1) * _silu(zc_ref[0])


def _mla_decode(page_table, cache_lat, cache_kr, layer, qn8, qr8, qfull8, kfull8, vnew, z, lp, pps=16):
    b, n_pages = page_table.shape
    page = cache_lat.shape[2]
    ns = n_pages // pps
    t = pps * page
    kern = functools.partial(_mla_decode_kernel, layer=layer, pps=pps, page=page)
    per_b = lambda shape: pl.BlockSpec((1,) + shape, lambda bi, si, pt: (bi, 0, 0))
    cst = lambda shape: pl.BlockSpec(shape, lambda bi, si, pt: (0,) * len(shape))
    grid_spec = pltpu.PrefetchScalarGridSpec(
        num_scalar_prefetch=1,
        grid=(b, ns),
        in_specs=[pl.BlockSpec(memory_space=pl.ANY), pl.BlockSpec(memory_space=pl.ANY),
                  per_b((SUBLANES, MLA_NOPE)), per_b((SUBLANES, MLA_ROPE)),
                  per_b((SUBLANES, MLA_QK)), per_b((SUBLANES, MLA_QK)), per_b((1, KV_RANK)),
                  pl.BlockSpec((1, 1, GROUP_W), lambda bi, si, pt: (bi, 0, 2)),
                  cst((MLA_HEADS * MLA_NOPE, KV_RANK)), cst((KV_RANK, GROUP_W)), cst((1, MLA_NOPE))],
        out_specs=pl.BlockSpec((1, 1, GROUP_W), lambda bi, si, pt: (bi, 0, 0)),
        scratch_shapes=[pltpu.VMEM((2, t, KV_RANK), F32), pltpu.VMEM((2, t, MLA_ROPE), F32),
                        pltpu.SemaphoreType.DMA((2, 2)),
                        pltpu.VMEM((SUBLANES, KV_RANK), F32),
                        pltpu.VMEM((SUBLANES, 1), F32), pltpu.VMEM((SUBLANES, 1), F32),
                        pltpu.VMEM((SUBLANES, KV_RANK), F32)],
    )
    return pl.pallas_call(
        kern,
        grid_spec=grid_spec,
        out_shape=jax.ShapeDtypeStruct((b, 1, GROUP_W), F32),
        compiler_params=_cparams(("arbitrary", "arbitrary")),
        name="mla_decode",
    )(page_table, cache_lat, cache_kr, qn8, qr8, qfull8, kfull8, vnew, z, lp["w_uk_t"], lp["w_uv"],
      lp["kn_nope_g"])


def _pack_w_in(w_in):
    names = ("z", "xbc", "dt", "xm", "i", "f", "o", "cq", "ckv", "kr", "xq")
    offs = {}
    o = 0
    for nme, w in zip(names, IN_SIZES):
        offs[nme] = (o, o + w)
        o += w
    col = lambda nme: w_in[:, :, offs[nme][0]:offs[nme][1]]
    pad = jnp.zeros(w_in.shape[:2] + (MISC_W - (MISC_F + ML_HEADS),), w_in.dtype)
    packed = jnp.concatenate([col("z"), col("xbc"), col("xm"), col("o"), col("cq"), col("ckv"), col("xq"),
                              col("kr"), col("dt"), col("i"), col("f"), pad], axis=-1)
    return packed.astype(BF16)


def _layer_params(l, p):
    depth_row = lambda a: a[l][None, :]
    lp = {}
    lp["norm_g"] = depth_row(p["norm_g"])
    lp["conv_a_w"] = p["conv_a_w"][l]
    lp["conv_a_b"] = depth_row(p["conv_a_b"])
    lp["dtb_row"] = _lane_pad(p["dt_bias"][l], MISC_DT)
    lp["alog_row"] = _lane_pad(p["a_log"][l], MISC_DT)
    lp["dtb_col"] = p["dt_bias"][l][:, None]
    lp["alog_col"] = p["a_log"][l][:, None]
    lp["dskip_row"] = jnp.repeat(p["d_skip"][l], SSD_HEAD_DIM)[None, :]
    lp["ssd_norm_g"] = depth_row(p["ssd_norm_g"])
    lp["conv_b_w"] = p["conv_b_w"][l]
    lp["conv_b_b"] = depth_row(p["conv_b_b"])
    lp["w_mq"] = p["w_mq"][l].astype(BF16)
    lp["w_mk"] = p["w_mk"][l].astype(BF16)
    lp["w_mv"] = p["w_mv"][l].astype(BF16)
    gate_b = jnp.concatenate([p["b_i"][l], p["b_f"][l]])
    lp["gb_row"] = _lane_pad(gate_b, MISC_I)
    lp["gb_col"] = gate_b[:, None]
    lp["ml_norm_g"] = depth_row(p["ml_norm_g"])
    lp["q_norm_g"] = depth_row(p["q_norm_g"])
    lp["kv_norm_g"] = depth_row(p["kv_norm_g"])
    w_uq = p["w_uq"][l]
    lp["w_uq_nope"] = w_uq[:, :, :MLA_NOPE].reshape(Q_RANK, MLA_HEADS * MLA_NOPE).astype(BF16)
    lp["w_uq_rope"] = jnp.transpose(w_uq[:, :, MLA_NOPE:], (1, 0, 2)).astype(BF16)
    w_uk = p["w_uk"][l].reshape(KV_RANK, MLA_HEADS * MLA_NOPE)
    lp["w_uk"] = w_uk.astype(BF16)
    lp["w_uk_t"] = w_uk.T.astype(BF16)
    lp["w_uv"] = p["w_uv"][l].reshape(KV_RANK, MLA_HEADS * MLA_V).astype(BF16)
    for nme in ("qn_nope_g", "qn_rope_g", "kn_nope_g", "kn_rope_g", "xq_norm_g"):
        lp[nme] = depth_row(p[nme])
    lp["w_out"] = p["w_out"][l].astype(BF16)
    half = MLA_ROPE // 2
    eye = jnp.eye(half, dtype=F32)
    zero = jnp.zeros((half, half), F32)
    lp["rope_swap"] = jnp.block([[zero, eye], [eye, zero]])
    head_of_lane = jnp.arange(LANES)[:, None] - MISC_DT
    head_of_chan = jnp.arange(GROUP_W)[None, :] // SSD_HEAD_DIM
    lp["head_expand"] = (head_of_lane == head_of_chan).astype(F32)
    return lp


def _rope_tables(pos):
    half = MLA_ROPE // 2
    inv = ROPE_BASE ** (-jnp.arange(half, dtype=F32) * 2.0 / MLA_ROPE)
    ang = pos.astype(F32)[:, None] * inv[None, :]
    cos = jnp.cos(ang)
    sin = jnp.sin(ang)
    return jnp.concatenate([cos, cos], axis=1), jnp.concatenate([-sin, sin], axis=1)


def _prompt_layer(lp, w_packed, x, mk, mv, rope_tabs, final_g, final):
    b, l, _ = x.shape
    n = b * l
    x2d = x.reshape(n, D_MODEL)
    z, xbc, xm, o, cq, ckv, xq, misc = _inproj(x2d, lp["norm_g"], w_packed, tm=512)
    r3 = lambda t: t.reshape(b, l, t.shape[-1])
    z, xbc, xm, o, cq, ckv, xq, misc = map(r3, (z, xbc, xm, o, cq, ckv, xq, misc))
    ya, ssm_h, ssm_buf = _ssd_prompt(xbc, misc, z, lp)
    yb, ml_c, ml_n, ml_m, ml_buf = _mlstm_prompt(xm, o, z, misc, lp)
    q, k, latb, lat, kr = _mla_prep(cq, ckv, misc, rope_tabs[0], rope_tabs[1], lp, tm=256)
    yc = _mla_attn_prompt(q, k, latb, z, lp["w_uv"])
    yd = _xattn(xq, mk, mv, z, lp["xq_norm_g"], tm=512)
    ys = [t.reshape(n, GROUP_W) for t in (ya, yb, yc, yd)]
    x_new = _outproj(ys, x2d, lp["w_out"], final_g, tm=512, final=final).reshape(b, l, D_MODEL)
    return x_new, (lat, kr, ssm_h, ssm_buf, ml_c, ml_n, ml_m, ml_buf)


def _sample_layer(lp, w_packed, layer, x, mem_k, mem_v, cache_lat, cache_kr, page_table, st, rope_tabs,
                  final_g, final, pps=16):
    b = x.shape[0]
    x2d = x.reshape(b, D_MODEL)
    z, xbc, xm, o, cq, ckv, xq, misc = _inproj(x2d, lp["norm_g"], w_packed, tm=b)
    tok = lambda t: t.reshape(b, 1, t.shape[-1])
    ya, yb, st_new = _dec_state(tok(xbc), tok(xm), tok(o), tok(z), tok(misc), st, lp)
    one = lambda t: t.reshape(1, b, t.shape[-1])
    q, k, latb, lat, kr = _mla_prep(one(cq), one(ckv), one(misc), rope_tabs[0], rope_tabs[1], lp, tm=b)
    qh = jnp.transpose(q[0], (1, 0, 2))
    kh = jnp.transpose(k[0], (1, 0, 2))
    dup = lambda t: jnp.concatenate([t, t], axis=1)
    qfull8 = dup(qh)
    kfull8 = dup(kh)
    qn8 = qfull8[:, :, :MLA_NOPE]
    qr8 = qfull8[:, :, MLA_NOPE:]
    vnew = latb.reshape(b, 1, KV_RANK)
    yc = _mla_decode(page_table, cache_lat, cache_kr, layer, qn8, qr8, qfull8, kfull8, vnew, tok(z), lp, pps=pps)
    mem_len = mem_k.shape[1]
    yd = _xattn(tok(xq), mem_k.reshape(b, mem_len, GROUP_W), mem_v.reshape(b, mem_len, GROUP_W), tok(z),
                lp["xq_norm_g"], tm=1)
    ys = [t.reshape(b, GROUP_W) for t in (ya, yb, yc, yd)]
    x_new = _outproj(ys, x2d, lp["w_out"], final_g, tm=b, final=final).reshape(b, 1, D_MODEL)
    hs_n, sbuf_n, cm_n, nm_n, mm_n, mbuf_n = st_new
    return x_new, (lat.reshape(b, 1, KV_RANK), kr.reshape(b, 1, MLA_ROPE), hs_n, sbuf_n, cm_n, nm_n, mm_n, mbuf_n)


def kernel(x_prompt, x_sample, mem_prompt, cache_mla_latent, cache_mla_krope, cache_mem_k, cache_mem_v, state_ssm, state_ssm_conv, state_mlstm_C, state_mlstm_n, state_mlstm_m, state_mlstm_conv, page_table, norm_g, w_in, conv_a_w, conv_a_b, dt_bias, a_log, d_skip, ssd_norm_g, conv_b_w, conv_b_b, w_mq, w_mk, w_mv, b_i, b_f, ml_norm_g, q_norm_g, kv_norm_g, w_uq, w_uk, w_uv, qn_nope_g, qn_rope_g, kn_nope_g, kn_rope_g, mem_norm_g, w_mem_k, w_mem_v, xk_norm_g, xq_norm_g, w_out, final_norm_g):
    params = dict(norm_g=norm_g, conv_a_w=conv_a_w, conv_a_b=conv_a_b, dt_bias=dt_bias, a_log=a_log, d_skip=d_skip,
                  ssd_norm_g=ssd_norm_g, conv_b_w=conv_b_w, conv_b_b=conv_b_b, w_mq=w_mq, w_mk=w_mk, w_mv=w_mv,
                  b_i=b_i, b_f=b_f, ml_norm_g=ml_norm_g, q_norm_g=q_norm_g, kv_norm_g=kv_norm_g, w_uq=w_uq,
                  w_uk=w_uk, w_uv=w_uv, qn_nope_g=qn_nope_g, qn_rope_g=qn_rope_g, kn_nope_g=kn_nope_g,
                  kn_rope_g=kn_rope_g, xq_norm_g=xq_norm_g, w_out=w_out)
    depth = w_in.shape[0]
    bp, lp_len, _ = x_prompt.shape
    bs, ls_len, _ = x_sample.shape
    n_pages = page_table.shape[1]
    past_len = n_pages * cache_mla_latent.shape[2]
    mem_len = mem_prompt.shape[1]
    w_packed = _pack_w_in(w_in)
    final_g = final_norm_g[None, :]

    rope_p = _rope_tables(jnp.arange(lp_len, dtype=jnp.int32))
    rope_s = _rope_tables(jnp.full((bs,), past_len, dtype=jnp.int32))

    mk_all, mv_all = _memkv(mem_prompt, mem_norm_g[:, None, :], w_mem_k.astype(BF16), w_mem_v.astype(BF16),
                            xk_norm_g[:, None, :])

    hp, hs = x_prompt, x_sample
    p_states, s_states = [], []
    for l in range(depth):
        lp = _layer_params(l, params)
        final = l == depth - 1
        hp, st_p = _prompt_layer(lp, w_packed[l], hp, mk_all[l], mv_all[l], rope_p, final_g, final)
        p_states.append(st_p)
        st_in = (state_ssm[l], state_ssm_conv[l], state_mlstm_C[l], state_mlstm_n[l], state_mlstm_m[l],
                 state_mlstm_conv[l])
        hs, st_s = _sample_layer(lp, w_packed[l], l, hs, cache_mem_k[l], cache_mem_v[l], cache_mla_latent,
                                 cache_mla_krope, page_table, st_in, rope_s, final_g, final)
        s_states.append(st_s)

    p_out = [jnp.stack([st[i] for st in p_states]) for i in range(8)]
    s_out = [jnp.stack([st[i] for st in s_states]) for i in range(8)]
    p_mem_k = mk_all.reshape(depth, bp, mem_len, X_HEADS, X_HEAD_DIM)
    p_mem_v = mv_all.reshape(depth, bp, mem_len, X_HEADS, X_HEAD_DIM)
    (p_lat, p_kr, p_ssm, p_sconv, p_mc, p_mn, p_mm, p_mconv) = p_out
    (s_lat, s_kr, s_ssm, s_sconv, s_mc, s_mn, s_mm, s_mconv) = s_out
    return (hp, hs, p_lat, p_kr, p_mem_k, p_mem_v, p_ssm, p_sconv, p_mc, p_mn, p_mm, p_mconv,
            s_lat, s_kr, s_ssm, s_sconv, s_mc, s_mn, s_mm, s_mconv)
```

```python
import functools
import math

import jax
import jax.numpy as jnp
from jax import lax
from jax.experimental import pallas as pl
from jax.experimental.pallas import tpu as pltpu

F32 = jnp.float32
BF16 = jnp.bfloat16
HIGHEST = lax.Precision.HIGHEST

D_MODEL = 1024
D_MIX = 2 * D_MODEL
GROUP_W = D_MIX // 4
SSD_HEAD_DIM = 64
SSD_HEADS = GROUP_W // SSD_HEAD_DIM
SSD_GROUPS = 2
SSD_HPG = SSD_HEADS // SSD_GROUPS
SSD_STATE = 128
CONV_W = 4
SSD_CONV_CH = GROUP_W + 2 * SSD_GROUPS * SSD_STATE
ML_HEADS = 4
ML_HEAD_DIM = GROUP_W // ML_HEADS
MLA_HEADS = 4
MLA_NOPE = 128
MLA_ROPE = 64
MLA_QK = MLA_NOPE + MLA_ROPE
MLA_V = GROUP_W // MLA_HEADS
Q_RANK = 3 * D_MODEL // 8
KV_RANK = D_MODEL // 4
ROPE_BASE = 10000.0
X_HEADS = 4
X_HEAD_DIM = GROUP_W // X_HEADS
CHUNK = 128
EPS = 1e-6
IN_SIZES = (D_MIX, SSD_CONV_CH, SSD_HEADS, GROUP_W, ML_HEADS, ML_HEADS, GROUP_W, Q_RANK, KV_RANK, MLA_ROPE, GROUP_W)

LANES = 128
SUBLANES = 8
VMEM_LIMIT = 56 * 1024 * 1024

MISC_W = LANES
MISC_KR = 0
MISC_DT = MLA_ROPE
MISC_I = MISC_DT + SSD_HEADS
MISC_F = MISC_I + ML_HEADS
PACK_SEGS = (("z", D_MIX), ("xbc", SSD_CONV_CH), ("xm", GROUP_W), ("o", GROUP_W), ("cq", Q_RANK),
             ("ckv", KV_RANK), ("xq", GROUP_W), ("misc", MISC_W))
PACK_W = sum(w for _, w in PACK_SEGS)


def _cparams(sem):
    return pltpu.CompilerParams(dimension_semantics=sem, vmem_limit_bytes=VMEM_LIMIT)


def _dot(a, b):
    return jnp.dot(a, b, preferred_element_type=F32)


def _dot_nt(a, b, precision=None):
    return lax.dot_general(a, b, (((1,), (1,)), ((), ())), preferred_element_type=F32, precision=precision)


def _dot_tn(a, b):
    return lax.dot_general(a, b, (((0,), (0,)), ((), ())), preferred_element_type=F32)


def _rms(x, g):
    return x * lax.rsqrt(jnp.mean(x * x, axis=-1, keepdims=True) + EPS) * g


def _silu(x):
    return x * jax.nn.sigmoid(x)


def _softplus(x):
    return jnp.maximum(x, 0.0) + jnp.log1p(jnp.exp(-jnp.abs(x)))


def _const_spec(shape):
    nd = len(shape)
    return pl.BlockSpec(shape, lambda *_: (0,) * nd)


def _inproj_kernel(x_ref, g_ref, *refs):
    nseg = len(PACK_SEGS)
    w_refs, out_refs = refs[:nseg], refs[nseg:]
    x = x_ref[...]
    u = _rms(x, g_ref[...]).astype(BF16)
    for w_ref, out_ref in zip(w_refs, out_refs):
        out_ref[...] = _dot(u, w_ref[0])


def _inproj(x2d, g, w_slabs, layer, tm):
    n = x2d.shape[0]
    out_shape = [jax.ShapeDtypeStruct((n, w), F32) for _, w in PACK_SEGS]
    out_specs = [pl.BlockSpec((tm, w), lambda i: (i, 0)) for _, w in PACK_SEGS]
    w_specs = [pl.BlockSpec((1, D_MODEL, w), lambda i: (layer, 0, 0), pipeline_mode=pl.Buffered(1))
               for _, w in PACK_SEGS]
    return pl.pallas_call(
        _inproj_kernel,
        grid=(n // tm,),
        in_specs=[pl.BlockSpec((tm, D_MODEL), lambda i: (i, 0)), _const_spec((1, D_MODEL))] + w_specs,
        out_specs=out_specs,
        out_shape=out_shape,
        compiler_params=_cparams(("parallel",)),
        name="inproj",
    )(x2d, g, *w_slabs)


def _tri_masks():
    r = lax.broadcasted_iota(jnp.int32, (CHUNK, CHUNK), 0)
    c = lax.broadcasted_iota(jnp.int32, (CHUNK, CHUNK), 1)
    return r >= c, c >= r


def _ssd_kernel(xbc_ref, misc_ref, za_ref, cw_ref, cb_ref, dtb_row_ref, alog_row_ref, dtb_col_ref,
                alog_col_ref, dskip_ref, ng_ref, ya_ref, hfin_ref, conv_ref, h_scr, ext_scr, y_scr):
    c = pl.program_id(1)
    nc = pl.num_programs(1)

    @pl.when(c == 0)
    def _():
        h_scr[...] = jnp.zeros_like(h_scr)
        ext_scr[0:SUBLANES, :] = jnp.zeros((SUBLANES, SSD_CONV_CH), F32)

    u = xbc_ref[0]
    ext_scr[SUBLANES:SUBLANES + CHUNK, :] = u
    cw = cw_ref[...]
    conv = cb_ref[...] + cw[3:4] * u
    for j in range(CONV_W - 1):
        back = CONV_W - 1 - j
        conv = conv + cw[j:j + 1] * ext_scr[SUBLANES - back:SUBLANES - back + CHUNK, :]
    ext_scr[0:SUBLANES, :] = u[CHUNK - SUBLANES:CHUNK, :]
    xbcs = _silu(conv)
    xa = xbcs[:, 0:GROUP_W]

    misc = misc_ref[0]
    misc_t = misc.T
    lower, upper = _tri_masks()
    lower_f = lower.astype(F32)
    upper_f = upper.astype(F32)
    dt_col = _softplus(misc + dtb_row_ref[...])
    dt_row = _softplus(misc_t[MISC_DT:MISC_DT + SSD_HEADS, :] + dtb_col_ref[...])
    a_row = -jnp.exp(alog_row_ref[...])
    a_col = -jnp.exp(alog_col_ref[...])
    acum_col = jnp.dot(lower_f, dt_col * a_row, precision=HIGHEST, preferred_element_type=F32)
    acum_row = jnp.dot(dt_row * a_col, upper_f, precision=HIGHEST, preferred_element_type=F32)

    for g in range(SSD_GROUPS):
        b_g = xbcs[:, GROUP_W + g * SSD_STATE:GROUP_W + (g + 1) * SSD_STATE].astype(BF16)
        c_off = GROUP_W + SSD_GROUPS * SSD_STATE
        c_g = xbcs[:, c_off + g * SSD_STATE:c_off + (g + 1) * SSD_STATE].astype(BF16)
        cb = _dot_nt(c_g, b_g)
        for hh in range(SSD_HPG):
            h = g * SSD_HPG + hh
            ac_col = acum_col[:, MISC_DT + h:MISC_DT + h + 1]
            ac_row = acum_row[h:h + 1, :]
            seg = jnp.where(lower, ac_col - ac_row, -jnp.inf)
            w = cb * jnp.exp(seg) * dt_row[h:h + 1, :]
            x_h = xa[:, h * SSD_HEAD_DIM:(h + 1) * SSD_HEAD_DIM]
            hs = h_scr[h]
            y_h = _dot(w.astype(BF16), x_h.astype(BF16)) + _dot_nt(c_g, hs.astype(BF16)) * jnp.exp(ac_col)
            y_scr[:, h * SSD_HEAD_DIM:(h + 1) * SSD_HEAD_DIM] = y_h
            ac_last = ac_col[CHUNK - 1:CHUNK, :]
            wend = jnp.exp(ac_last - ac_col) * dt_col[:, MISC_DT + h:MISC_DT + h + 1]
            xw = (x_h * wend).astype(BF16)
            h_scr[h] = jnp.exp(ac_last) * hs + _dot_tn(xw, b_g)

    y = y_scr[...] + dskip_ref[...] * xa
    y = y * _silu(za_ref[0])
    gw = GROUP_W // SSD_GROUPS
    ng = ng_ref[...]
    for g in range(SSD_GROUPS):
        ya_ref[0, :, g * gw:(g + 1) * gw] = _rms(y[:, g * gw:(g + 1) * gw], ng[:, g * gw:(g + 1) * gw])

    @pl.when(c == nc - 1)
    def _():
        hfin_ref[0] = h_scr[...]
        conv_ref[0] = xbc_ref[0, CHUNK - (CONV_W - 1):CHUNK, :]


def _lane_pad(v, offset):
    return jnp.zeros((1, LANES), F32).at[0, offset:offset + v.shape[0]].set(v)


def _ssd_prompt(xbc, misc, z, lp):
    b, l, _ = xbc.shape
    nc = l // CHUNK
    row = lambda bi, ci: (bi, ci, 0)
    return pl.pallas_call(
        _ssd_kernel,
        grid=(b, nc),
        in_specs=[pl.BlockSpec((1, CHUNK, SSD_CONV_CH), row),
                  pl.BlockSpec((1, CHUNK, MISC_W), row),
                  pl.BlockSpec((1, CHUNK, GROUP_W), lambda bi, ci: (bi, ci, 0)),
                  _const_spec((CONV_W, SSD_CONV_CH)), _const_spec((1, SSD_CONV_CH)),
                  _const_spec((1, LANES)), _const_spec((1, LANES)),
                  _const_spec((SSD_HEADS, 1)), _const_spec((SSD_HEADS, 1)),
                  _const_spec((1, GROUP_W)), _const_spec((1, GROUP_W))],
        out_specs=[pl.BlockSpec((1, CHUNK, GROUP_W), row),
                   pl.BlockSpec((1, SSD_HEADS, SSD_HEAD_DIM, SSD_STATE), lambda bi, ci: (bi, 0, 0, 0)),
                   pl.BlockSpec((1, CONV_W - 1, SSD_CONV_CH), lambda bi, ci: (bi, 0, 0))],
        out_shape=[jax.ShapeDtypeStruct((b, l, GROUP_W), F32),
                   jax.ShapeDtypeStruct((b, SSD_HEADS, SSD_HEAD_DIM, SSD_STATE), F32),
                   jax.ShapeDtypeStruct((b, CONV_W - 1, SSD_CONV_CH), F32)],
        scratch_shapes=[pltpu.VMEM((SSD_HEADS, SSD_HEAD_DIM, SSD_STATE), F32),
                        pltpu.VMEM((SUBLANES + CHUNK, SSD_CONV_CH), F32),
                        pltpu.VMEM((CHUNK, GROUP_W), F32)],
        compiler_params=_cparams(("parallel", "arbitrary")),
        name="ssd_prompt",
    )(xbc, misc, z, lp["conv_a_w"], lp["conv_a_b"], lp["dtb_row"], lp["alog_row"], lp["dtb_col"],
      lp["alog_col"], lp["dskip_row"], lp["ssd_norm_g"])


def _mlstm_kernel(xm_ref, o_ref, zb_ref, misc_ref, cw_ref, cb_ref, wqt_ref, wk_ref, wvt_ref, gb_row_ref,
                  gb_col_ref, ng_ref, yb_ref, cfin_ref, nfin_ref, mfin_ref, conv_ref,
                  c_scr, n_scr, m_scr, ext_scr):
    c = pl.program_id(1)
    nc = pl.num_programs(1)

    @pl.when(c == 0)
    def _():
        c_scr[...] = jnp.zeros_like(c_scr)
        n_scr[...] = jnp.zeros_like(n_scr)
        m_scr[...] = jnp.zeros_like(m_scr)
        ext_scr[0:SUBLANES, :] = jnp.zeros((SUBLANES, GROUP_W), F32)

    u = xm_ref[0]
    ext_scr[SUBLANES:SUBLANES + CHUNK, :] = u
    cw = cw_ref[...]
    conv = cb_ref[...] + cw[3:4] * u
    for j in range(CONV_W - 1):
        back = CONV_W - 1 - j
        conv = conv + cw[j:j + 1] * ext_scr[SUBLANES - back:SUBLANES - back + CHUNK, :]
    ext_scr[0:SUBLANES, :] = u[CHUNK - SUBLANES:CHUNK, :]
    xmc = _silu(conv)

    misc = misc_ref[0]
    misc_t = misc.T
    lower, upper = _tri_masks()
    gates_col = misc + gb_row_ref[...]
    gates_row = misc_t[MISC_I:MISC_I + 2 * ML_HEADS, :] + gb_col_ref[...]
    lf_col = -_softplus(-gates_col)
    lf_row = -_softplus(-gates_row)
    bcum_col = jnp.dot(lower.astype(F32), lf_col, precision=HIGHEST, preferred_element_type=F32)
    bcum_row = jnp.dot(lf_row, upper.astype(F32), precision=HIGHEST, preferred_element_type=F32)
    ng = ng_ref[...]
    o_all = o_ref[0]
    zb = zb_ref[0]
    m_all = m_scr[...]
    n_all = n_scr[...]
    m_rows, n_rows = [], []
    for h in range(ML_HEADS):
        sl = slice(h * ML_HEAD_DIM, (h + 1) * ML_HEAD_DIM)
        li_r = gates_row[h:h + 1, :]
        bc_r = bcum_row[ML_HEADS + h:ML_HEADS + h + 1, :]
        col = gates_col[:, MISC_I + h:MISC_I + h + 1] - bcum_col[:, MISC_F + h:MISC_F + h + 1]
        m_prev = m_all[h:h + 1, 0:1]
        lw = jnp.where(upper, bc_r + col, -jnp.inf)
        l_prev = bc_r + m_prev
        m_t = jnp.maximum(l_prev, jnp.max(lw, axis=0, keepdims=True))
        xmc_h = xmc[:, sl].astype(BF16)
        qtb = _dot_nt(wqt_ref[h], xmc_h).astype(BF16)
        k = _dot(xmc_h, wk_ref[h]) * (ML_HEAD_DIM ** -0.5)
        kb = k.astype(BF16)
        vt = _dot_nt(wvt_ref[h], u[:, sl].astype(BF16))
        sc = _dot(kb, qtb) * jnp.exp(lw - m_t)
        g_prev = jnp.exp(l_prev - m_t)
        cs = c_scr[h]
        ns = n_all[h:h + 1, :]
        ns8 = jnp.broadcast_to(ns, (SUBLANES, ML_HEAD_DIM)).astype(BF16)
        num = _dot(vt.astype(BF16), sc.astype(BF16)) + g_prev * _dot(cs.astype(BF16), qtb)
        den = jnp.sum(sc, axis=0, keepdims=True) + g_prev * _dot(ns8, qtb)[0:1]
        hout_t = num / jnp.maximum(jnp.abs(den), jnp.exp(-m_t))
        m_new = m_t[:, CHUNK - 1:CHUNK]
        bc_last = bc_r[:, CHUNK - 1:CHUNK]
        wend = jnp.exp(bc_last - bc_r + li_r - m_new)
        decay = jnp.exp(bc_last + m_prev - m_new)
        c_scr[h] = decay * cs + _dot((vt * wend).astype(BF16), kb)
        wend8 = jnp.broadcast_to(wend, (SUBLANES, CHUNK)).astype(BF16)
        n_rows.append(decay * ns + _dot(wend8, kb)[0:1])
        m_rows.append(jnp.broadcast_to(m_new, (1, LANES)))
        hb = jax.nn.sigmoid(o_all[:, sl]) * hout_t.T
        yb_ref[0, :, sl] = _rms(hb, ng[:, sl]) * _silu(zb[:, sl])
    pad = [jnp.zeros((SUBLANES - ML_HEADS, LANES), F32)]
    n_scr[...] = jnp.concatenate(n_rows + pad, axis=0)
    m_scr[...] = jnp.concatenate(m_rows + pad, axis=0)

    @pl.when(c == nc - 1)
    def _():
        cfin_ref[0] = c_scr[...]
        nfin_ref[0] = n_scr[...]
        mfin_ref[0] = m_scr[...]
        conv_ref[0] = xm_ref[0, CHUNK - (CONV_W - 1):CHUNK, :]


def _mlstm_prompt(xm, o, z, misc, lp):
    b, l, _ = xm.shape
    nc = l // CHUNK
    row = lambda bi, ci: (bi, ci, 0)
    outs = pl.pallas_call(
        _mlstm_kernel,
        grid=(b, nc),
        in_specs=[pl.BlockSpec((1, CHUNK, GROUP_W), row),
                  pl.BlockSpec((1, CHUNK, GROUP_W), row),
                  pl.BlockSpec((1, CHUNK, GROUP_W), lambda bi, ci: (bi, ci, 1)),
                  pl.BlockSpec((1, CHUNK, MISC_W), row),
                  _const_spec((CONV_W, GROUP_W)), _const_spec((1, GROUP_W)),
                  _const_spec((ML_HEADS, ML_HEAD_DIM, ML_HEAD_DIM)),
                  _const_spec((ML_HEADS, ML_HEAD_DIM, ML_HEAD_DIM)),
                  _const_spec((ML_HEADS, ML_HEAD_DIM, ML_HEAD_DIM)),
                  _const_spec((1, LANES)), _const_spec((2 * ML_HEADS, 1)), _const_spec((1, GROUP_W))],
        out_specs=[pl.BlockSpec((1, CHUNK, GROUP_W), row),
                   pl.BlockSpec((1, ML_HEADS, ML_HEAD_DIM, ML_HEAD_DIM), lambda bi, ci: (bi, 0, 0, 0)),
                   pl.BlockSpec((1, SUBLANES, LANES), lambda bi, ci: (bi, 0, 0)),
                   pl.BlockSpec((1, SUBLANES, LANES), lambda bi, ci: (bi, 0, 0)),
                   pl.BlockSpec((1, CONV_W - 1, GROUP_W), lambda bi, ci: (bi, 0, 0))],
        out_shape=[jax.ShapeDtypeStruct((b, l, GROUP_W), F32),
                   jax.ShapeDtypeStruct((b, ML_HEADS, ML_HEAD_DIM, ML_HEAD_DIM), F32),
                   jax.ShapeDtypeStruct((b, SUBLANES, LANES), F32),
                   jax.ShapeDtypeStruct((b, SUBLANES, LANES), F32),
                   jax.ShapeDtypeStruct((b, CONV_W - 1, GROUP_W), F32)],
        scratch_shapes=[pltpu.VMEM((ML_HEADS, ML_HEAD_DIM, ML_HEAD_DIM), F32),
                        pltpu.VMEM((SUBLANES, LANES), F32),
                        pltpu.VMEM((SUBLANES, LANES), F32),
                        pltpu.VMEM((SUBLANES + CHUNK, GROUP_W), F32)],
        compiler_params=_cparams(("parallel", "arbitrary")),
        name="mlstm_prompt",
    )(xm, o, z, misc, lp["conv_b_w"], lp["conv_b_b"], lp["w_mq_t"], lp["w_mk"], lp["w_mv_t"],
      lp["gb_row"], lp["gb_col"], lp["ml_norm_g"])
    yb, cfin, nfin, mfin, mconv = outs
    return yb, cfin, nfin[:, :ML_HEADS, :], mfin[:, :ML_HEADS, 0], mconv


def _rope_rows(r, g, cos2, sinsgn, swap):
    rn = r * lax.rsqrt(jnp.sum(r * r, axis=-1, keepdims=True) * (1.0 / MLA_ROPE) + EPS) * g
    r_sw = jnp.dot(rn, swap, precision=HIGHEST, preferred_element_type=F32)
    return rn * cos2 + r_sw * sinsgn


def _mla_prep_kernel(cq_ref, ckv_ref, misc_ref, cos_ref, sin_ref, swap_ref, qg_ref, kvg_ref, wqn_ref, wqr_ref,
                     wuk_ref, qnn_ref, qnr_ref, knn_ref, knr_ref, q_ref, k_ref, v_ref, lat_ref, kr_ref,
                     *, transposed):
    cos2 = cos_ref[...]
    sinsgn = sin_ref[...]
    swap = swap_ref[...]
    c_q = _rms(cq_ref[0], qg_ref[...]).astype(BF16)
    qn_all = _dot(c_q, wqn_ref[...])
    c_kv = _rms(ckv_ref[0], kvg_ref[...])
    lat_ref[0] = c_kv
    c_kv_b = c_kv.astype(BF16)
    if transposed:
        v_ref[0] = c_kv.T.astype(BF16)
    else:
        v_ref[0] = c_kv_b
    misc = misc_ref[0]
    lane = lax.broadcasted_iota(jnp.int32, misc.shape, 1)
    kr_raw = jnp.where(lane < MLA_ROPE, misc, 0.0)
    kr = _rope_rows(kr_raw, knr_ref[...], cos2, sinsgn, swap)
    kr_ref[0] = kr[:, 0:MLA_ROPE]
    kr_b = kr[:, 0:MLA_ROPE].astype(BF16)
    kn_all = _dot(c_kv_b, wuk_ref[...])
    for h in range(MLA_HEADS):
        sl = slice(h * MLA_NOPE, (h + 1) * MLA_NOPE)
        qn = _rms(qn_all[:, sl], qnn_ref[...])
        qr = _rope_rows(_dot(c_q, wqr_ref[h]), qnr_ref[...], cos2, sinsgn, swap)
        if transposed:
            q_ref[0, h, 0:MLA_NOPE, :] = qn.T.astype(BF16)
            q_ref[0, h, MLA_NOPE:MLA_QK, :] = qr.T[0:MLA_ROPE].astype(BF16)
        else:
            q_ref[0, h, :, 0:MLA_NOPE] = qn.astype(BF16)
            q_ref[0, h, :, MLA_NOPE:MLA_QK] = qr[:, 0:MLA_ROPE].astype(BF16)
        k_ref[0, h, :, 0:MLA_NOPE] = _rms(kn_all[:, sl], knn_ref[...]).astype(BF16)
        k_ref[0, h, :, MLA_NOPE:MLA_QK] = kr_b


def _mla_prep(cq, ckv, misc, cos2, sinsgn, lp, tm, transposed):
    b, l, _ = cq.shape
    row = lambda bi, ti: (bi, ti, 0)
    hrow = lambda bi, ti: (bi, 0, ti, 0)
    if transposed:
        q_spec = pl.BlockSpec((1, MLA_HEADS, MLA_QK, tm), lambda bi, ti: (bi, 0, 0, ti))
        q_shape = jax.ShapeDtypeStruct((b, MLA_HEADS, MLA_QK, l), BF16)
        v_spec = pl.BlockSpec((1, KV_RANK, tm), lambda bi, ti: (bi, 0, ti))
        v_shape = jax.ShapeDtypeStruct((b, KV_RANK, l), BF16)
    else:
        q_spec = pl.BlockSpec((1, MLA_HEADS, tm, MLA_QK), hrow)
        q_shape = jax.ShapeDtypeStruct((b, MLA_HEADS, l, MLA_QK), BF16)
        v_spec = pl.BlockSpec((1, tm, KV_RANK), row)
        v_shape = jax.ShapeDtypeStruct((b, l, KV_RANK), BF16)
    return pl.pallas_call(
        functools.partial(_mla_prep_kernel, transposed=transposed),
        grid=(b, l // tm),
        in_specs=[pl.BlockSpec((1, tm, Q_RANK), row), pl.BlockSpec((1, tm, KV_RANK), row),
                  pl.BlockSpec((1, tm, MISC_W), row),
                  pl.BlockSpec((tm, LANES), lambda bi, ti: (ti, 0)),
                  pl.BlockSpec((tm, LANES), lambda bi, ti: (ti, 0)),
                  _const_spec((LANES, LANES)),
                  _const_spec((1, Q_RANK)), _const_spec((1, KV_RANK)),
                  _const_spec((Q_RANK, MLA_HEADS * MLA_NOPE)),
                  _const_spec((MLA_HEADS, Q_RANK, LANES)),
                  _const_spec((KV_RANK, MLA_HEADS * MLA_NOPE)),
                  _const_spec((1, MLA_NOPE)), _const_spec((1, LANES)),
                  _const_spec((1, MLA_NOPE)), _const_spec((1, LANES))],
        out_specs=[q_spec,
                   pl.BlockSpec((1, MLA_HEADS, tm, MLA_QK), hrow),
                   v_spec,
                   pl.BlockSpec((1, tm, KV_RANK), row),
                   pl.BlockSpec((1, tm, MLA_ROPE), row)],
        out_shape=[q_shape,
                   jax.ShapeDtypeStruct((b, MLA_HEADS, l, MLA_QK), BF16),
                   v_shape,
                   jax.ShapeDtypeStruct((b, l, KV_RANK), F32),
                   jax.ShapeDtypeStruct((b, l, MLA_ROPE), F32)],
        compiler_params=_cparams(("parallel", "parallel")),
        name="mla_prep",
    )(cq, ckv, misc, cos2, sinsgn, lp["rope_swap"], lp["q_norm_g"], lp["kv_norm_g"], lp["w_uq_nope"],
      lp["w_uq_rope"], lp["w_uk"], lp["qn_nope_g"], lp["qn_rope_g_pad"], lp["kn_nope_g"], lp["kn_rope_g_pad"])


def _mla_attn_kernel(qt_ref, k_ref, vt_ref, zc_ref, wuv_ref, out_ref, m_scr, l_scr, acc_scr, *, tile):
    qi = pl.program_id(1)
    scale = MLA_QK ** -0.5
    m_scr[...] = jnp.full_like(m_scr, -jnp.inf)
    l_scr[...] = jnp.zeros_like(l_scr)
    acc_scr[...] = jnp.zeros_like(acc_scr)

    def kv_tile(j, diagonal):
        start = pl.multiple_of(j * tile, tile)
        vt = vt_ref[0, :, pl.ds(start, tile)]
        for h in range(MLA_HEADS):
            k = k_ref[0, h, pl.ds(start, tile), :]
            s = _dot(k, qt_ref[0, h]) * scale
            if diagonal:
                kid = lax.broadcasted_iota(jnp.int32, (tile, tile), 0)
                qid = lax.broadcasted_iota(jnp.int32, (tile, tile), 1)
                s = jnp.where(kid <= qid, s, -jnp.inf)
            m_prev = m_scr[h:h + 1, :]
            m_new = jnp.maximum(m_prev, jnp.max(s, axis=0, keepdims=True))
            alpha = jnp.exp(m_prev - m_new)
            p = jnp.exp(s - m_new)
            l_scr[h:h + 1, :] = alpha * l_scr[h:h + 1, :] + jnp.sum(p, axis=0, keepdims=True)
            acc_scr[h] = alpha * acc_scr[h] + _dot(vt, p.astype(BF16))
            m_scr[h:h + 1, :] = m_new

    def body(j, carry):
        kv_tile(j, False)
        return carry

    lax.fori_loop(0, qi, body, 0)
    kv_tile(qi, True)
    for h in range(MLA_HEADS):
        sl = slice(h * MLA_V, (h + 1) * MLA_V)
        ctx_t = (acc_scr[h] / l_scr[h:h + 1, :]).astype(BF16)
        out_ref[0, :, sl] = _dot_tn(ctx_t, wuv_ref[:, sl]) * _silu(zc_ref[0, :, sl])


def _mla_attn_prompt(qt, k, latt, z, w_uv, tile=256):
    b, hds, _, l = qt.shape
    kern = functools.partial(_mla_attn_kernel, tile=tile)
    return pl.pallas_call(
        kern,
        grid=(b, l // tile),
        in_specs=[pl.BlockSpec((1, hds, MLA_QK, tile), lambda bi, qi: (bi, 0, 0, qi)),
                  pl.BlockSpec((1, hds, l, MLA_QK), lambda bi, qi: (bi, 0, 0, 0)),
                  pl.BlockSpec((1, KV_RANK, l), lambda bi, qi: (bi, 0, 0)),
                  pl.BlockSpec((1, tile, GROUP_W), lambda bi, qi: (bi, qi, 2)),
                  _const_spec((KV_RANK, GROUP_W))],
        out_specs=pl.BlockSpec((1, tile, GROUP_W), lambda bi, qi: (bi, qi, 0)),
        out_shape=jax.ShapeDtypeStruct((b, l, GROUP_W), F32),
        scratch_shapes=[pltpu.VMEM((SUBLANES, tile), F32), pltpu.VMEM((SUBLANES, tile), F32),
                        pltpu.VMEM((hds, KV_RANK, tile), F32)],
        compiler_params=_cparams(("parallel", "arbitrary")),
        name="mla_attn",
    )(qt, k, latt, z, w_uv)


def _memkv_kernel(mem_ref, g_ref, wk_ref, wv_ref, xkg_ref, mk_ref, mv_ref):
    mem_n = _rms(mem_ref[0], g_ref[0]).astype(BF16)
    kraw = _dot(mem_n, wk_ref[0])
    for h in range(X_HEADS):
        sl = slice(h * X_HEAD_DIM, (h + 1) * X_HEAD_DIM)
        mk_ref[0, 0, :, sl] = _rms(kraw[:, sl], xkg_ref[0])
    mv_ref[0, 0] = _dot(mem_n, wv_ref[0])


def _memkv(mem, mem_norm_g, w_mem_k, w_mem_v, xk_norm_g):
    b, m, _ = mem.shape
    depth = w_mem_k.shape[0]
    lsel = lambda li, bi: (li, 0, 0)
    return pl.pallas_call(
        _memkv_kernel,
        grid=(depth, b),
        in_specs=[pl.BlockSpec((1, m, D_MODEL), lambda li, bi: (bi, 0, 0)),
                  pl.BlockSpec((1, 1, D_MODEL), lsel),
                  pl.BlockSpec((1, D_MODEL, GROUP_W), lsel),
                  pl.BlockSpec((1, D_MODEL, GROUP_W), lsel),
                  pl.BlockSpec((1, 1, X_HEAD_DIM), lsel)],
        out_specs=[pl.BlockSpec((1, 1, m, GROUP_W), lambda li, bi: (li, bi, 0, 0)),
                   pl.BlockSpec((1, 1, m, GROUP_W), lambda li, bi: (li, bi, 0, 0))],
        out_shape=[jax.ShapeDtypeStruct((depth, b, m, GROUP_W), F32),
                   jax.ShapeDtypeStruct((depth, b, m, GROUP_W), F32)],
        compiler_params=_cparams(("parallel", "parallel")),
        name="memkv",
    )(mem, mem_norm_g, w_mem_k, w_mem_v, xk_norm_g)


def _xattn_kernel(xq_ref, mk_ref, mv_ref, zd_ref, g_ref, out_ref, *, rows, mem_len, interleaved):
    xq = xq_ref[0]
    zd = zd_ref[0]
    if rows < SUBLANES:
        xq = jnp.broadcast_to(xq[0:1], (SUBLANES, GROUP_W))
    for h in range(X_HEADS):
        sl = slice(h * X_HEAD_DIM, (h + 1) * X_HEAD_DIM)
        if interleaved:
            mk_h = mk_ref[0, 0, pl.ds(h, mem_len, stride=X_HEADS), :]
            mv_h = mv_ref[0, 0, pl.ds(h, mem_len, stride=X_HEADS), :]
        else:
            mk_h = mk_ref[0, 0, :, sl]
            mv_h = mv_ref[0, 0, :, sl]
        qn = _rms(xq[:, sl], g_ref[...]).astype(BF16)
        s = _dot_nt(qn, mk_h.astype(BF16)) * (X_HEAD_DIM ** -0.5)
        s = s - jnp.max(s, axis=1, keepdims=True)
        e = jnp.exp(s)
        p = e / jnp.sum(e, axis=1, keepdims=True)
        y = _dot(p.astype(BF16), mv_h.astype(BF16))
        out_ref[0, :, sl] = y[0:rows] * _silu(zd[:, sl])


def _xattn(xq, mk, mv, layer, z, g, tm, interleaved):
    b, l, _ = xq.shape
    mem_len = mk.shape[2] // X_HEADS if interleaved else mk.shape[2]
    kern = functools.partial(_xattn_kernel, rows=tm, mem_len=mem_len, interleaved=interleaved)
    mem_spec = pl.BlockSpec((1, 1) + mk.shape[2:], lambda bi, ti: (layer, bi, 0, 0))
    return pl.pallas_call(
        kern,
        grid=(b, l // tm),
        in_specs=[pl.BlockSpec((1, tm, GROUP_W), lambda bi, ti: (bi, ti, 0)),
                  mem_spec, mem_spec,
                  pl.BlockSpec((1, tm, GROUP_W), lambda bi, ti: (bi, ti, 3)),
                  _const_spec((1, X_HEAD_DIM))],
        out_specs=pl.BlockSpec((1, tm, GROUP_W), lambda bi, ti: (bi, ti, 0)),
        out_shape=jax.ShapeDtypeStruct((b, l, GROUP_W), F32),
        compiler_params=_cparams(("parallel", "parallel")),
        name="xattn",
    )(xq, mk, mv, z, g)


def _outproj_kernel(ya_ref, yb_ref, yc_ref, yd_ref, x_ref, w_ref, fg_ref, out_ref, *, final):
    acc = x_ref[...]
    for i, ref in enumerate((ya_ref, yb_ref, yc_ref, yd_ref)):
        acc = acc + _dot(ref[...].astype(BF16), w_ref[i * GROUP_W:(i + 1) * GROUP_W, :])
    if final:
        acc = _rms(acc, fg_ref[...])
    out_ref[...] = acc


def _outproj(ys, x2d, w_out, final_g, tm, final):
    n = x2d.shape[0]
    kern = functools.partial(_outproj_kernel, final=final)
    yspec = pl.BlockSpec((tm, GROUP_W), lambda i: (i, 0))
    return pl.pallas_call(
        kern,
        grid=(n // tm,),
        in_specs=[yspec, yspec, yspec, yspec,
                  pl.BlockSpec((tm, D_MODEL), lambda i: (i, 0)),
                  _const_spec((D_MIX, D_MODEL)), _const_spec((1, D_MODEL))],
        out_specs=pl.BlockSpec((tm, D_MODEL), lambda i: (i, 0)),
        out_shape=jax.ShapeDtypeStruct((n, D_MODEL), F32),
        compiler_params=_cparams(("parallel",)),
        name="outproj",
    )(*ys, x2d, w_out, final_g)


def _col_form(row, n):
    return jnp.broadcast_to(row, (LANES, n)).T


N_DEC_STATES = 6


def _dec_state_kernel(*refs):
    xbc_ref, xm_ref, o_ref, z_ref, misc_ref = refs[0:5]
    hs_ref, sbuf_ref, cm_ref, nm_ref, mm_ref, mbuf_ref = (r.at[0] for r in refs[5:5 + N_DEC_STATES])
    (cwa_ref, cba_ref, dtb_ref, alog_ref, hexp_ref, dskip_ref, nga_ref,
     cwb_ref, cbb_ref, wq_ref, wk_ref, wv_ref, gb_ref, ngb_ref) = refs[11:25]
    ya_ref, yb_ref = refs[-8:-6]
    hs_out, sbuf_out, cm_out, nm_out, mm_out, mbuf_out = (r.at[0] for r in refs[-N_DEC_STATES:])
    misc = misc_ref[0]
    z = z_ref[0]

    u = xbc_ref[0]
    buf = sbuf_ref[0]
    cw = cwa_ref[...]
    conv = cba_ref[...] + cw[3:4] * u
    for j in range(CONV_W - 1):
        conv = conv + cw[j:j + 1] * buf[j:j + 1]
    sbuf_out[0, 0:CONV_W - 2, :] = buf[1:CONV_W - 1]
    sbuf_out[0, CONV_W - 2:CONV_W - 1, :] = u
    xbcs = _silu(conv)
    xa = xbcs[:, 0:GROUP_W]
    dt = _softplus(misc + dtb_ref[...])
    a = -jnp.exp(alog_ref[...])
    hexp = hexp_ref[...]
    dt8 = jnp.broadcast_to(dt, (SUBLANES, LANES))
    da8 = jnp.broadcast_to(dt * a, (SUBLANES, LANES))
    dt_x = jnp.dot(dt8, hexp, precision=HIGHEST, preferred_element_type=F32)[0:1]
    decay_x = jnp.exp(jnp.dot(da8, hexp, precision=HIGHEST, preferred_element_type=F32)[0:1])
    hstate = hs_ref[0].reshape(SSD_HEADS * SSD_HEAD_DIM, SSD_STATE)
    gw = GROUP_W // SSD_GROUPS
    c_off = GROUP_W + SSD_GROUPS * SSD_STATE
    dtx_col = _col_form(dt_x * xa, GROUP_W)
    decay_col = _col_form(decay_x, GROUP_W)
    y_parts = []
    for g in range(SSD_GROUPS):
        b_g = xbcs[:, GROUP_W + g * SSD_STATE:GROUP_W + (g + 1) * SSD_STATE]
        c_g = xbcs[:, c_off + g * SSD_STATE:c_off + (g + 1) * SSD_STATE]
        cb = jnp.sum(c_g * b_g, axis=1, keepdims=True)
        rows = slice(g * gw, (g + 1) * gw)
        h_g = hstate[rows]
        c8 = jnp.broadcast_to(c_g, (SUBLANES, SSD_STATE)).astype(BF16)
        ch = _dot_nt(c8, h_g.astype(BF16))[0:1]
        y_parts.append(cb * dt_x[:, rows] * xa[:, rows] + ch * decay_x[:, rows])
        h_new = decay_col[rows] * h_g + dtx_col[rows] * b_g
        hs_out[0, g * SSD_HPG:(g + 1) * SSD_HPG] = h_new.reshape(SSD_HPG, SSD_HEAD_DIM, SSD_STATE)
    y = jnp.concatenate(y_parts, axis=1) + dskip_ref[...] * xa
    y = y * _silu(z[:, 0:GROUP_W])
    nga = nga_ref[...]
    for g in range(SSD_GROUPS):
        rows = slice(g * gw, (g + 1) * gw)
        ya_ref[0, :, rows] = _rms(y[:, rows], nga[:, rows])

    um = xm_ref[0]
    mbuf = mbuf_ref[0]
    cwb = cwb_ref[...]
    convb = cbb_ref[...] + cwb[3:4] * um
    for j in range(CONV_W - 1):
        convb = convb + cwb[j:j + 1] * mbuf[j:j + 1]
    mbuf_out[0, 0:CONV_W - 2, :] = mbuf[1:CONV_W - 1]
    mbuf_out[0, CONV_W - 2:CONV_W - 1, :] = um
    xmc = _silu(convb)
    gates = misc + gb_ref[...]
    lf_all = -_softplus(-gates)
    mm = mm_ref[0]
    ngb = ngb_ref[...]
    o_all = o_ref[0]
    for h in range(ML_HEADS):
        sl = slice(h * ML_HEAD_DIM, (h + 1) * ML_HEAD_DIM)
        li = gates[:, MISC_I + h:MISC_I + h + 1]
        lf = lf_all[:, MISC_F + h:MISC_F + h + 1]
        m_prev = mm[:, h:h + 1]
        l_prev = lf + m_prev
        m_t = jnp.maximum(l_prev, li)
        xmc8 = jnp.broadcast_to(xmc[:, sl], (SUBLANES, ML_HEAD_DIM)).astype(BF16)
        um8 = jnp.broadcast_to(um[:, sl], (SUBLANES, ML_HEAD_DIM)).astype(BF16)
        q8 = _dot(xmc8, wq_ref[h])
        q = q8[0:1]
        k = _dot(xmc8, wk_ref[h])[0:1] * (ML_HEAD_DIM ** -0.5)
        v = _dot(um8, wv_ref[h])[0:1]
        qk = jnp.sum(q * k, axis=1, keepdims=True)
        sc = qk * jnp.exp(li - m_t)
        g_prev = jnp.exp(l_prev - m_t)
        cs = cm_ref[0, h]
        ns = nm_ref[0, h:h + 1, :]
        cq = _dot_nt(q8.astype(BF16), cs.astype(BF16))[0:1]
        num = sc * v + g_prev * cq
        den = sc + g_prev * jnp.sum(ns * q, axis=1, keepdims=True)
        hout = num / jnp.maximum(jnp.abs(den), jnp.exp(-m_t))
        wend = jnp.exp(li - m_t)
        decay = jnp.exp(l_prev - m_t)
        cm_out[0, h] = decay * cs + _col_form(wend * v, ML_HEAD_DIM) * k
        nm_out[0, h:h + 1, :] = decay * ns + wend * k
        mm_out[0, :, h:h + 1] = m_t
        hb = jax.nn.sigmoid(o_all[:, sl]) * hout
        yb_ref[0, :, sl] = _rms(hb, ngb[:, sl]) * _silu(z[:, GROUP_W + h * ML_HEAD_DIM:GROUP_W + (h + 1) * ML_HEAD_DIM])


def _dec_state(xbc, xm, o, z, misc, states, layer, lp):
    b = xbc.shape[0]
    tok = lambda w: pl.BlockSpec((1, 1, w), lambda bi: (bi, 0, 0))
    at_layer = lambda t: pl.BlockSpec((1, 1) + t.shape[2:], lambda bi: (layer, bi) + (0,) * (t.ndim - 2))
    state_specs = [at_layer(t) for t in states]
    param_specs = [_const_spec((CONV_W, SSD_CONV_CH)), _const_spec((1, SSD_CONV_CH)),
                   _const_spec((1, LANES)), _const_spec((1, LANES)), _const_spec((LANES, GROUP_W)),
                   _const_spec((1, GROUP_W)), _const_spec((1, GROUP_W)),
                   _const_spec((CONV_W, GROUP_W)), _const_spec((1, GROUP_W)),
                   _const_spec((ML_HEADS, ML_HEAD_DIM, ML_HEAD_DIM)),
                   _const_spec((ML_HEADS, ML_HEAD_DIM, ML_HEAD_DIM)),
                   _const_spec((ML_HEADS, ML_HEAD_DIM, ML_HEAD_DIM)),
                   _const_spec((1, LANES)), _const_spec((1, GROUP_W))]
    params = [lp["conv_a_w"], lp["conv_a_b"], lp["dtb_row"], lp["alog_row"], lp["head_expand"], lp["dskip_row"],
              lp["ssd_norm_g"], lp["conv_b_w"], lp["conv_b_b"], lp["w_mq"], lp["w_mk"], lp["w_mv"], lp["gb_row"],
              lp["ml_norm_g"]]
    inputs = [xbc, xm, o, z, misc] + list(states) + params
    in_specs = [tok(SSD_CONV_CH), tok(GROUP_W), tok(GROUP_W), tok(D_MIX), tok(MISC_W)] + state_specs + param_specs
    aliases = {5 + i: 2 + i for i in range(N_DEC_STATES)}
    outs = pl.pallas_call(
        _dec_state_kernel,
        grid=(b,),
        in_specs=in_specs,
        out_specs=[tok(GROUP_W), tok(GROUP_W)] + state_specs,
        out_shape=[jax.ShapeDtypeStruct((b, 1, GROUP_W), F32)] * 2
        + [jax.ShapeDtypeStruct(t.shape, F32) for t in states],
        input_output_aliases=aliases,
        compiler_params=_cparams(("parallel",)),
        name="dec_state",
    )(*inputs)
    return outs[0], outs[1], tuple(outs[2:])


def _mla_decode_kernel(pt_ref, lat_hbm, kr_hbm, qn_ref, qr_ref, qfull_ref, kfull_ref, vnew_ref, zc_ref,
                       wukt_ref, wuv_ref, kng_ref, out_ref,
                       lat_buf, kr_buf, sem, qa_scr, m_scr, l_scr, acc_scr, *, layer, pps, page):
    b = pl.program_id(0)
    s = pl.program_id(1)
    ns = pl.num_programs(1)
    nb = pl.num_programs(0)
    step = b * ns + s
    slot = step % 2
    t = pps * page

    def copies(bb, ss, sl):
        out = []
        for p in range(pps):
            pg = pt_ref[bb, ss * pps + p]
            out.append(pltpu.make_async_copy(lat_hbm.at[layer, pg], lat_buf.at[sl, pl.ds(p * page, page), :],
                                             sem.at[0, sl]))
            out.append(pltpu.make_async_copy(kr_hbm.at[layer, pg], kr_buf.at[sl, :, pl.ds(p * page, page)],
                                             sem.at[1, sl]))
        return out

    @pl.when(step == 0)
    def _():
        for cp in copies(b, s, slot):
            cp.start()

    @pl.when(step + 1 < nb * ns)
    def _():
        nxt = step + 1
        for cp in copies(nxt // ns, nxt % ns, 1 - slot):
            cp.start()

    scale = MLA_QK ** -0.5

    @pl.when(s == 0)
    def _():
        qng = (qn_ref[0].astype(F32) * kng_ref[...]).astype(BF16)
        for h in range(MLA_HEADS):
            full = _dot(qng, wukt_ref[h * MLA_NOPE:(h + 1) * MLA_NOPE, :])
            qa_scr[h:h + 1, :] = full[h:h + 1]
            qa_scr[MLA_HEADS + h:MLA_HEADS + h + 1, :] = full[h:h + 1]
        m_scr[...] = jnp.full_like(m_scr, -jnp.inf)
        l_scr[...] = jnp.zeros_like(l_scr)
        acc_scr[...] = jnp.zeros_like(acc_scr)

    for cp in copies(b, s, slot):
        cp.wait()

    latb = lat_buf[slot].astype(BF16)
    krb = kr_buf[slot].astype(BF16)
    kraw_t = _dot_nt(wukt_ref[...], latb)
    sa = _dot_nt(qa_scr[...].astype(BF16), latb)
    sr = _dot(qr_ref[0], krb)
    rinv_rows = []
    for h in range(MLA_HEADS):
        kh = kraw_t[h * MLA_NOPE:(h + 1) * MLA_NOPE, :]
        ssq = jnp.sum(kh * kh, axis=0, keepdims=True)
        rinv_rows.append(lax.rsqrt(ssq * (1.0 / MLA_NOPE) + EPS))
    rinv = jnp.concatenate(rinv_rows + rinv_rows, axis=0)
    sc = (sa * rinv + sr) * scale
    m_prev = m_scr[...]
    m_new = jnp.maximum(m_prev, jnp.max(sc, axis=1, keepdims=True))
    alpha = jnp.exp(m_prev - m_new)
    p = jnp.exp(sc - m_new)
    l_scr[...] = alpha * l_scr[...] + jnp.sum(p, axis=1, keepdims=True)
    acc_scr[...] = alpha * acc_scr[...] + _dot(p.astype(BF16), latb)
    m_scr[...] = m_new

    @pl.when(s == ns - 1)
    def _():
        s_new = jnp.sum(qfull_ref[0].astype(F32) * kfull_ref[0].astype(F32), axis=1, keepdims=True) * scale
        m_prev2 = m_scr[...]
        m_fin = jnp.maximum(m_prev2, s_new)
        alpha2 = jnp.exp(m_prev2 - m_fin)
        p_new = jnp.exp(s_new - m_fin)
        l_fin = alpha2 * l_scr[...] + p_new
        acc = alpha2 * acc_scr[...] + p_new * vnew_ref[0].astype(F32)
        ctx = (acc / l_fin).astype(BF16)
        yfull = _dot(ctx, wuv_ref[...])
        parts = [yfull[h:h + 1, h * MLA_V:(h + 1) * MLA_V] for h in range(MLA_HEADS)]
        out_ref[0] = jnp.concatenate(parts, axis=1) * _silu(zc_ref[0])


def _mla_decode(page_table, cache_lat, cache_kr, layer, qn8, qr8, qfull8, kfull8, vnew, z, lp, pps=16):
    b, n_pages = page_table.shape
    page = cache_lat.shape[2]
    ns = n_pages // pps
    t = pps * page
    kern = functools.partial(_mla_decode_kernel, layer=layer, pps=pps, page=page)
    per_b = lambda shape: pl.BlockSpec((1,) + shape, lambda bi, si, pt: (bi, 0, 0))
    cst = lambda shape: pl.BlockSpec(shape, lambda bi, si, pt: (0,) * len(shape))
    grid_spec = pltpu.PrefetchScalarGridSpec(
        num_scalar_prefetch=1,
        grid=(b, ns),
        in_specs=[pl.BlockSpec(memory_space=pl.ANY), pl.BlockSpec(memory_space=pl.ANY),
                  per_b((SUBLANES, MLA_NOPE)), per_b((SUBLANES, MLA_ROPE)),
                  per_b((SUBLANES, MLA_QK)), per_b((SUBLANES, MLA_QK)), per_b((1, KV_RANK)),
                  pl.BlockSpec((1, 1, GROUP_W), lambda bi, si, pt: (bi, 0, 2)),
                  cst((MLA_HEADS * MLA_NOPE, KV_RANK)), cst((KV_RANK, GROUP_W)), cst((1, MLA_NOPE))],
        out_specs=pl.BlockSpec((1, 1, GROUP_W), lambda bi, si, pt: (bi, 0, 0)),
        scratch_shapes=[pltpu.VMEM((2, t, KV_RANK), F32), pltpu.VMEM((2, MLA_ROPE, t), F32),
                        pltpu.SemaphoreType.DMA((2, 2)),
                        pltpu.VMEM((SUBLANES, KV_RANK), F32),
                        pltpu.VMEM((SUBLANES, 1), F32), pltpu.VMEM((SUBLANES, 1), F32),
                        pltpu.VMEM((SUBLANES, KV_RANK), F32)],
    )
    return pl.pallas_call(
        kern,
        grid_spec=grid_spec,
        out_shape=jax.ShapeDtypeStruct((b, 1, GROUP_W), F32),
        compiler_params=_cparams(("arbitrary", "arbitrary")),
        name="mla_decode",
    )(page_table, cache_lat, cache_kr, qn8, qr8, qfull8, kfull8, vnew, z, lp["w_uk_t"], lp["w_uv"],
      lp["kn_nope_g"])


def _pack_w_in(w_in):
    names = ("z", "xbc", "dt", "xm", "i", "f", "o", "cq", "ckv", "kr", "xq")
    offs = {}
    o = 0
    for nme, w in zip(names, IN_SIZES):
        offs[nme] = (o, o + w)
        o += w
    col = lambda nme: w_in[:, :, offs[nme][0]:offs[nme][1]].astype(BF16)
    pad = jnp.zeros(w_in.shape[:2] + (MISC_W - (MISC_F + ML_HEADS),), BF16)
    misc = jnp.concatenate([col("kr"), col("dt"), col("i"), col("f"), pad], axis=-1)
    return [misc if nme == "misc" else col(nme) for nme, _ in PACK_SEGS]


def _layer_params(l, p):
    depth_row = lambda a: a[l][None, :]
    lp = {}
    lp["norm_g"] = depth_row(p["norm_g"])
    lp["conv_a_w"] = p["conv_a_w"][l]
    lp["conv_a_b"] = depth_row(p["conv_a_b"])
    lp["dtb_row"] = _lane_pad(p["dt_bias"][l], MISC_DT)
    lp["alog_row"] = _lane_pad(p["a_log"][l], MISC_DT)
    lp["dtb_col"] = p["dt_bias"][l][:, None]
    lp["alog_col"] = p["a_log"][l][:, None]
    lp["dskip_row"] = jnp.repeat(p["d_skip"][l], SSD_HEAD_DIM)[None, :]
    lp["ssd_norm_g"] = depth_row(p["ssd_norm_g"])
    lp["conv_b_w"] = p["conv_b_w"][l]
    lp["conv_b_b"] = depth_row(p["conv_b_b"])
    lp["w_mq"] = p["w_mq"][l].astype(BF16)
    lp["w_mk"] = p["w_mk"][l].astype(BF16)
    lp["w_mv"] = p["w_mv"][l].astype(BF16)
    lp["w_mq_t"] = jnp.swapaxes(lp["w_mq"], 1, 2)
    lp["w_mv_t"] = jnp.swapaxes(lp["w_mv"], 1, 2)
    gate_b = jnp.concatenate([p["b_i"][l], p["b_f"][l]])
    lp["gb_row"] = _lane_pad(gate_b, MISC_I)
    lp["gb_col"] = gate_b[:, None]
    lp["ml_norm_g"] = depth_row(p["ml_norm_g"])
    lp["q_norm_g"] = depth_row(p["q_norm_g"])
    lp["kv_norm_g"] = depth_row(p["kv_norm_g"])
    w_uq = p["w_uq"][l]
    lp["w_uq_nope"] = w_uq[:, :, :MLA_NOPE].reshape(Q_RANK, MLA_HEADS * MLA_NOPE).astype(BF16)
    rope_pad = lambda t: jnp.pad(t, [(0, 0)] * (t.ndim - 1) + [(0, LANES - MLA_ROPE)])
    lp["w_uq_rope"] = rope_pad(jnp.transpose(w_uq[:, :, MLA_NOPE:], (1, 0, 2))).astype(BF16)
    w_uk = p["w_uk"][l].reshape(KV_RANK, MLA_HEADS * MLA_NOPE)
    lp["w_uk"] = w_uk.astype(BF16)
    lp["w_uk_t"] = w_uk.T.astype(BF16)
    lp["w_uv"] = p["w_uv"][l].reshape(KV_RANK, MLA_HEADS * MLA_V).astype(BF16)
    for nme in ("qn_nope_g", "kn_nope_g", "xq_norm_g"):
        lp[nme] = depth_row(p[nme])
    lp["qn_rope_g_pad"] = rope_pad(depth_row(p["qn_rope_g"]))
    lp["kn_rope_g_pad"] = rope_pad(depth_row(p["kn_rope_g"]))
    lp["w_out"] = p["w_out"][l].astype(BF16)
    half = MLA_ROPE // 2
    lane = jnp.arange(LANES)
    partner = jnp.where(lane < half, lane + half, lane - half)
    lp["rope_swap"] = ((lane[:, None] == partner[None, :]) & (lane[None, :] < MLA_ROPE)).astype(F32)
    head_of_lane = jnp.arange(LANES)[:, None] - MISC_DT
    head_of_chan = jnp.arange(GROUP_W)[None, :] // SSD_HEAD_DIM
    lp["head_expand"] = (head_of_lane == head_of_chan).astype(F32)
    return lp


def _rope_tables(pos):
    half = MLA_ROPE // 2
    inv = ROPE_BASE ** (-jnp.arange(half, dtype=F32) * 2.0 / MLA_ROPE)
    ang = pos.astype(F32)[:, None] * inv[None, :]
    cos = jnp.cos(ang)
    sin = jnp.sin(ang)
    zero = jnp.zeros((pos.shape[0], LANES - MLA_ROPE), F32)
    return jnp.concatenate([cos, cos, zero], axis=1), jnp.concatenate([-sin, sin, zero], axis=1)


def _prompt_layer(lp, w_slabs, layer, x, mk_all, mv_all, rope_tabs, final_g, final):
    b, l, _ = x.shape
    n = b * l
    x2d = x.reshape(n, D_MODEL)
    z, xbc, xm, o, cq, ckv, xq, misc = _inproj(x2d, lp["norm_g"], w_slabs, layer, tm=512)
    r3 = lambda t: t.reshape(b, l, t.shape[-1])
    z, xbc, xm, o, cq, ckv, xq, misc = map(r3, (z, xbc, xm, o, cq, ckv, xq, misc))
    ya, ssm_h, ssm_buf = _ssd_prompt(xbc, misc, z, lp)
    yb, ml_c, ml_n, ml_m, ml_buf = _mlstm_prompt(xm, o, z, misc, lp)
    qt, k, latt, lat, kr = _mla_prep(cq, ckv, misc, rope_tabs[0], rope_tabs[1], lp, tm=256, transposed=True)
    yc = _mla_attn_prompt(qt, k, latt, z, lp["w_uv"])
    yd = _xattn(xq, mk_all, mv_all, layer, z, lp["xq_norm_g"], tm=512, interleaved=False)
    ys = [t.reshape(n, GROUP_W) for t in (ya, yb, yc, yd)]
    x_new = _outproj(ys, x2d, lp["w_out"], final_g, tm=512, final=final).reshape(b, l, D_MODEL)
    return x_new, (lat, kr, ssm_h, ssm_buf, ml_c, ml_n, ml_m, ml_buf)


def _sample_layer(lp, w_slabs, layer, x, mem_k, mem_v, cache_lat, cache_kr, page_table, states,
                  rope_tabs, final_g, final, pps=16):
    b = x.shape[0]
    x2d = x.reshape(b, D_MODEL)
    z, xbc, xm, o, cq, ckv, xq, misc = _inproj(x2d, lp["norm_g"], w_slabs, layer, tm=b)
    tok = lambda t: t.reshape(b, 1, t.shape[-1])
    ya, yb, st_new = _dec_state(tok(xbc), tok(xm), tok(o), tok(z), tok(misc), states, layer, lp)
    one = lambda t: t.reshape(1, b, t.shape[-1])
    q, k, latb, lat, kr = _mla_prep(one(cq), one(ckv), one(misc), rope_tabs[0], rope_tabs[1], lp, tm=b,
                                    transposed=False)
    qh = jnp.transpose(q[0], (1, 0, 2))
    kh = jnp.transpose(k[0], (1, 0, 2))
    dup = lambda t: jnp.concatenate([t, t], axis=1)
    qfull8 = dup(qh)
    kfull8 = dup(kh)
    qn8 = qfull8[:, :, :MLA_NOPE]
    qr8 = qfull8[:, :, MLA_NOPE:]
    vnew = latb.reshape(b, 1, KV_RANK)
    yc = _mla_decode(page_table, cache_lat, cache_kr, layer, qn8, qr8, qfull8, kfull8, vnew, tok(z), lp, pps=pps)
    yd = _xattn(tok(xq), mem_k, mem_v, layer, tok(z), lp["xq_norm_g"], tm=1, interleaved=True)
    ys = [t.reshape(b, GROUP_W) for t in (ya, yb, yc, yd)]
    x_new = _outproj(ys, x2d, lp["w_out"], final_g, tm=b, final=final).reshape(b, 1, D_MODEL)
    return x_new, lat.reshape(b, 1, KV_RANK), kr.reshape(b, 1, MLA_ROPE), st_new


def kernel(x_prompt, x_sample, mem_prompt, cache_mla_latent, cache_mla_krope, cache_mem_k, cache_mem_v, state_ssm, state_ssm_conv, state_mlstm_C, state_mlstm_n, state_mlstm_m, state_mlstm_conv, page_table, norm_g, w_in, conv_a_w, conv_a_b, dt_bias, a_log, d_skip, ssd_norm_g, conv_b_w, conv_b_b, w_mq, w_mk, w_mv, b_i, b_f, ml_norm_g, q_norm_g, kv_norm_g, w_uq, w_uk, w_uv, qn_nope_g, qn_rope_g, kn_nope_g, kn_rope_g, mem_norm_g, w_mem_k, w_mem_v, xk_norm_g, xq_norm_g, w_out, final_norm_g):
    params = dict(norm_g=norm_g, conv_a_w=conv_a_w, conv_a_b=conv_a_b, dt_bias=dt_bias, a_log=a_log, d_skip=d_skip,
                  ssd_norm_g=ssd_norm_g, conv_b_w=conv_b_w, conv_b_b=conv_b_b, w_mq=w_mq, w_mk=w_mk, w_mv=w_mv,
                  b_i=b_i, b_f=b_f, ml_norm_g=ml_norm_g, q_norm_g=q_norm_g, kv_norm_g=kv_norm_g, w_uq=w_uq,
                  w_uk=w_uk, w_uv=w_uv, qn_nope_g=qn_nope_g, qn_rope_g=qn_rope_g, kn_nope_g=kn_nope_g,
                  kn_rope_g=kn_rope_g, xq_norm_g=xq_norm_g, w_out=w_out)
    depth = w_in.shape[0]
    bp, lp_len, _ = x_prompt.shape
    bs, ls_len, _ = x_sample.shape
    n_pages = page_table.shape[1]
    past_len = n_pages * cache_mla_latent.shape[2]
    mem_len = mem_prompt.shape[1]
    w_slabs = _pack_w_in(w_in)
    final_g = final_norm_g[None, :]

    cache_kr_t = jnp.swapaxes(cache_mla_krope, 2, 3)
    rope_p = _rope_tables(jnp.arange(lp_len, dtype=jnp.int32))
    rope_s = _rope_tables(jnp.full((bs,), past_len, dtype=jnp.int32))

    mk_all, mv_all = _memkv(mem_prompt, mem_norm_g[:, None, :], w_mem_k.astype(BF16), w_mem_v.astype(BF16),
                            xk_norm_g[:, None, :])

    mem_rows = cache_mem_k.shape[2] * X_HEADS
    cache_mk = cache_mem_k.reshape(depth, bs, mem_rows, X_HEAD_DIM)
    cache_mv = cache_mem_v.reshape(depth, bs, mem_rows, X_HEAD_DIM)
    states = (state_ssm, state_ssm_conv, state_mlstm_C, state_mlstm_n,
              state_mlstm_m.reshape(depth, bs, 1, ML_HEADS), state_mlstm_conv)

    hp, hs = x_prompt, x_sample
    p_states, s_lats, s_krs = [], [], []
    for l in range(depth):
        lp = _layer_params(l, params)
        final = l == depth - 1
        hp, st_p = _prompt_layer(lp, w_slabs, l, hp, mk_all, mv_all, rope_p, final_g, final)
        p_states.append(st_p)
        hs, s_lat_l, s_kr_l, states = _sample_layer(lp, w_slabs, l, hs, cache_mk, cache_mv, cache_mla_latent,
                                                    cache_kr_t, page_table, states, rope_s, final_g, final)
        s_lats.append(s_lat_l)
        s_krs.append(s_kr_l)

    p_out = [jnp.stack([st[i] for st in p_states]) for i in range(8)]
    p_mem_k = mk_all.reshape(depth, bp, mem_len, X_HEADS, X_HEAD_DIM)
    p_mem_v = mv_all.reshape(depth, bp, mem_len, X_HEADS, X_HEAD_DIM)
    (p_lat, p_kr, p_ssm, p_sconv, p_mc, p_mn, p_mm, p_mconv) = p_out
    s_ssm, s_sconv, s_mc, s_mn, s_mm, s_mconv = states
    return (hp, hs, p_lat, p_kr, p_mem_k, p_mem_v, p_ssm, p_sconv, p_mc, p_mn, p_mm, p_mconv,
            jnp.stack(s_lats), jnp.stack(s_krs), s_ssm, s_sconv, s_mc, s_mn, s_mm.reshape(depth, bs, ML_HEADS),
            s_mconv)
```

```python
import functools
import math

import jax
import jax.numpy as jnp
from jax import lax
from jax.experimental import pallas as pl
from jax.experimental.pallas import tpu as pltpu

F32 = jnp.float32
BF16 = jnp.bfloat16
HIGHEST = lax.Precision.HIGHEST

D_MODEL = 1024
D_MIX = 2 * D_MODEL
GROUP_W = D_MIX // 4
SSD_HEAD_DIM = 64
SSD_HEADS = GROUP_W // SSD_HEAD_DIM
SSD_GROUPS = 2
SSD_HPG = SSD_HEADS // SSD_GROUPS
SSD_STATE = 128
CONV_W = 4
SSD_CONV_CH = GROUP_W + 2 * SSD_GROUPS * SSD_STATE
ML_HEADS = 4
ML_HEAD_DIM = GROUP_W // ML_HEADS
MLA_HEADS = 4
MLA_NOPE = 128
MLA_ROPE = 64
MLA_QK = MLA_NOPE + MLA_ROPE
MLA_V = GROUP_W // MLA_HEADS
Q_RANK = 3 * D_MODEL // 8
KV_RANK = D_MODEL // 4
ROPE_BASE = 10000.0
X_HEADS = 4
X_HEAD_DIM = GROUP_W // X_HEADS
CHUNK = 256
EPS = 1e-6
IN_SIZES = (D_MIX, SSD_CONV_CH, SSD_HEADS, GROUP_W, ML_HEADS, ML_HEADS, GROUP_W, Q_RANK, KV_RANK, MLA_ROPE, GROUP_W)

LANES = 128
SUBLANES = 8
VMEM_LIMIT = 56 * 1024 * 1024

MISC_W = LANES
MISC_KR = 0
MISC_DT = MLA_ROPE
MISC_I = MISC_DT + SSD_HEADS
MISC_F = MISC_I + ML_HEADS
PACK_SEGS = (("z", D_MIX), ("xbc", SSD_CONV_CH), ("xm", GROUP_W), ("o", GROUP_W), ("cq", Q_RANK),
             ("ckv", KV_RANK), ("xq", GROUP_W), ("misc", MISC_W))
PACK_W = sum(w for _, w in PACK_SEGS)


def _cparams(sem):
    return pltpu.CompilerParams(dimension_semantics=sem, vmem_limit_bytes=VMEM_LIMIT)


def _dot(a, b):
    return jnp.dot(a, b, preferred_element_type=F32)


def _dot_nt(a, b, precision=None):
    return lax.dot_general(a, b, (((1,), (1,)), ((), ())), preferred_element_type=F32, precision=precision)


def _dot_tn(a, b):
    return lax.dot_general(a, b, (((0,), (0,)), ((), ())), preferred_element_type=F32)


def _rms(x, g):
    return x * lax.rsqrt(jnp.mean(x * x, axis=-1, keepdims=True) + EPS) * g


def _silu(x):
    return x * jax.nn.sigmoid(x)


def _softplus(x):
    return jnp.maximum(x, 0.0) + jnp.log1p(jnp.exp(-jnp.abs(x)))


def _const_spec(shape):
    nd = len(shape)
    return pl.BlockSpec(shape, lambda *_: (0,) * nd)


def _layer_spec(arr, layer):
    nd = arr.ndim
    return pl.BlockSpec((None,) + arr.shape[1:], lambda *_: (layer,) + (0,) * (nd - 1))


def _layer_args(lp, layer, names):
    return [lp[n] for n in names], [_layer_spec(lp[n], layer) for n in names]


def _inproj_kernel(x_ref, g_ref, *refs):
    nseg = len(PACK_SEGS)
    w_refs, out_refs = refs[:nseg], refs[nseg:]
    x = x_ref[...]
    u = _rms(x, g_ref[...]).astype(BF16)
    for w_ref, out_ref in zip(w_refs, out_refs):
        out_ref[...] = _dot(u, w_ref[0])


def _inproj(x2d, g, w_slabs, layer, tm):
    n = x2d.shape[0]
    out_shape = [jax.ShapeDtypeStruct((n, w), F32) for _, w in PACK_SEGS]
    out_specs = [pl.BlockSpec((tm, w), lambda i: (i, 0)) for _, w in PACK_SEGS]
    w_specs = [pl.BlockSpec((1, D_MODEL, w), lambda i: (layer, 0, 0), pipeline_mode=pl.Buffered(1))
               for _, w in PACK_SEGS]
    return pl.pallas_call(
        _inproj_kernel,
        grid=(n // tm,),
        in_specs=[pl.BlockSpec((tm, D_MODEL), lambda i: (i, 0)), _layer_spec(g, layer)] + w_specs,
        out_specs=out_specs,
        out_shape=out_shape,
        compiler_params=_cparams(("parallel",)),
        name="inproj",
    )(x2d, g, *w_slabs)


def _tri_masks():
    r = lax.broadcasted_iota(jnp.int32, (CHUNK, CHUNK), 0)
    c = lax.broadcasted_iota(jnp.int32, (CHUNK, CHUNK), 1)
    return r >= c, c >= r


def _ssd_kernel(xbc_ref, misc_ref, za_ref, cw_ref, cb_ref, dtb_row_ref, alog_row_ref, dtb_col_ref,
                alog_col_ref, dskip_ref, ng_ref, ya_ref, hfin_ref, conv_ref, h_scr, ext_scr, y_scr):
    c = pl.program_id(1)
    nc = pl.num_programs(1)

    @pl.when(c == 0)
    def _():
        h_scr[...] = jnp.zeros_like(h_scr)
        ext_scr[0:SUBLANES, :] = jnp.zeros((SUBLANES, SSD_CONV_CH), F32)

    u = xbc_ref[0]
    ext_scr[SUBLANES:SUBLANES + CHUNK, :] = u
    cw = cw_ref[...]
    conv = cb_ref[...] + cw[3:4] * u
    for j in range(CONV_W - 1):
        back = CONV_W - 1 - j
        conv = conv + cw[j:j + 1] * ext_scr[SUBLANES - back:SUBLANES - back + CHUNK, :]
    ext_scr[0:SUBLANES, :] = u[CHUNK - SUBLANES:CHUNK, :]
    xbcs = _silu(conv)
    xa = xbcs[:, 0:GROUP_W]

    misc = misc_ref[0]
    misc_t = misc.T
    lower, upper = _tri_masks()
    lower_f = lower.astype(F32)
    upper_f = upper.astype(F32)
    dt_col = _softplus(misc + dtb_row_ref[...])
    dt_row = _softplus(misc_t[MISC_DT:MISC_DT + SSD_HEADS, :] + dtb_col_ref[...])
    a_row = -jnp.exp(alog_row_ref[...])
    a_col = -jnp.exp(alog_col_ref[...])
    acum_col = jnp.dot(lower_f, dt_col * a_row, precision=HIGHEST, preferred_element_type=F32)
    acum_row = jnp.dot(dt_row * a_col, upper_f, precision=HIGHEST, preferred_element_type=F32)

    for g in range(SSD_GROUPS):
        b_g = xbcs[:, GROUP_W + g * SSD_STATE:GROUP_W + (g + 1) * SSD_STATE].astype(BF16)
        c_off = GROUP_W + SSD_GROUPS * SSD_STATE
        c_g = xbcs[:, c_off + g * SSD_STATE:c_off + (g + 1) * SSD_STATE].astype(BF16)
        cb = _dot_nt(c_g, b_g)
        for hh in range(SSD_HPG):
            h = g * SSD_HPG + hh
            ac_col = acum_col[:, MISC_DT + h:MISC_DT + h + 1]
            ac_row = acum_row[h:h + 1, :]
            seg = jnp.where(lower, ac_col - ac_row, -jnp.inf)
            w = cb * jnp.exp(seg) * dt_row[h:h + 1, :]
            x_h = xa[:, h * SSD_HEAD_DIM:(h + 1) * SSD_HEAD_DIM]
            hs = h_scr[h]
            y_h = _dot(w.astype(BF16), x_h.astype(BF16)) + _dot_nt(c_g, hs.astype(BF16)) * jnp.exp(ac_col)
            y_scr[:, h * SSD_HEAD_DIM:(h + 1) * SSD_HEAD_DIM] = y_h
            ac_last = ac_col[CHUNK - 1:CHUNK, :]
            wend = jnp.exp(ac_last - ac_col) * dt_col[:, MISC_DT + h:MISC_DT + h + 1]
            xw = (x_h * wend).astype(BF16)
            h_scr[h] = jnp.exp(ac_last) * hs + _dot_tn(xw, b_g)

    y = y_scr[...] + dskip_ref[...] * xa
    y = y * _silu(za_ref[0])
    gw = GROUP_W // SSD_GROUPS
    ng = ng_ref[...]
    for g in range(SSD_GROUPS):
        ya_ref[0, :, g * gw:(g + 1) * gw] = _rms(y[:, g * gw:(g + 1) * gw], ng[:, g * gw:(g + 1) * gw])

    @pl.when(c == nc - 1)
    def _():
        hfin_ref[0] = h_scr[...]
        conv_ref[0] = xbc_ref[0, CHUNK - (CONV_W - 1):CHUNK, :]


def _lane_pad(v, offset):
    return jnp.pad(v, ((0, 0), (offset, LANES - offset - v.shape[1])))[:, None, :]


def _ssd_prompt(xbc, misc, z, lp, layer):
    b, l, _ = xbc.shape
    nc = l // CHUNK
    row = lambda bi, ci: (bi, ci, 0)
    params, param_specs = _layer_args(lp, layer, ("conv_a_w", "conv_a_b", "dtb_row", "alog_row", "dtb_col",
                                                  "alog_col", "dskip_row", "ssd_norm_g"))
    return pl.pallas_call(
        _ssd_kernel,
        grid=(b, nc),
        in_specs=[pl.BlockSpec((1, CHUNK, SSD_CONV_CH), row),
                  pl.BlockSpec((1, CHUNK, MISC_W), row),
                  pl.BlockSpec((1, CHUNK, GROUP_W), lambda bi, ci: (bi, ci, 0)),
                  ] + param_specs,
        out_specs=[pl.BlockSpec((1, CHUNK, GROUP_W), row),
                   pl.BlockSpec((1, SSD_HEADS, SSD_HEAD_DIM, SSD_STATE), lambda bi, ci: (bi, 0, 0, 0)),
                   pl.BlockSpec((1, CONV_W - 1, SSD_CONV_CH), lambda bi, ci: (bi, 0, 0))],
        out_shape=[jax.ShapeDtypeStruct((b, l, GROUP_W), F32),
                   jax.ShapeDtypeStruct((b, SSD_HEADS, SSD_HEAD_DIM, SSD_STATE), F32),
                   jax.ShapeDtypeStruct((b, CONV_W - 1, SSD_CONV_CH), F32)],
        scratch_shapes=[pltpu.VMEM((SSD_HEADS, SSD_HEAD_DIM, SSD_STATE), F32),
                        pltpu.VMEM((SUBLANES + CHUNK, SSD_CONV_CH), F32),
                        pltpu.VMEM((CHUNK, GROUP_W), F32)],
        compiler_params=_cparams(("parallel", "arbitrary")),
        name="ssd_prompt",
    )(xbc, misc, z, *params)


def _mlstm_kernel(xm_ref, o_ref, zb_ref, misc_ref, cw_ref, cb_ref, wqt_ref, wk_ref, wvt_ref, gb_row_ref,
                  gb_col_ref, ng_ref, yb_ref, cfin_ref, nfin_ref, mfin_ref, conv_ref,
                  c_scr, n_scr, m_scr, ext_scr):
    c = pl.program_id(1)
    nc = pl.num_programs(1)

    @pl.when(c == 0)
    def _():
        c_scr[...] = jnp.zeros_like(c_scr)
        n_scr[...] = jnp.zeros_like(n_scr)
        m_scr[...] = jnp.zeros_like(m_scr)
        ext_scr[0:SUBLANES, :] = jnp.zeros((SUBLANES, GROUP_W), F32)

    u = xm_ref[0]
    ext_scr[SUBLANES:SUBLANES + CHUNK, :] = u
    cw = cw_ref[...]
    conv = cb_ref[...] + cw[3:4] * u
    for j in range(CONV_W - 1):
        back = CONV_W - 1 - j
        conv = conv + cw[j:j + 1] * ext_scr[SUBLANES - back:SUBLANES - back + CHUNK, :]
    ext_scr[0:SUBLANES, :] = u[CHUNK - SUBLANES:CHUNK, :]
    xmc = _silu(conv)

    misc = misc_ref[0]
    misc_t = misc.T
    lower, upper = _tri_masks()
    gates_col = misc + gb_row_ref[...]
    gates_row = misc_t[MISC_I:MISC_I + 2 * ML_HEADS, :] + gb_col_ref[...]
    lf_col = -_softplus(-gates_col)
    lf_row = -_softplus(-gates_row)
    bcum_col = jnp.dot(lower.astype(F32), lf_col, precision=HIGHEST, preferred_element_type=F32)
    bcum_row = jnp.dot(lf_row, upper.astype(F32), precision=HIGHEST, preferred_element_type=F32)
    ng = ng_ref[...]
    o_all = o_ref[0]
    zb = zb_ref[0]
    m_all = m_scr[...]
    n_all = n_scr[...]
    m_rows, n_rows = [], []
    for h in range(ML_HEADS):
        sl = slice(h * ML_HEAD_DIM, (h + 1) * ML_HEAD_DIM)
        li_r = gates_row[h:h + 1, :]
        bc_r = bcum_row[ML_HEADS + h:ML_HEADS + h + 1, :]
        col = gates_col[:, MISC_I + h:MISC_I + h + 1] - bcum_col[:, MISC_F + h:MISC_F + h + 1]
        m_prev = m_all[h:h + 1, 0:1]
        lw = jnp.where(upper, bc_r + col, -jnp.inf)
        l_prev = bc_r + m_prev
        m_t = jnp.maximum(l_prev, jnp.max(lw, axis=0, keepdims=True))
        xmc_h = xmc[:, sl].astype(BF16)
        qtb = _dot_nt(wqt_ref[h], xmc_h).astype(BF16)
        k = _dot(xmc_h, wk_ref[h]) * (ML_HEAD_DIM ** -0.5)
        kb = k.astype(BF16)
        vt = _dot_nt(wvt_ref[h], u[:, sl].astype(BF16))
        sc = _dot(kb, qtb) * jnp.exp(lw - m_t)
        g_prev = jnp.exp(l_prev - m_t)
        cs = c_scr[h]
        ns = n_all[h:h + 1, :]
        ns8 = jnp.broadcast_to(ns, (SUBLANES, ML_HEAD_DIM)).astype(BF16)
        num = _dot(vt.astype(BF16), sc.astype(BF16)) + g_prev * _dot(cs.astype(BF16), qtb)
        den = jnp.sum(sc, axis=0, keepdims=True) + g_prev * _dot(ns8, qtb)[0:1]
        hout_t = num / jnp.maximum(jnp.abs(den), jnp.exp(-m_t))
        m_new = m_t[:, CHUNK - 1:CHUNK]
        bc_last = bc_r[:, CHUNK - 1:CHUNK]
        wend = jnp.exp(bc_last - bc_r + li_r - m_new)
        decay = jnp.exp(bc_last + m_prev - m_new)
        c_scr[h] = decay * cs + _dot((vt * wend).astype(BF16), kb)
        wend8 = jnp.broadcast_to(wend, (SUBLANES, CHUNK)).astype(BF16)
        n_rows.append(decay * ns + _dot(wend8, kb)[0:1])
        m_rows.append(jnp.broadcast_to(m_new, (1, LANES)))
        hb = jax.nn.sigmoid(o_all[:, sl]) * hout_t.T
        yb_ref[0, :, sl] = _rms(hb, ng[:, sl]) * _silu(zb[:, sl])
    pad = [jnp.zeros((SUBLANES - ML_HEADS, LANES), F32)]
    n_scr[...] = jnp.concatenate(n_rows + pad, axis=0)
    m_scr[...] = jnp.concatenate(m_rows + pad, axis=0)

    @pl.when(c == nc - 1)
    def _():
        cfin_ref[0] = c_scr[...]
        nfin_ref[0] = n_scr[...]
        mfin_ref[0] = m_scr[...]
        conv_ref[0] = xm_ref[0, CHUNK - (CONV_W - 1):CHUNK, :]


def _mlstm_prompt(xm, o, z, misc, lp, layer):
    b, l, _ = xm.shape
    nc = l // CHUNK
    row = lambda bi, ci: (bi, ci, 0)
    params, param_specs = _layer_args(lp, layer, ("conv_b_w", "conv_b_b", "w_mq_t", "w_mk", "w_mv_t", "gb_row",
                                                  "gb_col", "ml_norm_g"))
    outs = pl.pallas_call(
        _mlstm_kernel,
        grid=(b, nc),
        in_specs=[pl.BlockSpec((1, CHUNK, GROUP_W), row),
                  pl.BlockSpec((1, CHUNK, GROUP_W), row),
                  pl.BlockSpec((1, CHUNK, GROUP_W), lambda bi, ci: (bi, ci, 1)),
                  pl.BlockSpec((1, CHUNK, MISC_W), row)] + param_specs,
        out_specs=[pl.BlockSpec((1, CHUNK, GROUP_W), row),
                   pl.BlockSpec((1, ML_HEADS, ML_HEAD_DIM, ML_HEAD_DIM), lambda bi, ci: (bi, 0, 0, 0)),
                   pl.BlockSpec((1, SUBLANES, LANES), lambda bi, ci: (bi, 0, 0)),
                   pl.BlockSpec((1, SUBLANES, LANES), lambda bi, ci: (bi, 0, 0)),
                   pl.BlockSpec((1, CONV_W - 1, GROUP_W), lambda bi, ci: (bi, 0, 0))],
        out_shape=[jax.ShapeDtypeStruct((b, l, GROUP_W), F32),
                   jax.ShapeDtypeStruct((b, ML_HEADS, ML_HEAD_DIM, ML_HEAD_DIM), F32),
                   jax.ShapeDtypeStruct((b, SUBLANES, LANES), F32),
                   jax.ShapeDtypeStruct((b, SUBLANES, LANES), F32),
                   jax.ShapeDtypeStruct((b, CONV_W - 1, GROUP_W), F32)],
        scratch_shapes=[pltpu.VMEM((ML_HEADS, ML_HEAD_DIM, ML_HEAD_DIM), F32),
                        pltpu.VMEM((SUBLANES, LANES), F32),
                        pltpu.VMEM((SUBLANES, LANES), F32),
                        pltpu.VMEM((SUBLANES + CHUNK, GROUP_W), F32)],
        compiler_params=_cparams(("parallel", "arbitrary")),
        name="mlstm_prompt",
    )(xm, o, z, misc, *params)
    yb, cfin, nfin, mfin, mconv = outs
    return yb, cfin, nfin[:, :ML_HEADS, :], mfin[:, :ML_HEADS, 0], mconv


def _rope_rows(r, g, cos2, sinsgn):
    half = MLA_ROPE // 2
    rn = r * lax.rsqrt(jnp.sum(r * r, axis=-1, keepdims=True) * (1.0 / MLA_ROPE) + EPS) * g
    lane = lax.broadcasted_iota(jnp.int32, rn.shape, 1)
    r_sw = jnp.where(lane < half, pltpu.roll(rn, LANES - half, 1), pltpu.roll(rn, half, 1))
    return rn * cos2 + r_sw * sinsgn


def _mla_prep_kernel(cq_ref, ckv_ref, misc_ref, cos_ref, sin_ref, qg_ref, kvg_ref, wqn_ref, wqr_ref,
                     wuk_ref, qnn_ref, qnr_ref, knn_ref, knr_ref, q_ref, k_ref, v_ref, lat_ref, kr_ref,
                     *, transposed):
    cos2 = cos_ref[...]
    sinsgn = sin_ref[...]
    c_q = _rms(cq_ref[0], qg_ref[...]).astype(BF16)
    qn_all = _dot(c_q, wqn_ref[...])
    c_kv = _rms(ckv_ref[0], kvg_ref[...])
    lat_ref[0] = c_kv
    c_kv_b = c_kv.astype(BF16)
    if transposed:
        v_ref[0] = c_kv.T.astype(BF16)
    else:
        v_ref[0] = c_kv_b
    misc = misc_ref[0]
    lane = lax.broadcasted_iota(jnp.int32, misc.shape, 1)
    kr_raw = jnp.where(lane < MLA_ROPE, misc, 0.0)
    kr = _rope_rows(kr_raw, knr_ref[...], cos2, sinsgn)
    kr_ref[0] = kr[:, 0:MLA_ROPE]
    kr_b = kr[:, 0:MLA_ROPE].astype(BF16)
    kn_all = _dot(c_kv_b, wuk_ref[...])
    for h in range(MLA_HEADS):
        sl = slice(h * MLA_NOPE, (h + 1) * MLA_NOPE)
        qn = _rms(qn_all[:, sl], qnn_ref[...])
        qr = _rope_rows(_dot(c_q, wqr_ref[h]), qnr_ref[...], cos2, sinsgn)
        if transposed:
            q_ref[0, h, 0:MLA_NOPE, :] = qn.T.astype(BF16)
            q_ref[0, h, MLA_NOPE:MLA_QK, :] = qr.T[0:MLA_ROPE].astype(BF16)
        else:
            q_ref[0, h, :, 0:MLA_NOPE] = qn.astype(BF16)
            q_ref[0, h, :, MLA_NOPE:MLA_QK] = qr[:, 0:MLA_ROPE].astype(BF16)
        k_ref[0, h, :, 0:MLA_NOPE] = _rms(kn_all[:, sl], knn_ref[...]).astype(BF16)
        k_ref[0, h, :, MLA_NOPE:MLA_QK] = kr_b


def _mla_prep(cq, ckv, misc, cos2, sinsgn, lp, layer, tm, transposed):
    b, l, _ = cq.shape
    params, param_specs = _layer_args(lp, layer, ("q_norm_g", "kv_norm_g", "w_uq_nope", "w_uq_rope", "w_uk",
                                                  "qn_nope_g", "qn_rope_g_pad", "kn_nope_g", "kn_rope_g_pad"))
    row = lambda bi, ti: (bi, ti, 0)
    hrow = lambda bi, ti: (bi, 0, ti, 0)
    if transposed:
        q_spec = pl.BlockSpec((1, MLA_HEADS, MLA_QK, tm), lambda bi, ti: (bi, 0, 0, ti))
        q_shape = jax.ShapeDtypeStruct((b, MLA_HEADS, MLA_QK, l), BF16)
        v_spec = pl.BlockSpec((1, KV_RANK, tm), lambda bi, ti: (bi, 0, ti))
        v_shape = jax.ShapeDtypeStruct((b, KV_RANK, l), BF16)
    else:
        q_spec = pl.BlockSpec((1, MLA_HEADS, tm, MLA_QK), hrow)
        q_shape = jax.ShapeDtypeStruct((b, MLA_HEADS, l, MLA_QK), BF16)
        v_spec = pl.BlockSpec((1, tm, KV_RANK), row)
        v_shape = jax.ShapeDtypeStruct((b, l, KV_RANK), BF16)
    return pl.pallas_call(
        functools.partial(_mla_prep_kernel, transposed=transposed),
        grid=(b, l // tm),
        in_specs=[pl.BlockSpec((1, tm, Q_RANK), row), pl.BlockSpec((1, tm, KV_RANK), row),
                  pl.BlockSpec((1, tm, MISC_W), row),
                  pl.BlockSpec((tm, LANES), lambda bi, ti: (ti, 0)),
                  pl.BlockSpec((tm, LANES), lambda bi, ti: (ti, 0))] + param_specs,
        out_specs=[q_spec,
                   pl.BlockSpec((1, MLA_HEADS, tm, MLA_QK), hrow),
                   v_spec,
                   pl.BlockSpec((1, tm, KV_RANK), row),
                   pl.BlockSpec((1, tm, MLA_ROPE), row)],
        out_shape=[q_shape,
                   jax.ShapeDtypeStruct((b, MLA_HEADS, l, MLA_QK), BF16),
                   v_shape,
                   jax.ShapeDtypeStruct((b, l, KV_RANK), F32),
                   jax.ShapeDtypeStruct((b, l, MLA_ROPE), F32)],
        compiler_params=_cparams(("parallel", "parallel")),
        name="mla_prep",
    )(cq, ckv, misc, cos2, sinsgn, *params)


def _mla_attn_kernel(qt_ref, k_ref, vt_ref, zc_ref, wuv_ref, out_ref, m_scr, l_scr, acc_scr, *, tile):
    qi = pl.program_id(1)
    scale = MLA_QK ** -0.5
    m_scr[...] = jnp.full_like(m_scr, -jnp.inf)
    l_scr[...] = jnp.zeros_like(l_scr)
    acc_scr[...] = jnp.zeros_like(acc_scr)

    def kv_tile(j, diagonal):
        start = pl.multiple_of(j * tile, tile)
        vt = vt_ref[0, :, pl.ds(start, tile)]
        for h in range(MLA_HEADS):
            k = k_ref[0, h, pl.ds(start, tile), :]
            s = _dot(k, qt_ref[0, h]) * scale
            if diagonal:
                kid = lax.broadcasted_iota(jnp.int32, (tile, tile), 0)
                qid = lax.broadcasted_iota(jnp.int32, (tile, tile), 1)
                s = jnp.where(kid <= qid, s, -jnp.inf)
            m_prev = m_scr[h:h + 1, :]
            m_new = jnp.maximum(m_prev, jnp.max(s, axis=0, keepdims=True))
            alpha = jnp.exp(m_prev - m_new)
            p = jnp.exp(s - m_new)
            l_scr[h:h + 1, :] = alpha * l_scr[h:h + 1, :] + jnp.sum(p, axis=0, keepdims=True)
            acc_scr[h] = alpha * acc_scr[h] + _dot(vt, p.astype(BF16))
            m_scr[h:h + 1, :] = m_new

    def body(j, carry):
        kv_tile(j, False)
        return carry

    lax.fori_loop(0, qi, body, 0)
    kv_tile(qi, True)
    for h in range(MLA_HEADS):
        sl = slice(h * MLA_V, (h + 1) * MLA_V)
        ctx_t = (acc_scr[h] / l_scr[h:h + 1, :]).astype(BF16)
        out_ref[0, :, sl] = _dot_tn(ctx_t, wuv_ref[:, sl]) * _silu(zc_ref[0, :, sl])


def _mla_attn_prompt(qt, k, latt, z, w_uv, layer, tile=256):
    b, hds, _, l = qt.shape
    kern = functools.partial(_mla_attn_kernel, tile=tile)
    return pl.pallas_call(
        kern,
        grid=(b, l // tile),
        in_specs=[pl.BlockSpec((1, hds, MLA_QK, tile), lambda bi, qi: (bi, 0, 0, qi)),
                  pl.BlockSpec((1, hds, l, MLA_QK), lambda bi, qi: (bi, 0, 0, 0)),
                  pl.BlockSpec((1, KV_RANK, l), lambda bi, qi: (bi, 0, 0)),
                  pl.BlockSpec((1, tile, GROUP_W), lambda bi, qi: (bi, qi, 2)),
                  _layer_spec(w_uv, layer)],
        out_specs=pl.BlockSpec((1, tile, GROUP_W), lambda bi, qi: (bi, qi, 0)),
        out_shape=jax.ShapeDtypeStruct((b, l, GROUP_W), F32),
        scratch_shapes=[pltpu.VMEM((SUBLANES, tile), F32), pltpu.VMEM((SUBLANES, tile), F32),
                        pltpu.VMEM((hds, KV_RANK, tile), F32)],
        compiler_params=_cparams(("parallel", "arbitrary")),
        name="mla_attn",
    )(qt, k, latt, z, w_uv)


def _memkv_kernel(mem_ref, g_ref, wk_ref, wv_ref, xkg_ref, mk_ref, mv_ref):
    mem_len = mem_ref.shape[1]
    mem_n = _rms(mem_ref[0], g_ref[0]).astype(BF16)
    kraw = _dot(mem_n, wk_ref[0])
    vraw = _dot(mem_n, wv_ref[0])
    for h in range(X_HEADS):
        sl = slice(h * X_HEAD_DIM, (h + 1) * X_HEAD_DIM)
        rows = pl.ds(h, mem_len, stride=X_HEADS)
        mk_ref[0, 0, rows, :] = _rms(kraw[:, sl], xkg_ref[0])
        mv_ref[0, 0, rows, :] = vraw[:, sl]


def _memkv(mem, mem_norm_g, w_mem_k, w_mem_v, xk_norm_g):
    b, m, _ = mem.shape
    depth = w_mem_k.shape[0]
    lsel = lambda li, bi: (li, 0, 0)
    return pl.pallas_call(
        _memkv_kernel,
        grid=(depth, b),
        in_specs=[pl.BlockSpec((1, m, D_MODEL), lambda li, bi: (bi, 0, 0)),
                  pl.BlockSpec((1, 1, D_MODEL), lsel),
                  pl.BlockSpec((1, D_MODEL, GROUP_W), lsel),
                  pl.BlockSpec((1, D_MODEL, GROUP_W), lsel),
                  pl.BlockSpec((1, 1, X_HEAD_DIM), lsel)],
        out_specs=[pl.BlockSpec((1, 1, m * X_HEADS, X_HEAD_DIM), lambda li, bi: (li, bi, 0, 0)),
                   pl.BlockSpec((1, 1, m * X_HEADS, X_HEAD_DIM), lambda li, bi: (li, bi, 0, 0))],
        out_shape=[jax.ShapeDtypeStruct((depth, b, m * X_HEADS, X_HEAD_DIM), F32),
                   jax.ShapeDtypeStruct((depth, b, m * X_HEADS, X_HEAD_DIM), F32)],
        compiler_params=_cparams(("parallel", "parallel")),
        name="memkv",
    )(mem, mem_norm_g, w_mem_k, w_mem_v, xk_norm_g)


def _xattn_kernel(xq_ref, mk_ref, mv_ref, zd_ref, g_ref, out_ref, *, rows, mem_len):
    xq = xq_ref[0]
    zd = zd_ref[0]
    if rows < SUBLANES:
        xq = jnp.broadcast_to(xq[0:1], (SUBLANES, GROUP_W))
    for h in range(X_HEADS):
        sl = slice(h * X_HEAD_DIM, (h + 1) * X_HEAD_DIM)
        mk_h = mk_ref[0, 0, pl.ds(h, mem_len, stride=X_HEADS), :]
        mv_h = mv_ref[0, 0, pl.ds(h, mem_len, stride=X_HEADS), :]
        qn = _rms(xq[:, sl], g_ref[...]).astype(BF16)
        s = _dot_nt(qn, mk_h.astype(BF16)) * (X_HEAD_DIM ** -0.5)
        s = s - jnp.max(s, axis=1, keepdims=True)
        e = jnp.exp(s)
        p = e / jnp.sum(e, axis=1, keepdims=True)
        y = _dot(p.astype(BF16), mv_h.astype(BF16))
        out_ref[0, :, sl] = y[0:rows] * _silu(zd[:, sl])


def _xattn(xq, mk, mv, layer, z, g, tm):
    b, l, _ = xq.shape
    mem_len = mk.shape[2] // X_HEADS
    kern = functools.partial(_xattn_kernel, rows=tm, mem_len=mem_len)
    mem_spec = pl.BlockSpec((1, 1) + mk.shape[2:], lambda bi, ti: (layer, bi, 0, 0))
    return pl.pallas_call(
        kern,
        grid=(b, l // tm),
        in_specs=[pl.BlockSpec((1, tm, GROUP_W), lambda bi, ti: (bi, ti, 0)),
                  mem_spec, mem_spec,
                  pl.BlockSpec((1, tm, GROUP_W), lambda bi, ti: (bi, ti, 3)),
                  _layer_spec(g, layer)],
        out_specs=pl.BlockSpec((1, tm, GROUP_W), lambda bi, ti: (bi, ti, 0)),
        out_shape=jax.ShapeDtypeStruct((b, l, GROUP_W), F32),
        compiler_params=_cparams(("parallel", "parallel")),
        name="xattn",
    )(xq, mk, mv, z, g)


def _outproj_kernel(ya_ref, yb_ref, yc_ref, yd_ref, x_ref, w_ref, fg_ref, out_ref, *, final):
    acc = x_ref[...]
    for i, ref in enumerate((ya_ref, yb_ref, yc_ref, yd_ref)):
        acc = acc + _dot(ref[...].astype(BF16), w_ref[i * GROUP_W:(i + 1) * GROUP_W, :])
    if final:
        acc = _rms(acc, fg_ref[...])
    out_ref[...] = acc


def _outproj(ys, x2d, w_out, layer, final_g, tm, final):
    n = x2d.shape[0]
    kern = functools.partial(_outproj_kernel, final=final)
    yspec = pl.BlockSpec((tm, GROUP_W), lambda i: (i, 0))
    return pl.pallas_call(
        kern,
        grid=(n // tm,),
        in_specs=[yspec, yspec, yspec, yspec,
                  pl.BlockSpec((tm, D_MODEL), lambda i: (i, 0)),
                  _layer_spec(w_out, layer), _const_spec((1, D_MODEL))],
        out_specs=pl.BlockSpec((tm, D_MODEL), lambda i: (i, 0)),
        out_shape=jax.ShapeDtypeStruct((n, D_MODEL), F32),
        compiler_params=_cparams(("parallel",)),
        name="outproj",
    )(*ys, x2d, w_out, final_g)


def _col_form(row, n):
    return jnp.broadcast_to(row, (LANES, n)).T


N_DEC_STATES = 6


def _dec_state_kernel(*refs):
    xbc_ref, xm_ref, o_ref, z_ref, misc_ref = refs[0:5]
    hs_ref, sbuf_ref, cm_ref, nm_ref, mm_ref, mbuf_ref = (r.at[0] for r in refs[5:5 + N_DEC_STATES])
    (cwa_ref, cba_ref, dtb_ref, alog_ref, hexp_ref, dskip_ref, nga_ref,
     cwb_ref, cbb_ref, wq_ref, wk_ref, wv_ref, gb_ref, ngb_ref) = refs[11:25]
    ya_ref, yb_ref = refs[-8:-6]
    hs_out, sbuf_out, cm_out, nm_out, mm_out, mbuf_out = (r.at[0] for r in refs[-N_DEC_STATES:])
    misc = misc_ref[0]
    z = z_ref[0]

    u = xbc_ref[0]
    buf = sbuf_ref[0]
    cw = cwa_ref[...]
    conv = cba_ref[...] + cw[3:4] * u
    for j in range(CONV_W - 1):
        conv = conv + cw[j:j + 1] * buf[j:j + 1]
    sbuf_out[0, 0:CONV_W - 2, :] = buf[1:CONV_W - 1]
    sbuf_out[0, CONV_W - 2:CONV_W - 1, :] = u
    xbcs = _silu(conv)
    xa = xbcs[:, 0:GROUP_W]
    dt = _softplus(misc + dtb_ref[...])
    a = -jnp.exp(alog_ref[...])
    hexp = hexp_ref[...]
    dt8 = jnp.broadcast_to(dt, (SUBLANES, LANES))
    da8 = jnp.broadcast_to(dt * a, (SUBLANES, LANES))
    dt_x = jnp.dot(dt8, hexp, precision=HIGHEST, preferred_element_type=F32)[0:1]
    decay_x = jnp.exp(jnp.dot(da8, hexp, precision=HIGHEST, preferred_element_type=F32)[0:1])
    hstate = hs_ref[0].reshape(SSD_HEADS * SSD_HEAD_DIM, SSD_STATE)
    gw = GROUP_W // SSD_GROUPS
    c_off = GROUP_W + SSD_GROUPS * SSD_STATE
    dtx_col = _col_form(dt_x * xa, GROUP_W)
    decay_col = _col_form(decay_x, GROUP_W)
    y_parts = []
    for g in range(SSD_GROUPS):
        b_g = xbcs[:, GROUP_W + g * SSD_STATE:GROUP_W + (g + 1) * SSD_STATE]
        c_g = xbcs[:, c_off + g * SSD_STATE:c_off + (g + 1) * SSD_STATE]
        cb = jnp.sum(c_g * b_g, axis=1, keepdims=True)
        rows = slice(g * gw, (g + 1) * gw)
        h_g = hstate[rows]
        c8 = jnp.broadcast_to(c_g, (SUBLANES, SSD_STATE)).astype(BF16)
        ch = _dot_nt(c8, h_g.astype(BF16))[0:1]
        y_parts.append(cb * dt_x[:, rows] * xa[:, rows] + ch * decay_x[:, rows])
        h_new = decay_col[rows] * h_g + dtx_col[rows] * b_g
        hs_out[0, g * SSD_HPG:(g + 1) * SSD_HPG] = h_new.reshape(SSD_HPG, SSD_HEAD_DIM, SSD_STATE)
    y = jnp.concatenate(y_parts, axis=1) + dskip_ref[...] * xa
    y = y * _silu(z[:, 0:GROUP_W])
    nga = nga_ref[...]
    for g in range(SSD_GROUPS):
        rows = slice(g * gw, (g + 1) * gw)
        ya_ref[0, :, rows] = _rms(y[:, rows], nga[:, rows])

    um = xm_ref[0]
    mbuf = mbuf_ref[0]
    cwb = cwb_ref[...]
    convb = cbb_ref[...] + cwb[3:4] * um
    for j in range(CONV_W - 1):
        convb = convb + cwb[j:j + 1] * mbuf[j:j + 1]
    mbuf_out[0, 0:CONV_W - 2, :] = mbuf[1:CONV_W - 1]
    mbuf_out[0, CONV_W - 2:CONV_W - 1, :] = um
    xmc = _silu(convb)
    gates = misc + gb_ref[...]
    lf_all = -_softplus(-gates)
    mm = mm_ref[0]
    ngb = ngb_ref[...]
    o_all = o_ref[0]
    for h in range(ML_HEADS):
        sl = slice(h * ML_HEAD_DIM, (h + 1) * ML_HEAD_DIM)
        li = gates[:, MISC_I + h:MISC_I + h + 1]
        lf = lf_all[:, MISC_F + h:MISC_F + h + 1]
        m_prev = mm[:, h:h + 1]
        l_prev = lf + m_prev
        m_t = jnp.maximum(l_prev, li)
        xmc8 = jnp.broadcast_to(xmc[:, sl], (SUBLANES, ML_HEAD_DIM)).astype(BF16)
        um8 = jnp.broadcast_to(um[:, sl], (SUBLANES, ML_HEAD_DIM)).astype(BF16)
        q8 = _dot(xmc8, wq_ref[h])
        q = q8[0:1]
        k = _dot(xmc8, wk_ref[h])[0:1] * (ML_HEAD_DIM ** -0.5)
        v = _dot(um8, wv_ref[h])[0:1]
        qk = jnp.sum(q * k, axis=1, keepdims=True)
        sc = qk * jnp.exp(li - m_t)
        g_prev = jnp.exp(l_prev - m_t)
        cs = cm_ref[0, h]
        ns = nm_ref[0, h:h + 1, :]
        cq = _dot_nt(q8.astype(BF16), cs.astype(BF16))[0:1]
        num = sc * v + g_prev * cq
        den = sc + g_prev * jnp.sum(ns * q, axis=1, keepdims=True)
        hout = num / jnp.maximum(jnp.abs(den), jnp.exp(-m_t))
        wend = jnp.exp(li - m_t)
        decay = jnp.exp(l_prev - m_t)
        cm_out[0, h] = decay * cs + _col_form(wend * v, ML_HEAD_DIM) * k
        nm_out[0, h:h + 1, :] = decay * ns + wend * k
        mm_out[0, :, h:h + 1] = m_t
        hb = jax.nn.sigmoid(o_all[:, sl]) * hout
        yb_ref[0, :, sl] = _rms(hb, ngb[:, sl]) * _silu(z[:, GROUP_W + h * ML_HEAD_DIM:GROUP_W + (h + 1) * ML_HEAD_DIM])


def _dec_state(xbc, xm, o, z, misc, states, layer, lp):
    b = xbc.shape[0]
    tok = lambda w: pl.BlockSpec((1, 1, w), lambda bi: (bi, 0, 0))
    at_layer = lambda t: pl.BlockSpec((1, 1) + t.shape[2:], lambda bi: (layer, bi) + (0,) * (t.ndim - 2))
    state_specs = [at_layer(t) for t in states]
    pa, pa_specs = _layer_args(lp, layer, ("conv_a_w", "conv_a_b", "dtb_row", "alog_row"))
    pb, pb_specs = _layer_args(lp, layer, ("dskip_row", "ssd_norm_g", "conv_b_w", "conv_b_b", "w_mq", "w_mk", "w_mv",
                                           "gb_row", "ml_norm_g"))
    params = pa + [lp["head_expand"]] + pb
    param_specs = pa_specs + [_const_spec((LANES, GROUP_W))] + pb_specs
    inputs = [xbc, xm, o, z, misc] + list(states) + params
    in_specs = [tok(SSD_CONV_CH), tok(GROUP_W), tok(GROUP_W), tok(D_MIX), tok(MISC_W)] + state_specs + param_specs
    aliases = {5 + i: 2 + i for i in range(N_DEC_STATES)}
    outs = pl.pallas_call(
        _dec_state_kernel,
        grid=(b,),
        in_specs=in_specs,
        out_specs=[tok(GROUP_W), tok(GROUP_W)] + state_specs,
        out_shape=[jax.ShapeDtypeStruct((b, 1, GROUP_W), F32)] * 2
        + [jax.ShapeDtypeStruct(t.shape, F32) for t in states],
        input_output_aliases=aliases,
        compiler_params=_cparams(("parallel",)),
        name="dec_state",
    )(*inputs)
    return outs[0], outs[1], tuple(outs[2:])


def _mla_decode_kernel(pt_ref, lat_hbm, kr_hbm, qn_ref, qr_ref, qfull_ref, kfull_ref, vnew_ref, zc_ref,
                       wukt_ref, wuv_ref, kng_ref, out_ref,
                       lat_buf, kr_buf, sem, qa_scr, m_scr, l_scr, acc_scr, *, layer, pps, page, nsub):
    b = pl.program_id(0)
    s = pl.program_id(1)
    ns = pl.num_programs(1)
    nb = pl.num_programs(0)
    step = b * ns + s
    last = nb * ns - 1
    slot = step % 2
    sub = pps * page // nsub

    def copies(bb, ss, sl):
        out = []
        for p in range(pps):
            pg = pt_ref[bb, ss * pps + p]
            out.append(pltpu.make_async_copy(lat_hbm.at[layer, pg], lat_buf.at[sl, pl.ds(p * page, page), :],
                                             sem.at[0, sl]))
            out.append(pltpu.make_async_copy(kr_hbm.at[layer, pg], kr_buf.at[sl, :, pl.ds(p * page, page)],
                                             sem.at[1, sl]))
        return out

    @pl.when(step == 0)
    def _():
        for cp in copies(b, s, slot):
            cp.start()

    scale = MLA_QK ** -0.5

    @pl.when(s == 0)
    def _():
        qng = (qn_ref[0].astype(F32) * kng_ref[...]).astype(BF16)
        for h in range(MLA_HEADS):
            full = _dot(qng, wukt_ref[h * MLA_NOPE:(h + 1) * MLA_NOPE, :])
            qa_scr[h:h + 1, :] = full[h:h + 1]
            qa_scr[MLA_HEADS + h:MLA_HEADS + h + 1, :] = full[h:h + 1]
        m_scr[...] = jnp.full_like(m_scr, -jnp.inf)
        l_scr[...] = jnp.zeros_like(l_scr)
        acc_scr[...] = jnp.zeros_like(acc_scr)

    for cp in copies(b, s, slot):
        cp.wait()

    nxt = jnp.minimum(step + 1, last)
    for cp in copies(nxt // ns, nxt % ns, 1 - slot):
        cp.start()

    qa = qa_scr[...].astype(BF16)
    qr = qr_ref[0]
    wukt = wukt_ref[...]

    def scores(i):
        latb = lat_buf[slot, i * sub:(i + 1) * sub, :].astype(BF16)
        krb = kr_buf[slot, :, i * sub:(i + 1) * sub].astype(BF16)
        kraw_t = _dot_nt(wukt, latb)
        sa = _dot_nt(qa, latb)
        sr = _dot(qr, krb)
        rinv_rows = []
        for h in range(MLA_HEADS):
            kh = kraw_t[h * MLA_NOPE:(h + 1) * MLA_NOPE, :]
            ssq = jnp.sum(kh * kh, axis=0, keepdims=True)
            rinv_rows.append(lax.rsqrt(ssq * (1.0 / MLA_NOPE) + EPS))
        rinv = jnp.concatenate(rinv_rows + rinv_rows, axis=0)
        return latb, (sa * rinv + sr) * scale

    m = m_scr[...]
    l = l_scr[...]
    acc = acc_scr[...]
    pending = scores(0)
    for i in range(nsub):
        latb, sc = pending
        if i + 1 < nsub:
            pending = scores(i + 1)
        m_new = jnp.maximum(m, jnp.max(sc, axis=1, keepdims=True))
        alpha = jnp.exp(m - m_new)
        p = jnp.exp(sc - m_new)
        l = alpha * l + jnp.sum(p, axis=1, keepdims=True)
        acc = alpha * acc + _dot(p.astype(BF16), latb)
        m = m_new
    m_scr[...] = m
    l_scr[...] = l
    acc_scr[...] = acc

    @pl.when(s == ns - 1)
    def _():
        s_new = jnp.sum(qfull_ref[0].astype(F32) * kfull_ref[0].astype(F32), axis=1, keepdims=True) * scale
        m_fin = jnp.maximum(m, s_new)
        alpha2 = jnp.exp(m - m_fin)
        p_new = jnp.exp(s_new - m_fin)
        l_fin = alpha2 * l + p_new
        acc_fin = alpha2 * acc + p_new * vnew_ref[0].astype(F32)
        ctx = (acc_fin / l_fin).astype(BF16)
        yfull = _dot(ctx, wuv_ref[...])
        parts = [yfull[h:h + 1, h * MLA_V:(h + 1) * MLA_V] for h in range(MLA_HEADS)]
        out_ref[0] = jnp.concatenate(parts, axis=1) * _silu(zc_ref[0])

    @pl.when(step == last)
    def _():
        for cp in copies(b, s, 1 - slot):
            cp.wait()


DECODE_PAGES_PER_STEP = 32
DECODE_SUB_BLOCKS = 2


def _mla_decode(page_table, cache_lat, cache_kr, layer, qn8, qr8, qfull8, kfull8, vnew, z, lp,
                pps=DECODE_PAGES_PER_STEP, nsub=DECODE_SUB_BLOCKS):
    b, n_pages = page_table.shape
    page = cache_lat.shape[2]
    ns = n_pages // pps
    t = pps * page
    kern = functools.partial(_mla_decode_kernel, layer=layer, pps=pps, page=page, nsub=nsub)
    per_b = lambda shape: pl.BlockSpec((1,) + shape, lambda bi, si, pt: (bi, 0, 0))
    params, param_specs = _layer_args(lp, layer, ("w_uk_t", "w_uv", "kn_nope_g"))
    grid_spec = pltpu.PrefetchScalarGridSpec(
        num_scalar_prefetch=1,
        grid=(b, ns),
        in_specs=[pl.BlockSpec(memory_space=pl.ANY), pl.BlockSpec(memory_space=pl.ANY),
                  per_b((SUBLANES, MLA_NOPE)), per_b((SUBLANES, MLA_ROPE)),
                  per_b((SUBLANES, MLA_QK)), per_b((SUBLANES, MLA_QK)), per_b((1, KV_RANK)),
                  pl.BlockSpec((1, 1, GROUP_W), lambda bi, si, pt: (bi, 0, 2)),
                  ] + param_specs,
        out_specs=pl.BlockSpec((1, 1, GROUP_W), lambda bi, si, pt: (bi, 0, 0)),
        scratch_shapes=[pltpu.VMEM((2, t, KV_RANK), F32), pltpu.VMEM((2, MLA_ROPE, t), F32),
                        pltpu.SemaphoreType.DMA((2, 2)),
                        pltpu.VMEM((SUBLANES, KV_RANK), F32),
                        pltpu.VMEM((SUBLANES, 1), F32), pltpu.VMEM((SUBLANES, 1), F32),
                        pltpu.VMEM((SUBLANES, KV_RANK), F32)],
    )
    return pl.pallas_call(
        kern,
        grid_spec=grid_spec,
        out_shape=jax.ShapeDtypeStruct((b, 1, GROUP_W), F32),
        compiler_params=_cparams(("arbitrary", "arbitrary")),
        name="mla_decode",
    )(page_table, cache_lat, cache_kr, qn8, qr8, qfull8, kfull8, vnew, z, *params)


def _pack_w_in(w_in):
    names = ("z", "xbc", "dt", "xm", "i", "f", "o", "cq", "ckv", "kr", "xq")
    offs = {}
    o = 0
    for nme, w in zip(names, IN_SIZES):
        offs[nme] = (o, o + w)
        o += w
    col = lambda nme: w_in[:, :, offs[nme][0]:offs[nme][1]].astype(BF16)
    pad = jnp.zeros(w_in.shape[:2] + (MISC_W - (MISC_F + ML_HEADS),), BF16)
    misc = jnp.concatenate([col("kr"), col("dt"), col("i"), col("f"), pad], axis=-1)
    return [misc if nme == "misc" else col(nme) for nme, _ in PACK_SEGS]


def _stack_params(p):
    depth = p["norm_g"].shape[0]
    row = lambda a: a[:, None, :]
    lp = {}
    for nme in ("norm_g", "conv_a_b", "ssd_norm_g", "conv_b_b", "ml_norm_g", "q_norm_g", "kv_norm_g", "qn_nope_g",
                "kn_nope_g", "xq_norm_g"):
        lp[nme] = row(p[nme])
    lp["conv_a_w"] = p["conv_a_w"]
    lp["conv_b_w"] = p["conv_b_w"]
    lp["dtb_row"] = _lane_pad(p["dt_bias"], MISC_DT)
    lp["alog_row"] = _lane_pad(p["a_log"], MISC_DT)
    lp["dtb_col"] = p["dt_bias"][:, :, None]
    lp["alog_col"] = p["a_log"][:, :, None]
    lp["dskip_row"] = row(jnp.repeat(p["d_skip"], SSD_HEAD_DIM, axis=1))
    lp["w_mq"] = p["w_mq"].astype(BF16)
    lp["w_mk"] = p["w_mk"].astype(BF16)
    lp["w_mv"] = p["w_mv"].astype(BF16)
    lp["w_mq_t"] = jnp.swapaxes(lp["w_mq"], 2, 3)
    lp["w_mv_t"] = jnp.swapaxes(lp["w_mv"], 2, 3)
    gate_b = jnp.concatenate([p["b_i"], p["b_f"]], axis=1)
    lp["gb_row"] = _lane_pad(gate_b, MISC_I)
    lp["gb_col"] = gate_b[:, :, None]
    w_uq = p["w_uq"]
    lp["w_uq_nope"] = w_uq[..., :MLA_NOPE].reshape(depth, Q_RANK, MLA_HEADS * MLA_NOPE).astype(BF16)
    rope_pad = lambda t: jnp.pad(t, [(0, 0)] * (t.ndim - 1) + [(0, LANES - MLA_ROPE)])
    lp["w_uq_rope"] = rope_pad(jnp.transpose(w_uq[..., MLA_NOPE:], (0, 2, 1, 3))).astype(BF16)
    w_uk = p["w_uk"].reshape(depth, KV_RANK, MLA_HEADS * MLA_NOPE)
    lp["w_uk"] = w_uk.astype(BF16)
    lp["w_uk_t"] = jnp.swapaxes(w_uk, 1, 2).astype(BF16)
    lp["w_uv"] = p["w_uv"].reshape(depth, KV_RANK, MLA_HEADS * MLA_V).astype(BF16)
    lp["qn_rope_g_pad"] = rope_pad(row(p["qn_rope_g"]))
    lp["kn_rope_g_pad"] = rope_pad(row(p["kn_rope_g"]))
    lp["w_out"] = p["w_out"].astype(BF16)
    head_of_lane = jnp.arange(LANES)[:, None] - MISC_DT
    head_of_chan = jnp.arange(GROUP_W)[None, :] // SSD_HEAD_DIM
    lp["head_expand"] = (head_of_lane == head_of_chan).astype(F32)
    return lp


def _rope_tables(pos):
    half = MLA_ROPE // 2
    inv = ROPE_BASE ** (-jnp.arange(half, dtype=F32) * 2.0 / MLA_ROPE)
    ang = pos.astype(F32)[:, None] * inv[None, :]
    cos = jnp.cos(ang)
    sin = jnp.sin(ang)
    zero = jnp.zeros((pos.shape[0], LANES - MLA_ROPE), F32)
    return jnp.concatenate([cos, cos, zero], axis=1), jnp.concatenate([-sin, sin, zero], axis=1)


def _prompt_layer(lp, w_slabs, layer, x, mk_all, mv_all, rope_tabs, final_g, final):
    b, l, _ = x.shape
    n = b * l
    x2d = x.reshape(n, D_MODEL)
    z, xbc, xm, o, cq, ckv, xq, misc = _inproj(x2d, lp["norm_g"], w_slabs, layer, tm=512)
    r3 = lambda t: t.reshape(b, l, t.shape[-1])
    z, xbc, xm, o, cq, ckv, xq, misc = map(r3, (z, xbc, xm, o, cq, ckv, xq, misc))
    ya, ssm_h, ssm_buf = _ssd_prompt(xbc, misc, z, lp, layer)
    yb, ml_c, ml_n, ml_m, ml_buf = _mlstm_prompt(xm, o, z, misc, lp, layer)
    qt, k, latt, lat, kr = _mla_prep(cq, ckv, misc, rope_tabs[0], rope_tabs[1], lp, layer, tm=256,
                                     transposed=True)
    yc = _mla_attn_prompt(qt, k, latt, z, lp["w_uv"], layer)
    yd = _xattn(xq, mk_all, mv_all, layer, z, lp["xq_norm_g"], tm=512)
    ys = [t.reshape(n, GROUP_W) for t in (ya, yb, yc, yd)]
    x_new = _outproj(ys, x2d, lp["w_out"], layer, final_g, tm=512, final=final).reshape(b, l, D_MODEL)
    return x_new, (lat, kr, ssm_h, ssm_buf, ml_c, ml_n, ml_m, ml_buf)


def _sample_layer(lp, w_slabs, layer, x, mem_k, mem_v, cache_lat, cache_kr, page_table, states,
                  rope_tabs, final_g, final, pps=DECODE_PAGES_PER_STEP):
    b = x.shape[0]
    x2d = x.reshape(b, D_MODEL)
    z, xbc, xm, o, cq, ckv, xq, misc = _inproj(x2d, lp["norm_g"], w_slabs, layer, tm=b)
    tok = lambda t: t.reshape(b, 1, t.shape[-1])
    ya, yb, st_new = _dec_state(tok(xbc), tok(xm), tok(o), tok(z), tok(misc), states, layer, lp)
    one = lambda t: t.reshape(1, b, t.shape[-1])
    q, k, latb, lat, kr = _mla_prep(one(cq), one(ckv), one(misc), rope_tabs[0], rope_tabs[1], lp, layer, tm=b,
                                    transposed=False)
    qh = jnp.transpose(q[0], (1, 0, 2))
    kh = jnp.transpose(k[0], (1, 0, 2))
    dup = lambda t: jnp.concatenate([t, t], axis=1)
    qfull8 = dup(qh)
    kfull8 = dup(kh)
    qn8 = qfull8[:, :, :MLA_NOPE]
    qr8 = qfull8[:, :, MLA_NOPE:]
    vnew = latb.reshape(b, 1, KV_RANK)
    yc = _mla_decode(page_table, cache_lat, cache_kr, layer, qn8, qr8, qfull8, kfull8, vnew, tok(z), lp, pps=pps)
    yd = _xattn(tok(xq), mem_k, mem_v, layer, tok(z), lp["xq_norm_g"], tm=1)
    ys = [t.reshape(b, GROUP_W) for t in (ya, yb, yc, yd)]
    x_new = _outproj(ys, x2d, lp["w_out"], layer, final_g, tm=b, final=final).reshape(b, 1, D_MODEL)
    return x_new, lat.reshape(b, 1, KV_RANK), kr.reshape(b, 1, MLA_ROPE), st_new


def kernel(x_prompt, x_sample, mem_prompt, cache_mla_latent, cache_mla_krope, cache_mem_k, cache_mem_v, state_ssm, state_ssm_conv, state_mlstm_C, state_mlstm_n, state_mlstm_m, state_mlstm_conv, page_table, norm_g, w_in, conv_a_w, conv_a_b, dt_bias, a_log, d_skip, ssd_norm_g, conv_b_w, conv_b_b, w_mq, w_mk, w_mv, b_i, b_f, ml_norm_g, q_norm_g, kv_norm_g, w_uq, w_uk, w_uv, qn_nope_g, qn_rope_g, kn_nope_g, kn_rope_g, mem_norm_g, w_mem_k, w_mem_v, xk_norm_g, xq_norm_g, w_out, final_norm_g):
    params = dict(norm_g=norm_g, conv_a_w=conv_a_w, conv_a_b=conv_a_b, dt_bias=dt_bias, a_log=a_log, d_skip=d_skip,
                  ssd_norm_g=ssd_norm_g, conv_b_w=conv_b_w, conv_b_b=conv_b_b, w_mq=w_mq, w_mk=w_mk, w_mv=w_mv,
                  b_i=b_i, b_f=b_f, ml_norm_g=ml_norm_g, q_norm_g=q_norm_g, kv_norm_g=kv_norm_g, w_uq=w_uq,
                  w_uk=w_uk, w_uv=w_uv, qn_nope_g=qn_nope_g, qn_rope_g=qn_rope_g, kn_nope_g=kn_nope_g,
                  kn_rope_g=kn_rope_g, xq_norm_g=xq_norm_g, w_out=w_out)
    depth = w_in.shape[0]
    bp, lp_len, _ = x_prompt.shape
    bs, ls_len, _ = x_sample.shape
    n_pages = page_table.shape[1]
    past_len = n_pages * cache_mla_latent.shape[2]
    mem_len = mem_prompt.shape[1]
    w_slabs = _pack_w_in(w_in)
    final_g = final_norm_g[None, :]

    cache_kr_t = jnp.swapaxes(cache_mla_krope, 2, 3)
    rope_p = _rope_tables(jnp.arange(lp_len, dtype=jnp.int32))
    rope_s = _rope_tables(jnp.full((bs,), past_len, dtype=jnp.int32))

    mk_all, mv_all = _memkv(mem_prompt, mem_norm_g[:, None, :], w_mem_k.astype(BF16), w_mem_v.astype(BF16),
                            xk_norm_g[:, None, :])

    mem_rows = cache_mem_k.shape[2] * X_HEADS
    cache_mk = cache_mem_k.reshape(depth, bs, mem_rows, X_HEAD_DIM)
    cache_mv = cache_mem_v.reshape(depth, bs, mem_rows, X_HEAD_DIM)
    states = (state_ssm, state_ssm_conv, state_mlstm_C, state_mlstm_n,
              state_mlstm_m.reshape(depth, bs, 1, ML_HEADS), state_mlstm_conv)

    hp, hs = x_prompt, x_sample
    p_states, s_lats, s_krs = [], [], []
    lp = _stack_params(params)
    for l in range(depth):
        final = l == depth - 1
        hp, st_p = _prompt_layer(lp, w_slabs, l, hp, mk_all, mv_all, rope_p, final_g, final)
        p_states.append(st_p)
        hs, s_lat_l, s_kr_l, states = _sample_layer(lp, w_slabs, l, hs, cache_mk, cache_mv, cache_mla_latent,
                                                    cache_kr_t, page_table, states, rope_s, final_g, final)
        s_lats.append(s_lat_l)
        s_krs.append(s_kr_l)

    p_out = [jnp.stack([st[i] for st in p_states]) for i in range(8)]
    p_mem_k = mk_all.reshape(depth, bp, mem_len, X_HEADS, X_HEAD_DIM)
    p_mem_v = mv_all.reshape(depth, bp, mem_len, X_HEADS, X_HEAD_DIM)
    (p_lat, p_kr, p_ssm, p_sconv, p_mc, p_mn, p_mm, p_mconv) = p_out
    s_ssm, s_sconv, s_mc, s_mn, s_mm, s_mconv = states
    return (hp, hs, p_lat, p_kr, p_mem_k, p_mem_v, p_ssm, p_sconv, p_mc, p_mn, p_mm, p_mconv,
            jnp.stack(s_lats), jnp.stack(s_krs), s_ssm, s_sconv, s_mc, s_mn, s_mm.reshape(depth, bs, ML_HEADS),
            s_mconv)
```

```python
import functools
import math

import jax
import jax.numpy as jnp
from jax import lax
from jax.experimental import pallas as pl
from jax.experimental.pallas import tpu as pltpu

F32 = jnp.float32
BF16 = jnp.bfloat16
HIGHEST = lax.Precision.HIGHEST

D_MODEL = 1024
D_MIX = 2 * D_MODEL
GROUP_W = D_MIX // 4
SSD_HEAD_DIM = 64
SSD_HEADS = GROUP_W // SSD_HEAD_DIM
SSD_GROUPS = 2
SSD_HPG = SSD_HEADS // SSD_GROUPS
SSD_STATE = 128
CONV_W = 4
SSD_CONV_CH = GROUP_W + 2 * SSD_GROUPS * SSD_STATE
ML_HEADS = 4
ML_HEAD_DIM = GROUP_W // ML_HEADS
MLA_HEADS = 4
MLA_NOPE = 128
MLA_ROPE = 64
MLA_QK = MLA_NOPE + MLA_ROPE
MLA_V = GROUP_W // MLA_HEADS
Q_RANK = 3 * D_MODEL // 8
KV_RANK = D_MODEL // 4
ROPE_BASE = 10000.0
X_HEADS = 4
X_HEAD_DIM = GROUP_W // X_HEADS
CHUNK = 256
EPS = 1e-6
IN_SIZES = (D_MIX, SSD_CONV_CH, SSD_HEADS, GROUP_W, ML_HEADS, ML_HEADS, GROUP_W, Q_RANK, KV_RANK, MLA_ROPE, GROUP_W)

LANES = 128
SUBLANES = 8
VMEM_LIMIT = 56 * 1024 * 1024

MISC_W = LANES
MISC_KR = 0
MISC_DT = MLA_ROPE
MISC_I = MISC_DT + SSD_HEADS
MISC_F = MISC_I + ML_HEADS
PACK_SEGS = (("z", D_MIX), ("xbc", SSD_CONV_CH), ("xm", GROUP_W), ("o", GROUP_W), ("cq", Q_RANK),
             ("ckv", KV_RANK), ("xq", GROUP_W), ("misc", MISC_W))
PACK_W = sum(w for _, w in PACK_SEGS)


def _cparams(sem):
    return pltpu.CompilerParams(dimension_semantics=sem, vmem_limit_bytes=VMEM_LIMIT)


def _dot(a, b):
    return jnp.dot(a, b, preferred_element_type=F32)


def _dot_nt(a, b, precision=None):
    return lax.dot_general(a, b, (((1,), (1,)), ((), ())), preferred_element_type=F32, precision=precision)


def _dot_tn(a, b):
    return lax.dot_general(a, b, (((0,), (0,)), ((), ())), preferred_element_type=F32)


def _rms(x, g):
    return x * lax.rsqrt(jnp.mean(x * x, axis=-1, keepdims=True) + EPS) * g


def _silu(x):
    return x * jax.nn.sigmoid(x)


def _softplus(x):
    return jnp.maximum(x, 0.0) + jnp.log1p(jnp.exp(-jnp.abs(x)))


def _const_spec(shape):
    nd = len(shape)
    return pl.BlockSpec(shape, lambda *_: (0,) * nd)


def _layer_spec(arr, layer):
    nd = arr.ndim
    return pl.BlockSpec((None,) + arr.shape[1:], lambda *_: (layer,) + (0,) * (nd - 1))


def _layer_args(lp, layer, names):
    return [lp[n] for n in names], [_layer_spec(lp[n], layer) for n in names]


def _inproj_kernel(x_ref, g_ref, *refs):
    nseg = len(PACK_SEGS)
    w_refs, out_refs = refs[:nseg], refs[nseg:]
    x = x_ref[...]
    u = _rms(x, g_ref[...]).astype(BF16)
    for w_ref, out_ref in zip(w_refs, out_refs):
        out_ref[...] = _dot(u, w_ref[0])


def _inproj(x2d, g, w_slabs, layer, tm):
    n = x2d.shape[0]
    out_shape = [jax.ShapeDtypeStruct((n, w), F32) for _, w in PACK_SEGS]
    out_specs = [pl.BlockSpec((tm, w), lambda i: (i, 0)) for _, w in PACK_SEGS]
    w_specs = [pl.BlockSpec((1, D_MODEL, w), lambda i: (layer, 0, 0), pipeline_mode=pl.Buffered(1))
               for _, w in PACK_SEGS]
    return pl.pallas_call(
        _inproj_kernel,
        grid=(n // tm,),
        in_specs=[pl.BlockSpec((tm, D_MODEL), lambda i: (i, 0)), _layer_spec(g, layer)] + w_specs,
        out_specs=out_specs,
        out_shape=out_shape,
        compiler_params=_cparams(("parallel",)),
        name="inproj",
    )(x2d, g, *w_slabs)


def _tri_masks():
    r = lax.broadcasted_iota(jnp.int32, (CHUNK, CHUNK), 0)
    c = lax.broadcasted_iota(jnp.int32, (CHUNK, CHUNK), 1)
    return r >= c, c >= r


def _ssd_chunk(xbc_ref, misc_ref, za_ref, cw_ref, cb_ref, dtb_row_ref, alog_row_ref, dtb_col_ref,
               alog_col_ref, dskip_ref, ng_ref, ya_ref, h_scr, ext_scr, y_scr):
    u = xbc_ref[0]
    ext_scr[SUBLANES:SUBLANES + CHUNK, :] = u
    cw = cw_ref[...]
    conv = cb_ref[...] + cw[3:4] * u
    for j in range(CONV_W - 1):
        back = CONV_W - 1 - j
        conv = conv + cw[j:j + 1] * ext_scr[SUBLANES - back:SUBLANES - back + CHUNK, :]
    ext_scr[0:SUBLANES, :] = u[CHUNK - SUBLANES:CHUNK, :]
    xbcs = _silu(conv)
    xa = xbcs[:, 0:GROUP_W]

    misc = misc_ref[0]
    misc_t = misc.T
    lower, upper = _tri_masks()
    lower_f = lower.astype(F32)
    upper_f = upper.astype(F32)
    dt_col = _softplus(misc + dtb_row_ref[...])
    dt_row = _softplus(misc_t[MISC_DT:MISC_DT + SSD_HEADS, :] + dtb_col_ref[...])
    a_row = -jnp.exp(alog_row_ref[...])
    a_col = -jnp.exp(alog_col_ref[...])
    acum_col = jnp.dot(lower_f, dt_col * a_row, precision=HIGHEST, preferred_element_type=F32)
    acum_row = jnp.dot(dt_row * a_col, upper_f, precision=HIGHEST, preferred_element_type=F32)

    for g in range(SSD_GROUPS):
        b_g = xbcs[:, GROUP_W + g * SSD_STATE:GROUP_W + (g + 1) * SSD_STATE].astype(BF16)
        c_off = GROUP_W + SSD_GROUPS * SSD_STATE
        c_g = xbcs[:, c_off + g * SSD_STATE:c_off + (g + 1) * SSD_STATE].astype(BF16)
        cb = _dot_nt(c_g, b_g)
        for hh in range(SSD_HPG):
            h = g * SSD_HPG + hh
            ac_col = acum_col[:, MISC_DT + h:MISC_DT + h + 1]
            ac_row = acum_row[h:h + 1, :]
            seg = jnp.where(lower, ac_col - ac_row, -jnp.inf)
            w = cb * jnp.exp(seg) * dt_row[h:h + 1, :]
            x_h = xa[:, h * SSD_HEAD_DIM:(h + 1) * SSD_HEAD_DIM]
            hs = h_scr[h]
            y_h = _dot(w.astype(BF16), x_h.astype(BF16)) + _dot_nt(c_g, hs.astype(BF16)) * jnp.exp(ac_col)
            y_scr[:, h * SSD_HEAD_DIM:(h + 1) * SSD_HEAD_DIM] = y_h
            ac_last = ac_col[CHUNK - 1:CHUNK, :]
            wend = jnp.exp(ac_last - ac_col) * dt_col[:, MISC_DT + h:MISC_DT + h + 1]
            xw = (x_h * wend).astype(BF16)
            h_scr[h] = jnp.exp(ac_last) * hs + _dot_tn(xw, b_g)

    y = y_scr[...] + dskip_ref[...] * xa
    y = y * _silu(za_ref[0])
    gw = GROUP_W // SSD_GROUPS
    ng = ng_ref[...]
    for g in range(SSD_GROUPS):
        ya_ref[0, :, g * gw:(g + 1) * gw] = _rms(y[:, g * gw:(g + 1) * gw], ng[:, g * gw:(g + 1) * gw])


def _lane_pad(v, offset):
    return jnp.pad(v, ((0, 0), (offset, LANES - offset - v.shape[1])))[:, None, :]


def _mlstm_chunk(xm_ref, o_ref, zb_ref, misc_ref, cw_ref, cb_ref, wqt_ref, wk_ref, wvt_ref, gb_row_ref,
                 gb_col_ref, ng_ref, yb_ref, c_scr, n_scr, m_scr, ext_scr):
    u = xm_ref[0]
    ext_scr[SUBLANES:SUBLANES + CHUNK, :] = u
    cw = cw_ref[...]
    conv = cb_ref[...] + cw[3:4] * u
    for j in range(CONV_W - 1):
        back = CONV_W - 1 - j
        conv = conv + cw[j:j + 1] * ext_scr[SUBLANES - back:SUBLANES - back + CHUNK, :]
    ext_scr[0:SUBLANES, :] = u[CHUNK - SUBLANES:CHUNK, :]
    xmc = _silu(conv)

    misc = misc_ref[0]
    misc_t = misc.T
    lower, upper = _tri_masks()
    gates_col = misc + gb_row_ref[...]
    gates_row = misc_t[MISC_I:MISC_I + 2 * ML_HEADS, :] + gb_col_ref[...]
    lf_col = -_softplus(-gates_col)
    lf_row = -_softplus(-gates_row)
    bcum_col = jnp.dot(lower.astype(F32), lf_col, precision=HIGHEST, preferred_element_type=F32)
    bcum_row = jnp.dot(lf_row, upper.astype(F32), precision=HIGHEST, preferred_element_type=F32)
    ng = ng_ref[...]
    o_all = o_ref[0]
    zb = zb_ref[0]
    m_all = m_scr[...]
    n_all = n_scr[...]
    m_rows, n_rows = [], []
    for h in range(ML_HEADS):
        sl = slice(h * ML_HEAD_DIM, (h + 1) * ML_HEAD_DIM)
        li_r = gates_row[h:h + 1, :]
        bc_r = bcum_row[ML_HEADS + h:ML_HEADS + h + 1, :]
        col = gates_col[:, MISC_I + h:MISC_I + h + 1] - bcum_col[:, MISC_F + h:MISC_F + h + 1]
        m_prev = m_all[h:h + 1, 0:1]
        lw = jnp.where(upper, bc_r + col, -jnp.inf)
        l_prev = bc_r + m_prev
        m_t = jnp.maximum(l_prev, jnp.max(lw, axis=0, keepdims=True))
        xmc_h = xmc[:, sl].astype(BF16)
        qtb = _dot_nt(wqt_ref[h], xmc_h).astype(BF16)
        k = _dot(xmc_h, wk_ref[h]) * (ML_HEAD_DIM ** -0.5)
        kb = k.astype(BF16)
        vt = _dot_nt(wvt_ref[h], u[:, sl].astype(BF16))
        sc = _dot(kb, qtb) * jnp.exp(lw - m_t)
        g_prev = jnp.exp(l_prev - m_t)
        cs = c_scr[h]
        ns = n_all[h:h + 1, :]
        ns8 = jnp.broadcast_to(ns, (SUBLANES, ML_HEAD_DIM)).astype(BF16)
        num = _dot(vt.astype(BF16), sc.astype(BF16)) + g_prev * _dot(cs.astype(BF16), qtb)
        den = jnp.sum(sc, axis=0, keepdims=True) + g_prev * _dot(ns8, qtb)[0:1]
        hout_t = num / jnp.maximum(jnp.abs(den), jnp.exp(-m_t))
        m_new = m_t[:, CHUNK - 1:CHUNK]
        bc_last = bc_r[:, CHUNK - 1:CHUNK]
        wend = jnp.exp(bc_last - bc_r + li_r - m_new)
        decay = jnp.exp(bc_last + m_prev - m_new)
        c_scr[h] = decay * cs + _dot((vt * wend).astype(BF16), kb)
        wend8 = jnp.broadcast_to(wend, (SUBLANES, CHUNK)).astype(BF16)
        n_rows.append(decay * ns + _dot(wend8, kb)[0:1])
        m_rows.append(jnp.broadcast_to(m_new, (1, LANES)))
        hb = jax.nn.sigmoid(o_all[:, sl]) * hout_t.T
        yb_ref[0, :, sl] = _rms(hb, ng[:, sl]) * _silu(zb[:, sl])
    pad = [jnp.zeros((SUBLANES - ML_HEADS, LANES), F32)]
    n_scr[...] = jnp.concatenate(n_rows + pad, axis=0)
    m_scr[...] = jnp.concatenate(m_rows + pad, axis=0)


SSD_PARAMS = ("conv_a_w", "conv_a_b", "dtb_row", "alog_row", "dtb_col", "alog_col", "dskip_row", "ssd_norm_g")
MLSTM_PARAMS = ("conv_b_w", "conv_b_b", "w_mq_t", "w_mk", "w_mv_t", "gb_row", "gb_col", "ml_norm_g")


def _scan_kernel(xbc_ref, misc_ref, za_ref, xm_ref, o_ref, zb_ref, *rest):
    ns, nm = len(SSD_PARAMS), len(MLSTM_PARAMS)
    ssd_params, ml_params = rest[:ns], rest[ns:ns + nm]
    ya_ref, hfin_ref, sconv_ref, yb_ref, cfin_ref, nfin_ref, mfin_ref, mconv_ref = rest[ns + nm:ns + nm + 8]
    h_scr, sext_scr, y_scr, c_scr, n_scr, m_scr, mext_scr = rest[ns + nm + 8:]
    c = pl.program_id(1)
    nc = pl.num_programs(1)

    @pl.when(c == 0)
    def _():
        h_scr[...] = jnp.zeros_like(h_scr)
        sext_scr[0:SUBLANES, :] = jnp.zeros((SUBLANES, SSD_CONV_CH), F32)
        c_scr[...] = jnp.zeros_like(c_scr)
        n_scr[...] = jnp.zeros_like(n_scr)
        m_scr[...] = jnp.zeros_like(m_scr)
        mext_scr[0:SUBLANES, :] = jnp.zeros((SUBLANES, GROUP_W), F32)

    _ssd_chunk(xbc_ref, misc_ref, za_ref, *ssd_params, ya_ref, h_scr, sext_scr, y_scr)
    _mlstm_chunk(xm_ref, o_ref, zb_ref, misc_ref, *ml_params, yb_ref, c_scr, n_scr, m_scr, mext_scr)

    @pl.when(c == nc - 1)
    def _():
        hfin_ref[0] = h_scr[...]
        sconv_ref[0] = xbc_ref[0, CHUNK - (CONV_W - 1):CHUNK, :]
        cfin_ref[0] = c_scr[...]
        nfin_ref[0] = n_scr[...]
        mfin_ref[0] = m_scr[...]
        mconv_ref[0] = xm_ref[0, CHUNK - (CONV_W - 1):CHUNK, :]


def _scans_prompt(xbc, xm, o, z, misc, lp, layer):
    b, l, _ = xbc.shape
    nc = l // CHUNK
    row = lambda bi, ci: (bi, ci, 0)
    first = lambda nd: (lambda bi, ci: (bi,) + (0,) * (nd - 1))
    tokens = lambda w: pl.BlockSpec((1, CHUNK, w), row)
    pa, pa_specs = _layer_args(lp, layer, SSD_PARAMS)
    pb, pb_specs = _layer_args(lp, layer, MLSTM_PARAMS)
    outs = pl.pallas_call(
        _scan_kernel,
        grid=(b, nc),
        in_specs=[tokens(SSD_CONV_CH), tokens(MISC_W),
                  pl.BlockSpec((1, CHUNK, GROUP_W), lambda bi, ci: (bi, ci, 0)),
                  tokens(GROUP_W), tokens(GROUP_W),
                  pl.BlockSpec((1, CHUNK, GROUP_W), lambda bi, ci: (bi, ci, 1)),
                  ] + pa_specs + pb_specs,
        out_specs=[tokens(GROUP_W),
                   pl.BlockSpec((1, SSD_HEADS, SSD_HEAD_DIM, SSD_STATE), first(4)),
                   pl.BlockSpec((1, CONV_W - 1, SSD_CONV_CH), first(3)),
                   tokens(GROUP_W),
                   pl.BlockSpec((1, ML_HEADS, ML_HEAD_DIM, ML_HEAD_DIM), first(4)),
                   pl.BlockSpec((1, SUBLANES, LANES), first(3)),
                   pl.BlockSpec((1, SUBLANES, LANES), first(3)),
                   pl.BlockSpec((1, CONV_W - 1, GROUP_W), first(3))],
        out_shape=[jax.ShapeDtypeStruct((b, l, GROUP_W), F32),
                   jax.ShapeDtypeStruct((b, SSD_HEADS, SSD_HEAD_DIM, SSD_STATE), F32),
                   jax.ShapeDtypeStruct((b, CONV_W - 1, SSD_CONV_CH), F32),
                   jax.ShapeDtypeStruct((b, l, GROUP_W), F32),
                   jax.ShapeDtypeStruct((b, ML_HEADS, ML_HEAD_DIM, ML_HEAD_DIM), F32),
                   jax.ShapeDtypeStruct((b, SUBLANES, LANES), F32),
                   jax.ShapeDtypeStruct((b, SUBLANES, LANES), F32),
                   jax.ShapeDtypeStruct((b, CONV_W - 1, GROUP_W), F32)],
        scratch_shapes=[pltpu.VMEM((SSD_HEADS, SSD_HEAD_DIM, SSD_STATE), F32),
                        pltpu.VMEM((SUBLANES + CHUNK, SSD_CONV_CH), F32),
                        pltpu.VMEM((CHUNK, GROUP_W), F32),
                        pltpu.VMEM((ML_HEADS, ML_HEAD_DIM, ML_HEAD_DIM), F32),
                        pltpu.VMEM((SUBLANES, LANES), F32),
                        pltpu.VMEM((SUBLANES, LANES), F32),
                        pltpu.VMEM((SUBLANES + CHUNK, GROUP_W), F32)],
        compiler_params=_cparams(("parallel", "arbitrary")),
        name="scans_prompt",
    )(xbc, misc, z, xm, o, z, *pa, *pb)
    ya, ssm_h, ssm_buf, yb, cfin, nfin, mfin, mconv = outs
    return ya, ssm_h, ssm_buf, yb, cfin, nfin[:, :ML_HEADS, :], mfin[:, :ML_HEADS, 0], mconv


def _rope_rows(r, g, cos2, sinsgn):
    half = MLA_ROPE // 2
    rn = r * lax.rsqrt(jnp.sum(r * r, axis=-1, keepdims=True) * (1.0 / MLA_ROPE) + EPS) * g
    lane = lax.broadcasted_iota(jnp.int32, rn.shape, 1)
    r_sw = jnp.where(lane < half, pltpu.roll(rn, LANES - half, 1), pltpu.roll(rn, half, 1))
    return rn * cos2 + r_sw * sinsgn


def _mla_prep_kernel(cq_ref, ckv_ref, misc_ref, cos_ref, sin_ref, qg_ref, kvg_ref, wqn_ref, wqr_ref,
                     wuk_ref, qnn_ref, qnr_ref, knn_ref, knr_ref, q_ref, k_ref, v_ref, lat_ref, kr_ref,
                     *, transposed):
    cos2 = cos_ref[...]
    sinsgn = sin_ref[...]
    c_q = _rms(cq_ref[0], qg_ref[...]).astype(BF16)
    qn_all = _dot(c_q, wqn_ref[...])
    c_kv = _rms(ckv_ref[0], kvg_ref[...])
    lat_ref[0] = c_kv
    c_kv_b = c_kv.astype(BF16)
    if transposed:
        v_ref[0] = c_kv.T.astype(BF16)
    else:
        v_ref[0] = c_kv_b
    misc = misc_ref[0]
    lane = lax.broadcasted_iota(jnp.int32, misc.shape, 1)
    kr_raw = jnp.where(lane < MLA_ROPE, misc, 0.0)
    kr = _rope_rows(kr_raw, knr_ref[...], cos2, sinsgn)
    kr_ref[0] = kr[:, 0:MLA_ROPE]
    kr_b = kr[:, 0:MLA_ROPE].astype(BF16)
    kn_all = _dot(c_kv_b, wuk_ref[...])
    for h in range(MLA_HEADS):
        sl = slice(h * MLA_NOPE, (h + 1) * MLA_NOPE)
        qn = _rms(qn_all[:, sl], qnn_ref[...])
        qr = _rope_rows(_dot(c_q, wqr_ref[h]), qnr_ref[...], cos2, sinsgn)
        if transposed:
            q_ref[0, h, 0:MLA_NOPE, :] = qn.T.astype(BF16)
            q_ref[0, h, MLA_NOPE:MLA_QK, :] = qr.T[0:MLA_ROPE].astype(BF16)
        else:
            q_ref[0, h, :, 0:MLA_NOPE] = qn.astype(BF16)
            q_ref[0, h, :, MLA_NOPE:MLA_QK] = qr[:, 0:MLA_ROPE].astype(BF16)
        k_ref[0, h, :, 0:MLA_NOPE] = _rms(kn_all[:, sl], knn_ref[...]).astype(BF16)
        k_ref[0, h, :, MLA_NOPE:MLA_QK] = kr_b


def _mla_prep(cq, ckv, misc, cos2, sinsgn, lp, layer, tm, transposed):
    b, l, _ = cq.shape
    params, param_specs = _layer_args(lp, layer, ("q_norm_g", "kv_norm_g", "w_uq_nope", "w_uq_rope", "w_uk",
                                                  "qn_nope_g", "qn_rope_g_pad", "kn_nope_g", "kn_rope_g_pad"))
    row = lambda bi, ti: (bi, ti, 0)
    hrow = lambda bi, ti: (bi, 0, ti, 0)
    if transposed:
        q_spec = pl.BlockSpec((1, MLA_HEADS, MLA_QK, tm), lambda bi, ti: (bi, 0, 0, ti))
        q_shape = jax.ShapeDtypeStruct((b, MLA_HEADS, MLA_QK, l), BF16)
        v_spec = pl.BlockSpec((1, KV_RANK, tm), lambda bi, ti: (bi, 0, ti))
        v_shape = jax.ShapeDtypeStruct((b, KV_RANK, l), BF16)
    else:
        q_spec = pl.BlockSpec((1, MLA_HEADS, tm, MLA_QK), hrow)
        q_shape = jax.ShapeDtypeStruct((b, MLA_HEADS, l, MLA_QK), BF16)
        v_spec = pl.BlockSpec((1, tm, KV_RANK), row)
        v_shape = jax.ShapeDtypeStruct((b, l, KV_RANK), BF16)
    return pl.pallas_call(
        functools.partial(_mla_prep_kernel, transposed=transposed),
        grid=(b, l // tm),
        in_specs=[pl.BlockSpec((1, tm, Q_RANK), row), pl.BlockSpec((1, tm, KV_RANK), row),
                  pl.BlockSpec((1, tm, MISC_W), row),
                  pl.BlockSpec((tm, LANES), lambda bi, ti: (ti, 0)),
                  pl.BlockSpec((tm, LANES), lambda bi, ti: (ti, 0))] + param_specs,
        out_specs=[q_spec,
                   pl.BlockSpec((1, MLA_HEADS, tm, MLA_QK), hrow),
                   v_spec,
                   pl.BlockSpec((1, tm, KV_RANK), row),
                   pl.BlockSpec((1, tm, MLA_ROPE), row)],
        out_shape=[q_shape,
                   jax.ShapeDtypeStruct((b, MLA_HEADS, l, MLA_QK), BF16),
                   v_shape,
                   jax.ShapeDtypeStruct((b, l, KV_RANK), F32),
                   jax.ShapeDtypeStruct((b, l, MLA_ROPE), F32)],
        compiler_params=_cparams(("parallel", "parallel")),
        name="mla_prep",
    )(cq, ckv, misc, cos2, sinsgn, *params)


def _mla_attn_kernel(qt_ref, k_ref, vt_ref, zc_ref, wuv_ref, out_ref, m_scr, l_scr, acc_scr, *, tile):
    qi = pl.program_id(1)
    scale = MLA_QK ** -0.5
    m_scr[...] = jnp.full_like(m_scr, -jnp.inf)
    l_scr[...] = jnp.zeros_like(l_scr)
    acc_scr[...] = jnp.zeros_like(acc_scr)

    def kv_block(start, width, diagonal):
        vt = vt_ref[0, :, pl.ds(start, width)]
        scores = [_dot(k_ref[0, h, pl.ds(start, width), :], qt_ref[0, h]) for h in range(MLA_HEADS)]
        for h in range(MLA_HEADS):
            s = scores[h] * scale
            if diagonal:
                kid = lax.broadcasted_iota(jnp.int32, (width, tile), 0)
                qid = lax.broadcasted_iota(jnp.int32, (width, tile), 1)
                s = jnp.where(kid <= qid, s, -jnp.inf)
            m_prev = m_scr[h:h + 1, :]
            m_new = jnp.maximum(m_prev, jnp.max(s, axis=0, keepdims=True))
            alpha = jnp.exp(m_prev - m_new)
            p = jnp.exp(s - m_new)
            l_scr[h:h + 1, :] = alpha * l_scr[h:h + 1, :] + jnp.sum(p, axis=0, keepdims=True)
            acc_scr[h] = alpha * acc_scr[h] + _dot(vt, p.astype(BF16))
            m_scr[h:h + 1, :] = m_new

    def pair_body(p, carry):
        kv_block(pl.multiple_of(p * 2 * tile, 2 * tile), 2 * tile, False)
        return carry

    def single_body(_, carry):
        kv_block(pl.multiple_of((qi - 1) * tile, tile), tile, False)
        return carry

    lax.fori_loop(0, qi // 2, pair_body, 0)
    lax.fori_loop(0, qi % 2, single_body, 0)
    kv_block(pl.multiple_of(qi * tile, tile), tile, True)
    for h in range(MLA_HEADS):
        sl = slice(h * MLA_V, (h + 1) * MLA_V)
        ctx_t = (acc_scr[h] / l_scr[h:h + 1, :]).astype(BF16)
        out_ref[0, :, sl] = _dot_tn(ctx_t, wuv_ref[:, sl]) * _silu(zc_ref[0, :, sl])


def _mla_attn_prompt(qt, k, latt, z, w_uv, layer, tile=256):
    b, hds, _, l = qt.shape
    kern = functools.partial(_mla_attn_kernel, tile=tile)
    return pl.pallas_call(
        kern,
        grid=(b, l // tile),
        in_specs=[pl.BlockSpec((1, hds, MLA_QK, tile), lambda bi, qi: (bi, 0, 0, qi)),
                  pl.BlockSpec((1, hds, l, MLA_QK), lambda bi, qi: (bi, 0, 0, 0)),
                  pl.BlockSpec((1, KV_RANK, l), lambda bi, qi: (bi, 0, 0)),
                  pl.BlockSpec((1, tile, GROUP_W), lambda bi, qi: (bi, qi, 2)),
                  _layer_spec(w_uv, layer)],
        out_specs=pl.BlockSpec((1, tile, GROUP_W), lambda bi, qi: (bi, qi, 0)),
        out_shape=jax.ShapeDtypeStruct((b, l, GROUP_W), F32),
        scratch_shapes=[pltpu.VMEM((SUBLANES, tile), F32), pltpu.VMEM((SUBLANES, tile), F32),
                        pltpu.VMEM((hds, KV_RANK, tile), F32)],
        compiler_params=_cparams(("parallel", "arbitrary")),
        name="mla_attn",
    )(qt, k, latt, z, w_uv)


def _memkv_kernel(mem_ref, g_ref, wk_ref, wv_ref, xkg_ref, mk_ref, mv_ref):
    mem_len = mem_ref.shape[1]
    mem_n = _rms(mem_ref[0], g_ref[0]).astype(BF16)
    kraw = _dot(mem_n, wk_ref[0])
    vraw = _dot(mem_n, wv_ref[0])
    for h in range(X_HEADS):
        sl = slice(h * X_HEAD_DIM, (h + 1) * X_HEAD_DIM)
        rows = pl.ds(h, mem_len, stride=X_HEADS)
        mk_ref[0, 0, rows, :] = _rms(kraw[:, sl], xkg_ref[0])
        mv_ref[0, 0, rows, :] = vraw[:, sl]


def _memkv(mem, mem_norm_g, w_mem_k, w_mem_v, xk_norm_g):
    b, m, _ = mem.shape
    depth = w_mem_k.shape[0]
    lsel = lambda li, bi: (li, 0, 0)
    return pl.pallas_call(
        _memkv_kernel,
        grid=(depth, b),
        in_specs=[pl.BlockSpec((1, m, D_MODEL), lambda li, bi: (bi, 0, 0)),
                  pl.BlockSpec((1, 1, D_MODEL), lsel),
                  pl.BlockSpec((1, D_MODEL, GROUP_W), lsel),
                  pl.BlockSpec((1, D_MODEL, GROUP_W), lsel),
                  pl.BlockSpec((1, 1, X_HEAD_DIM), lsel)],
        out_specs=[pl.BlockSpec((1, 1, m * X_HEADS, X_HEAD_DIM), lambda li, bi: (li, bi, 0, 0)),
                   pl.BlockSpec((1, 1, m * X_HEADS, X_HEAD_DIM), lambda li, bi: (li, bi, 0, 0))],
        out_shape=[jax.ShapeDtypeStruct((depth, b, m * X_HEADS, X_HEAD_DIM), F32),
                   jax.ShapeDtypeStruct((depth, b, m * X_HEADS, X_HEAD_DIM), F32)],
        compiler_params=_cparams(("parallel", "parallel")),
        name="memkv",
    )(mem, mem_norm_g, w_mem_k, w_mem_v, xk_norm_g)


def _xattn_kernel(xq_ref, mk_ref, mv_ref, zd_ref, g_ref, out_ref, *, rows, mem_len):
    xq = xq_ref[0]
    zd = zd_ref[0]
    if rows == 1:
        heads = [xq[:, h * X_HEAD_DIM:(h + 1) * X_HEAD_DIM] for h in range(X_HEADS)]
        qn = _rms(jnp.concatenate(heads + heads, axis=0), g_ref[...]).astype(BF16)
        s = _dot_nt(qn, mk_ref[0, 0].astype(BF16)) * (X_HEAD_DIM ** -0.5)
        row_head = lax.broadcasted_iota(jnp.int32, s.shape, 0) & (X_HEADS - 1)
        col_head = lax.broadcasted_iota(jnp.int32, s.shape, 1) & (X_HEADS - 1)
        s = jnp.where(row_head == col_head, s, -jnp.inf)
        e = jnp.exp(s - jnp.max(s, axis=1, keepdims=True))
        p = e / jnp.sum(e, axis=1, keepdims=True)
        y = _dot(p.astype(BF16), mv_ref[0, 0].astype(BF16))
        out_ref[0] = jnp.concatenate([y[h:h + 1] for h in range(X_HEADS)], axis=1) * _silu(zd)
        return
    for h in range(X_HEADS):
        sl = slice(h * X_HEAD_DIM, (h + 1) * X_HEAD_DIM)
        mk_h = mk_ref[0, 0, pl.ds(h, mem_len, stride=X_HEADS), :]
        mv_h = mv_ref[0, 0, pl.ds(h, mem_len, stride=X_HEADS), :]
        qn = _rms(xq[:, sl], g_ref[...]).astype(BF16)
        s = _dot_nt(qn, mk_h.astype(BF16)) * (X_HEAD_DIM ** -0.5)
        s = s - jnp.max(s, axis=1, keepdims=True)
        e = jnp.exp(s)
        p = e / jnp.sum(e, axis=1, keepdims=True)
        y = _dot(p.astype(BF16), mv_h.astype(BF16))
        out_ref[0, :, sl] = y * _silu(zd[:, sl])


def _xattn(xq, mk, mv, layer, z, g, tm):
    b, l, _ = xq.shape
    mem_len = mk.shape[2] // X_HEADS
    kern = functools.partial(_xattn_kernel, rows=tm, mem_len=mem_len)
    mem_spec = pl.BlockSpec((1, 1) + mk.shape[2:], lambda bi, ti: (layer, bi, 0, 0))
    return pl.pallas_call(
        kern,
        grid=(b, l // tm),
        in_specs=[pl.BlockSpec((1, tm, GROUP_W), lambda bi, ti: (bi, ti, 0)),
                  mem_spec, mem_spec,
                  pl.BlockSpec((1, tm, GROUP_W), lambda bi, ti: (bi, ti, 3)),
                  _layer_spec(g, layer)],
        out_specs=pl.BlockSpec((1, tm, GROUP_W), lambda bi, ti: (bi, ti, 0)),
        out_shape=jax.ShapeDtypeStruct((b, l, GROUP_W), F32),
        compiler_params=_cparams(("parallel", "parallel")),
        name="xattn",
    )(xq, mk, mv, z, g)


def _outproj_kernel(ya_ref, yb_ref, yc_ref, yd_ref, x_ref, w_ref, fg_ref, out_ref, *, final):
    acc = x_ref[...]
    for i, ref in enumerate((ya_ref, yb_ref, yc_ref, yd_ref)):
        acc = acc + _dot(ref[...].astype(BF16), w_ref[i * GROUP_W:(i + 1) * GROUP_W, :])
    if final:
        acc = _rms(acc, fg_ref[...])
    out_ref[...] = acc


def _outproj(ys, x2d, w_out, layer, final_g, tm, final):
    n = x2d.shape[0]
    kern = functools.partial(_outproj_kernel, final=final)
    yspec = pl.BlockSpec((tm, GROUP_W), lambda i: (i, 0))
    return pl.pallas_call(
        kern,
        grid=(n // tm,),
        in_specs=[yspec, yspec, yspec, yspec,
                  pl.BlockSpec((tm, D_MODEL), lambda i: (i, 0)),
                  _layer_spec(w_out, layer), _const_spec((1, D_MODEL))],
        out_specs=pl.BlockSpec((tm, D_MODEL), lambda i: (i, 0)),
        out_shape=jax.ShapeDtypeStruct((n, D_MODEL), F32),
        compiler_params=_cparams(("parallel",)),
        name="outproj",
    )(*ys, x2d, w_out, final_g)


def _col_form(row, n):
    return jnp.broadcast_to(row, (LANES, n)).T


N_DEC_STATES = 6


def _dec_state_kernel(*refs):
    xbc_ref, xm_ref, o_ref, z_ref, misc_ref = refs[0:5]
    hs_ref, sbuf_ref, cm_ref, nm_ref, mm_ref, mbuf_ref = (r.at[0] for r in refs[5:5 + N_DEC_STATES])
    (cwa_ref, cba_ref, dtb_ref, alog_ref, hexp_ref, dskip_ref, nga_ref,
     cwb_ref, cbb_ref, wq_ref, wk_ref, wv_ref, gb_ref, ngb_ref) = refs[11:25]
    ya_ref, yb_ref = refs[-8:-6]
    hs_out, sbuf_out, cm_out, nm_out, mm_out, mbuf_out = (r.at[0] for r in refs[-N_DEC_STATES:])
    misc = misc_ref[0]
    z = z_ref[0]

    u = xbc_ref[0]
    buf = sbuf_ref[0]
    cw = cwa_ref[...]
    conv = cba_ref[...] + cw[3:4] * u
    for j in range(CONV_W - 1):
        conv = conv + cw[j:j + 1] * buf[j:j + 1]
    sbuf_out[0, 0:CONV_W - 2, :] = buf[1:CONV_W - 1]
    sbuf_out[0, CONV_W - 2:CONV_W - 1, :] = u
    xbcs = _silu(conv)
    xa = xbcs[:, 0:GROUP_W]
    dt = _softplus(misc + dtb_ref[...])
    a = -jnp.exp(alog_ref[...])
    hexp = hexp_ref[...]
    dt8 = jnp.broadcast_to(dt, (SUBLANES, LANES))
    da8 = jnp.broadcast_to(dt * a, (SUBLANES, LANES))
    dt_x = jnp.dot(dt8, hexp, precision=HIGHEST, preferred_element_type=F32)[0:1]
    decay_x = jnp.exp(jnp.dot(da8, hexp, precision=HIGHEST, preferred_element_type=F32)[0:1])
    hstate = hs_ref[0].reshape(SSD_HEADS * SSD_HEAD_DIM, SSD_STATE)
    gw = GROUP_W // SSD_GROUPS
    c_off = GROUP_W + SSD_GROUPS * SSD_STATE
    dtx_col = _col_form(dt_x * xa, GROUP_W)
    decay_col = _col_form(decay_x, GROUP_W)
    y_parts = []
    for g in range(SSD_GROUPS):
        b_g = xbcs[:, GROUP_W + g * SSD_STATE:GROUP_W + (g + 1) * SSD_STATE]
        c_g = xbcs[:, c_off + g * SSD_STATE:c_off + (g + 1) * SSD_STATE]
        cb = jnp.sum(c_g * b_g, axis=1, keepdims=True)
        rows = slice(g * gw, (g + 1) * gw)
        h_g = hstate[rows]
        c8 = jnp.broadcast_to(c_g, (SUBLANES, SSD_STATE)).astype(BF16)
        ch = _dot_nt(c8, h_g.astype(BF16))[0:1]
        y_parts.append(cb * dt_x[:, rows] * xa[:, rows] + ch * decay_x[:, rows])
        h_new = decay_col[rows] * h_g + dtx_col[rows] * b_g
        hs_out[0, g * SSD_HPG:(g + 1) * SSD_HPG] = h_new.reshape(SSD_HPG, SSD_HEAD_DIM, SSD_STATE)
    y = jnp.concatenate(y_parts, axis=1) + dskip_ref[...] * xa
    y = y * _silu(z[:, 0:GROUP_W])
    nga = nga_ref[...]
    for g in range(SSD_GROUPS):
        rows = slice(g * gw, (g + 1) * gw)
        ya_ref[0, :, rows] = _rms(y[:, rows], nga[:, rows])

    um = xm_ref[0]
    mbuf = mbuf_ref[0]
    cwb = cwb_ref[...]
    convb = cbb_ref[...] + cwb[3:4] * um
    for j in range(CONV_W - 1):
        convb = convb + cwb[j:j + 1] * mbuf[j:j + 1]
    mbuf_out[0, 0:CONV_W - 2, :] = mbuf[1:CONV_W - 1]
    mbuf_out[0, CONV_W - 2:CONV_W - 1, :] = um
    xmc = _silu(convb)
    gates = misc + gb_ref[...]
    lf_all = -_softplus(-gates)
    mm = mm_ref[0]
    ngb = ngb_ref[...]
    o_all = o_ref[0]
    for h in range(ML_HEADS):
        sl = slice(h * ML_HEAD_DIM, (h + 1) * ML_HEAD_DIM)
        li = gates[:, MISC_I + h:MISC_I + h + 1]
        lf = lf_all[:, MISC_F + h:MISC_F + h + 1]
        m_prev = mm[:, h:h + 1]
        l_prev = lf + m_prev
        m_t = jnp.maximum(l_prev, li)
        xmc8 = jnp.broadcast_to(xmc[:, sl], (SUBLANES, ML_HEAD_DIM)).astype(BF16)
        um8 = jnp.broadcast_to(um[:, sl], (SUBLANES, ML_HEAD_DIM)).astype(BF16)
        q8 = _dot(xmc8, wq_ref[h])
        q = q8[0:1]
        k = _dot(xmc8, wk_ref[h])[0:1] * (ML_HEAD_DIM ** -0.5)
        v = _dot(um8, wv_ref[h])[0:1]
        qk = jnp.sum(q * k, axis=1, keepdims=True)
        sc = qk * jnp.exp(li - m_t)
        g_prev = jnp.exp(l_prev - m_t)
        cs = cm_ref[0, h]
        ns = nm_ref[0, h:h + 1, :]
        cq = _dot_nt(q8.astype(BF16), cs.astype(BF16))[0:1]
        num = sc * v + g_prev * cq
        den = sc + g_prev * jnp.sum(ns * q, axis=1, keepdims=True)
        hout = num / jnp.maximum(jnp.abs(den), jnp.exp(-m_t))
        wend = jnp.exp(li - m_t)
        decay = jnp.exp(l_prev - m_t)
        cm_out[0, h] = decay * cs + _col_form(wend * v, ML_HEAD_DIM) * k
        nm_out[0, h:h + 1, :] = decay * ns + wend * k
        mm_out[0, :, h:h + 1] = m_t
        hb = jax.nn.sigmoid(o_all[:, sl]) * hout
        yb_ref[0, :, sl] = _rms(hb, ngb[:, sl]) * _silu(z[:, GROUP_W + h * ML_HEAD_DIM:GROUP_W + (h + 1) * ML_HEAD_DIM])


def _dec_state(xbc, xm, o, z, misc, states, layer, lp):
    b = xbc.shape[0]
    tok = lambda w: pl.BlockSpec((1, 1, w), lambda bi: (bi, 0, 0))
    at_layer = lambda t: pl.BlockSpec((1, 1) + t.shape[2:], lambda bi: (layer, bi) + (0,) * (t.ndim - 2))
    state_specs = [at_layer(t) for t in states]
    pa, pa_specs = _layer_args(lp, layer, ("conv_a_w", "conv_a_b", "dtb_row", "alog_row"))
    pb, pb_specs = _layer_args(lp, layer, ("dskip_row", "ssd_norm_g", "conv_b_w", "conv_b_b", "w_mq", "w_mk", "w_mv",
                                           "gb_row", "ml_norm_g"))
    params = pa + [lp["head_expand"]] + pb
    param_specs = pa_specs + [_const_spec((LANES, GROUP_W))] + pb_specs
    inputs = [xbc, xm, o, z, misc] + list(states) + params
    in_specs = [tok(SSD_CONV_CH), tok(GROUP_W), tok(GROUP_W), tok(D_MIX), tok(MISC_W)] + state_specs + param_specs
    aliases = {5 + i: 2 + i for i in range(N_DEC_STATES)}
    outs = pl.pallas_call(
        _dec_state_kernel,
        grid=(b,),
        in_specs=in_specs,
        out_specs=[tok(GROUP_W), tok(GROUP_W)] + state_specs,
        out_shape=[jax.ShapeDtypeStruct((b, 1, GROUP_W), F32)] * 2
        + [jax.ShapeDtypeStruct(t.shape, F32) for t in states],
        input_output_aliases=aliases,
        compiler_params=_cparams(("parallel",)),
        name="dec_state",
    )(*inputs)
    return outs[0], outs[1], tuple(outs[2:])


def _mla_decode_kernel(pt_ref, lat_hbm, kr_hbm, qn_ref, qr_ref, qfull_ref, kfull_ref, vnew_ref, zc_ref,
                       wukt_ref, wuv_ref, kng_ref, out_ref,
                       lat_buf, kr_buf, sem, qa_scr, m_scr, l_scr, acc_scr, *, layer, pps, page, nsub):
    b = pl.program_id(0)
    s = pl.program_id(1)
    ns = pl.num_programs(1)
    nb = pl.num_programs(0)
    step = b * ns + s
    last = nb * ns - 1
    slot = step % DECODE_SLOTS
    sub = pps * page // nsub

    def copies(bb, ss, sl):
        out = []
        for p in range(pps):
            pg = pt_ref[bb, ss * pps + p]
            out.append(pltpu.make_async_copy(lat_hbm.at[layer, pg], lat_buf.at[sl, pl.ds(p * page, page), :],
                                             sem.at[0, sl]))
            out.append(pltpu.make_async_copy(kr_hbm.at[layer, pg], kr_buf.at[sl, :, pl.ds(p * page, page)],
                                             sem.at[1, sl]))
        return out

    def start_step(target, sl):
        tgt = jnp.minimum(target, last)
        for cp in copies(tgt // ns, tgt % ns, sl):
            cp.start()

    @pl.when(step == 0)
    def _():
        start_step(step, 0)
        start_step(step + 1, 1)

    scale = MLA_QK ** -0.5

    @pl.when(s == 0)
    def _():
        qng = (qn_ref[0].astype(F32) * kng_ref[...]).astype(BF16)
        for h in range(MLA_HEADS):
            full = _dot(qng, wukt_ref[h * MLA_NOPE:(h + 1) * MLA_NOPE, :])
            qa_scr[h:h + 1, :] = full[h:h + 1]
            qa_scr[MLA_HEADS + h:MLA_HEADS + h + 1, :] = full[h:h + 1]
        m_scr[...] = jnp.full_like(m_scr, -jnp.inf)
        l_scr[...] = jnp.zeros_like(l_scr)
        acc_scr[...] = jnp.zeros_like(acc_scr)

    for cp in copies(b, s, slot):
        cp.wait()

    qa = qa_scr[...].astype(BF16)
    qr = qr_ref[0]
    wukt = wukt_ref[...]

    def scores(i):
        latb = lat_buf[slot, i * sub:(i + 1) * sub, :].astype(BF16)
        krb = kr_buf[slot, :, i * sub:(i + 1) * sub].astype(BF16)
        kraw_t = _dot_nt(wukt, latb)
        sa = _dot_nt(qa, latb)
        sr = _dot(qr, krb)
        rinv_rows = []
        for h in range(MLA_HEADS):
            kh = kraw_t[h * MLA_NOPE:(h + 1) * MLA_NOPE, :]
            ssq = jnp.sum(kh * kh, axis=0, keepdims=True)
            rinv_rows.append(lax.rsqrt(ssq * (1.0 / MLA_NOPE) + EPS))
        rinv = jnp.concatenate(rinv_rows + rinv_rows, axis=0)
        return latb, (sa * rinv + sr) * scale

    m = m_scr[...]
    l = l_scr[...]
    acc = acc_scr[...]
    pending = scores(0)
    for i in range(nsub):
        latb, sc = pending
        if i + 1 < nsub:
            pending = scores(i + 1)
        m_new = jnp.maximum(m, jnp.max(sc, axis=1, keepdims=True))
        alpha = jnp.exp(m - m_new)
        p = jnp.exp(sc - m_new)
        l = alpha * l + jnp.sum(p, axis=1, keepdims=True)
        acc = alpha * acc + _dot(p.astype(BF16), latb)
        m = m_new
    m_scr[...] = m
    l_scr[...] = l
    acc_scr[...] = acc

    start_step(step + 2, (step + 2) % DECODE_SLOTS)

    @pl.when(s == ns - 1)
    def _():
        s_new = jnp.sum(qfull_ref[0].astype(F32) * kfull_ref[0].astype(F32), axis=1, keepdims=True) * scale
        m_fin = jnp.maximum(m, s_new)
        alpha2 = jnp.exp(m - m_fin)
        p_new = jnp.exp(s_new - m_fin)
        l_fin = alpha2 * l + p_new
        acc_fin = alpha2 * acc + p_new * vnew_ref[0].astype(F32)
        ctx = (acc_fin / l_fin).astype(BF16)
        yfull = _dot(ctx, wuv_ref[...])
        parts = [yfull[h:h + 1, h * MLA_V:(h + 1) * MLA_V] for h in range(MLA_HEADS)]
        out_ref[0] = jnp.concatenate(parts, axis=1) * _silu(zc_ref[0])

    @pl.when(step == last)
    def _():
        for ahead in (1, 2):
            for cp in copies(b, s, (step + ahead) % DECODE_SLOTS):
                cp.wait()


DECODE_PAGES_PER_STEP = 32
DECODE_SUB_BLOCKS = 2
DECODE_SLOTS = 3


def _mla_decode(page_table, cache_lat, cache_kr, layer, qn8, qr8, qfull8, kfull8, vnew, z, lp,
                pps=DECODE_PAGES_PER_STEP, nsub=DECODE_SUB_BLOCKS):
    b, n_pages = page_table.shape
    page = cache_lat.shape[2]
    ns = n_pages // pps
    t = pps * page
    kern = functools.partial(_mla_decode_kernel, layer=layer, pps=pps, page=page, nsub=nsub)
    per_b = lambda shape: pl.BlockSpec((1,) + shape, lambda bi, si, pt: (bi, 0, 0))
    params, param_specs = _layer_args(lp, layer, ("w_uk_t", "w_uv", "kn_nope_g"))
    grid_spec = pltpu.PrefetchScalarGridSpec(
        num_scalar_prefetch=1,
        grid=(b, ns),
        in_specs=[pl.BlockSpec(memory_space=pl.ANY), pl.BlockSpec(memory_space=pl.ANY),
                  per_b((SUBLANES, MLA_NOPE)), per_b((SUBLANES, MLA_ROPE)),
                  per_b((SUBLANES, MLA_QK)), per_b((SUBLANES, MLA_QK)), per_b((1, KV_RANK)),
                  pl.BlockSpec((1, 1, GROUP_W), lambda bi, si, pt: (bi, 0, 2)),
                  ] + param_specs,
        out_specs=pl.BlockSpec((1, 1, GROUP_W), lambda bi, si, pt: (bi, 0, 0)),
        scratch_shapes=[pltpu.VMEM((DECODE_SLOTS, t, KV_RANK), F32), pltpu.VMEM((DECODE_SLOTS, MLA_ROPE, t), F32),
                        pltpu.SemaphoreType.DMA((2, DECODE_SLOTS)),
                        pltpu.VMEM((SUBLANES, KV_RANK), F32),
                        pltpu.VMEM((SUBLANES, 1), F32), pltpu.VMEM((SUBLANES, 1), F32),
                        pltpu.VMEM((SUBLANES, KV_RANK), F32)],
    )
    return pl.pallas_call(
        kern,
        grid_spec=grid_spec,
        out_shape=jax.ShapeDtypeStruct((b, 1, GROUP_W), F32),
        compiler_params=_cparams(("arbitrary", "arbitrary")),
        name="mla_decode",
    )(page_table, cache_lat, cache_kr, qn8, qr8, qfull8, kfull8, vnew, z, *params)


def _pack_w_in(w_in):
    names = ("z", "xbc", "dt", "xm", "i", "f", "o", "cq", "ckv", "kr", "xq")
    offs = {}
    o = 0
    for nme, w in zip(names, IN_SIZES):
        offs[nme] = (o, o + w)
        o += w
    col = lambda nme: w_in[:, :, offs[nme][0]:offs[nme][1]].astype(BF16)
    pad = jnp.zeros(w_in.shape[:2] + (MISC_W - (MISC_F + ML_HEADS),), BF16)
    misc = jnp.concatenate([col("kr"), col("dt"), col("i"), col("f"), pad], axis=-1)
    return [misc if nme == "misc" else col(nme) for nme, _ in PACK_SEGS]


def _stack_params(p):
    depth = p["norm_g"].shape[0]
    row = lambda a: a[:, None, :]
    lp = {}
    for nme in ("norm_g", "conv_a_b", "ssd_norm_g", "conv_b_b", "ml_norm_g", "q_norm_g", "kv_norm_g", "qn_nope_g",
                "kn_nope_g", "xq_norm_g"):
        lp[nme] = row(p[nme])
    lp["conv_a_w"] = p["conv_a_w"]
    lp["conv_b_w"] = p["conv_b_w"]
    lp["dtb_row"] = _lane_pad(p["dt_bias"], MISC_DT)
    lp["alog_row"] = _lane_pad(p["a_log"], MISC_DT)
    lp["dtb_col"] = p["dt_bias"][:, :, None]
    lp["alog_col"] = p["a_log"][:, :, None]
    lp["dskip_row"] = row(jnp.repeat(p["d_skip"], SSD_HEAD_DIM, axis=1))
    lp["w_mq"] = p["w_mq"].astype(BF16)
    lp["w_mk"] = p["w_mk"].astype(BF16)
    lp["w_mv"] = p["w_mv"].astype(BF16)
    lp["w_mq_t"] = jnp.swapaxes(lp["w_mq"], 2, 3)
    lp["w_mv_t"] = jnp.swapaxes(lp["w_mv"], 2, 3)
    gate_b = jnp.concatenate([p["b_i"], p["b_f"]], axis=1)
    lp["gb_row"] = _lane_pad(gate_b, MISC_I)
    lp["gb_col"] = gate_b[:, :, None]
    w_uq = p["w_uq"]
    lp["w_uq_nope"] = w_uq[..., :MLA_NOPE].reshape(depth, Q_RANK, MLA_HEADS * MLA_NOPE).astype(BF16)
    rope_pad = lambda t: jnp.pad(t, [(0, 0)] * (t.ndim - 1) + [(0, LANES - MLA_ROPE)])
    lp["w_uq_rope"] = rope_pad(jnp.transpose(w_uq[..., MLA_NOPE:], (0, 2, 1, 3))).astype(BF16)
    w_uk = p["w_uk"].reshape(depth, KV_RANK, MLA_HEADS * MLA_NOPE)
    lp["w_uk"] = w_uk.astype(BF16)
    lp["w_uk_t"] = jnp.swapaxes(w_uk, 1, 2).astype(BF16)
    lp["w_uv"] = p["w_uv"].reshape(depth, KV_RANK, MLA_HEADS * MLA_V).astype(BF16)
    lp["qn_rope_g_pad"] = rope_pad(row(p["qn_rope_g"]))
    lp["kn_rope_g_pad"] = rope_pad(row(p["kn_rope_g"]))
    lp["w_out"] = p["w_out"].astype(BF16)
    head_of_lane = jnp.arange(LANES)[:, None] - MISC_DT
    head_of_chan = jnp.arange(GROUP_W)[None, :] // SSD_HEAD_DIM
    lp["head_expand"] = (head_of_lane == head_of_chan).astype(F32)
    return lp


def _rope_tables(pos):
    half = MLA_ROPE // 2
    inv = ROPE_BASE ** (-jnp.arange(half, dtype=F32) * 2.0 / MLA_ROPE)
    ang = pos.astype(F32)[:, None] * inv[None, :]
    cos = jnp.cos(ang)
    sin = jnp.sin(ang)
    zero = jnp.zeros((pos.shape[0], LANES - MLA_ROPE), F32)
    return jnp.concatenate([cos, cos, zero], axis=1), jnp.concatenate([-sin, sin, zero], axis=1)


def _prompt_layer(lp, w_slabs, layer, x, mk_all, mv_all, rope_tabs, final_g, final):
    b, l, _ = x.shape
    n = b * l
    x2d = x.reshape(n, D_MODEL)
    z, xbc, xm, o, cq, ckv, xq, misc = _inproj(x2d, lp["norm_g"], w_slabs, layer, tm=512)
    r3 = lambda t: t.reshape(b, l, t.shape[-1])
    z, xbc, xm, o, cq, ckv, xq, misc = map(r3, (z, xbc, xm, o, cq, ckv, xq, misc))
    ya, ssm_h, ssm_buf, yb, ml_c, ml_n, ml_m, ml_buf = _scans_prompt(xbc, xm, o, z, misc, lp, layer)
    qt, k, latt, lat, kr = _mla_prep(cq, ckv, misc, rope_tabs[0], rope_tabs[1], lp, layer, tm=256,
                                     transposed=True)
    yc = _mla_attn_prompt(qt, k, latt, z, lp["w_uv"], layer)
    yd = _xattn(xq, mk_all, mv_all, layer, z, lp["xq_norm_g"], tm=512)
    ys = [t.reshape(n, GROUP_W) for t in (ya, yb, yc, yd)]
    x_new = _outproj(ys, x2d, lp["w_out"], layer, final_g, tm=512, final=final).reshape(b, l, D_MODEL)
    return x_new, (lat, kr, ssm_h, ssm_buf, ml_c, ml_n, ml_m, ml_buf)


def _sample_layer(lp, w_slabs, layer, x, mem_k, mem_v, cache_lat, cache_kr, page_table, states,
                  rope_tabs, final_g, final, pps=DECODE_PAGES_PER_STEP):
    b = x.shape[0]
    x2d = x.reshape(b, D_MODEL)
    z, xbc, xm, o, cq, ckv, xq, misc = _inproj(x2d, lp["norm_g"], w_slabs, layer, tm=b)
    tok = lambda t: t.reshape(b, 1, t.shape[-1])
    ya, yb, st_new = _dec_state(tok(xbc), tok(xm), tok(o), tok(z), tok(misc), states, layer, lp)
    one = lambda t: t.reshape(1, b, t.shape[-1])
    q, k, latb, lat, kr = _mla_prep(one(cq), one(ckv), one(misc), rope_tabs[0], rope_tabs[1], lp, layer, tm=b,
                                    transposed=False)
    qh = jnp.transpose(q[0], (1, 0, 2))
    kh = jnp.transpose(k[0], (1, 0, 2))
    dup = lambda t: jnp.concatenate([t, t], axis=1)
    qfull8 = dup(qh)
    kfull8 = dup(kh)
    qn8 = qfull8[:, :, :MLA_NOPE]
    qr8 = qfull8[:, :, MLA_NOPE:]
    vnew = latb.reshape(b, 1, KV_RANK)
    yc = _mla_decode(page_table, cache_lat, cache_kr, layer, qn8, qr8, qfull8, kfull8, vnew, tok(z), lp, pps=pps)
    yd = _xattn(tok(xq), mem_k, mem_v, layer, tok(z), lp["xq_norm_g"], tm=1)
    ys = [t.reshape(b, GROUP_W) for t in (ya, yb, yc, yd)]
    x_new = _outproj(ys, x2d, lp["w_out"], layer, final_g, tm=b, final=final).reshape(b, 1, D_MODEL)
    return x_new, lat.reshape(b, 1, KV_RANK), kr.reshape(b, 1, MLA_ROPE), st_new


def kernel(x_prompt, x_sample, mem_prompt, cache_mla_latent, cache_mla_krope, cache_mem_k, cache_mem_v, state_ssm, state_ssm_conv, state_mlstm_C, state_mlstm_n, state_mlstm_m, state_mlstm_conv, page_table, norm_g, w_in, conv_a_w, conv_a_b, dt_bias, a_log, d_skip, ssd_norm_g, conv_b_w, conv_b_b, w_mq, w_mk, w_mv, b_i, b_f, ml_norm_g, q_norm_g, kv_norm_g, w_uq, w_uk, w_uv, qn_nope_g, qn_rope_g, kn_nope_g, kn_rope_g, mem_norm_g, w_mem_k, w_mem_v, xk_norm_g, xq_norm_g, w_out, final_norm_g):
    params = dict(norm_g=norm_g, conv_a_w=conv_a_w, conv_a_b=conv_a_b, dt_bias=dt_bias, a_log=a_log, d_skip=d_skip,
                  ssd_norm_g=ssd_norm_g, conv_b_w=conv_b_w, conv_b_b=conv_b_b, w_mq=w_mq, w_mk=w_mk, w_mv=w_mv,
                  b_i=b_i, b_f=b_f, ml_norm_g=ml_norm_g, q_norm_g=q_norm_g, kv_norm_g=kv_norm_g, w_uq=w_uq,
                  w_uk=w_uk, w_uv=w_uv, qn_nope_g=qn_nope_g, qn_rope_g=qn_rope_g, kn_nope_g=kn_nope_g,
                  kn_rope_g=kn_rope_g, xq_norm_g=xq_norm_g, w_out=w_out)
    depth = w_in.shape[0]
    bp, lp_len, _ = x_prompt.shape
    bs, ls_len, _ = x_sample.shape
    n_pages = page_table.shape[1]
    past_len = n_pages * cache_mla_latent.shape[2]
    mem_len = mem_prompt.shape[1]
    w_slabs = _pack_w_in(w_in)
    final_g = final_norm_g[None, :]

    cache_kr_t = jnp.swapaxes(cache_mla_krope, 2, 3)
    rope_p = _rope_tables(jnp.arange(lp_len, dtype=jnp.int32))
    rope_s = _rope_tables(jnp.full((bs,), past_len, dtype=jnp.int32))

    mk_all, mv_all = _memkv(mem_prompt, mem_norm_g[:, None, :], w_mem_k.astype(BF16), w_mem_v.astype(BF16),
                            xk_norm_g[:, None, :])

    mem_rows = cache_mem_k.shape[2] * X_HEADS
    cache_mk = cache_mem_k.reshape(depth, bs, mem_rows, X_HEAD_DIM)
    cache_mv = cache_mem_v.reshape(depth, bs, mem_rows, X_HEAD_DIM)
    states = (state_ssm, state_ssm_conv, state_mlstm_C, state_mlstm_n,
              state_mlstm_m.reshape(depth, bs, 1, ML_HEADS), state_mlstm_conv)

    hp, hs = x_prompt, x_sample
    p_states, s_lats, s_krs = [], [], []
    lp = _stack_params(params)
    for l in range(depth):
        final = l == depth - 1
        hp, st_p = _prompt_layer(lp, w_slabs, l, hp, mk_all, mv_all, rope_p, final_g, final)
        p_states.append(st_p)
        hs, s_lat_l, s_kr_l, states = _sample_layer(lp, w_slabs, l, hs, cache_mk, cache_mv, cache_mla_latent,
                                                    cache_kr_t, page_table, states, rope_s, final_g, final)
        s_lats.append(s_lat_l)
        s_krs.append(s_kr_l)

    p_out = [jnp.stack([st[i] for st in p_states]) for i in range(8)]
    p_mem_k = mk_all.reshape(depth, bp, mem_len, X_HEADS, X_HEAD_DIM)
    p_mem_v = mv_all.reshape(depth, bp, mem_len, X_HEADS, X_HEAD_DIM)
    (p_lat, p_kr, p_ssm, p_sconv, p_mc, p_mn, p_mm, p_mconv) = p_out
    s_ssm, s_sconv, s_mc, s_mn, s_mm, s_mconv = states
    return (hp, hs, p_lat, p_kr, p_mem_k, p_mem_v, p_ssm, p_sconv, p_mc, p_mn, p_mm, p_mconv,
            jnp.stack(s_lats), jnp.stack(s_krs), s_ssm, s_sconv, s_mc, s_mn, s_mm.reshape(depth, bs, ML_HEADS),
            s_mconv)
```

```python
import functools
import math

import jax
import jax.numpy as jnp
from jax import lax
from jax.experimental import pallas as pl
from jax.experimental.pallas import tpu as pltpu

F32 = jnp.float32
BF16 = jnp.bfloat16
HIGHEST = lax.Precision.HIGHEST

D_MODEL = 1024
D_MIX = 2 * D_MODEL
GROUP_W = D_MIX // 4
SSD_HEAD_DIM = 64
SSD_HEADS = GROUP_W // SSD_HEAD_DIM
SSD_GROUPS = 2
SSD_HPG = SSD_HEADS // SSD_GROUPS
SSD_STATE = 128
CONV_W = 4
SSD_CONV_CH = GROUP_W + 2 * SSD_GROUPS * SSD_STATE
ML_HEADS = 4
ML_HEAD_DIM = GROUP_W // ML_HEADS
MLA_HEADS = 4
MLA_NOPE = 128
MLA_ROPE = 64
MLA_QK = MLA_NOPE + MLA_ROPE
MLA_V = GROUP_W // MLA_HEADS
Q_RANK = 3 * D_MODEL // 8
KV_RANK = D_MODEL // 4
ROPE_BASE = 10000.0
X_HEADS = 4
X_HEAD_DIM = GROUP_W // X_HEADS
CHUNK = 256
EPS = 1e-6
IN_SIZES = (D_MIX, SSD_CONV_CH, SSD_HEADS, GROUP_W, ML_HEADS, ML_HEADS, GROUP_W, Q_RANK, KV_RANK, MLA_ROPE, GROUP_W)

LANES = 128
SUBLANES = 8
VMEM_LIMIT = 56 * 1024 * 1024

MISC_W = LANES
MISC_KR = 0
MISC_DT = MLA_ROPE
MISC_I = MISC_DT + SSD_HEADS
MISC_F = MISC_I + ML_HEADS
PACK_SEGS = (("z", D_MIX), ("xbc", SSD_CONV_CH), ("xm", GROUP_W), ("o", GROUP_W), ("cq", Q_RANK),
             ("ckv", KV_RANK), ("xq", GROUP_W), ("misc", MISC_W))
PACK_W = sum(w for _, w in PACK_SEGS)


def _cparams(sem):
    return pltpu.CompilerParams(dimension_semantics=sem, vmem_limit_bytes=VMEM_LIMIT)


def _dot(a, b):
    return jnp.dot(a, b, preferred_element_type=F32)


def _dot_nt(a, b, precision=None):
    return lax.dot_general(a, b, (((1,), (1,)), ((), ())), preferred_element_type=F32, precision=precision)


def _dot_tn(a, b):
    return lax.dot_general(a, b, (((0,), (0,)), ((), ())), preferred_element_type=F32)


def _rms(x, g):
    return x * lax.rsqrt(jnp.mean(x * x, axis=-1, keepdims=True) + EPS) * g


def _silu(x):
    return x * jax.nn.sigmoid(x)


def _softplus(x):
    return jnp.maximum(x, 0.0) + jnp.log1p(jnp.exp(-jnp.abs(x)))


def _const_spec(shape):
    nd = len(shape)
    return pl.BlockSpec(shape, lambda *_: (0,) * nd)


def _layer_spec(arr, layer):
    nd = arr.ndim
    return pl.BlockSpec((None,) + arr.shape[1:], lambda *_: (layer,) + (0,) * (nd - 1))


def _layer_args(lp, layer, names):
    return [lp[n] for n in names], [_layer_spec(lp[n], layer) for n in names]


def _inproj_kernel(x_ref, g_ref, *refs):
    nseg = len(PACK_SEGS)
    w_refs, out_refs = refs[:nseg], refs[nseg:]
    x = x_ref[...]
    u = _rms(x, g_ref[...]).astype(BF16)
    for w_ref, out_ref in zip(w_refs, out_refs):
        out_ref[...] = _dot(u, w_ref[0])


def _inproj(x2d, g, w_slabs, layer, tm):
    n = x2d.shape[0]
    out_shape = [jax.ShapeDtypeStruct((n, w), F32) for _, w in PACK_SEGS]
    out_specs = [pl.BlockSpec((tm, w), lambda i: (i, 0)) for _, w in PACK_SEGS]
    w_specs = [pl.BlockSpec((1, D_MODEL, w), lambda i: (layer, 0, 0), pipeline_mode=pl.Buffered(1))
               for _, w in PACK_SEGS]
    return pl.pallas_call(
        _inproj_kernel,
        grid=(n // tm,),
        in_specs=[pl.BlockSpec((tm, D_MODEL), lambda i: (i, 0)), _layer_spec(g, layer)] + w_specs,
        out_specs=out_specs,
        out_shape=out_shape,
        compiler_params=_cparams(("parallel",)),
        name="inproj",
    )(x2d, g, *w_slabs)


def _tri_masks():
    r = lax.broadcasted_iota(jnp.int32, (CHUNK, CHUNK), 0)
    c = lax.broadcasted_iota(jnp.int32, (CHUNK, CHUNK), 1)
    return r >= c, c >= r


def _ssd_chunk(xbc_ref, misc_ref, za_ref, cw_ref, cb_ref, dtb_row_ref, alog_row_ref, dtb_col_ref,
               alog_col_ref, dskip_ref, ng_ref, ya_ref, h_scr, ext_scr, y_scr):
    u = xbc_ref[0]
    ext_scr[SUBLANES:SUBLANES + CHUNK, :] = u
    cw = cw_ref[...]
    conv = cb_ref[...] + cw[3:4] * u
    for j in range(CONV_W - 1):
        back = CONV_W - 1 - j
        conv = conv + cw[j:j + 1] * ext_scr[SUBLANES - back:SUBLANES - back + CHUNK, :]
    ext_scr[0:SUBLANES, :] = u[CHUNK - SUBLANES:CHUNK, :]
    xbcs = _silu(conv)
    xa = xbcs[:, 0:GROUP_W]

    misc = misc_ref[0]
    misc_t = misc.T
    lower, upper = _tri_masks()
    lower_f = lower.astype(F32)
    upper_f = upper.astype(F32)
    dt_col = _softplus(misc + dtb_row_ref[...])
    dt_row = _softplus(misc_t[MISC_DT:MISC_DT + SSD_HEADS, :] + dtb_col_ref[...])
    a_row = -jnp.exp(alog_row_ref[...])
    a_col = -jnp.exp(alog_col_ref[...])
    acum_col = jnp.dot(lower_f, dt_col * a_row, precision=HIGHEST, preferred_element_type=F32)
    acum_row = jnp.dot(dt_row * a_col, upper_f, precision=HIGHEST, preferred_element_type=F32)

    for g in range(SSD_GROUPS):
        b_g = xbcs[:, GROUP_W + g * SSD_STATE:GROUP_W + (g + 1) * SSD_STATE].astype(BF16)
        c_off = GROUP_W + SSD_GROUPS * SSD_STATE
        c_g = xbcs[:, c_off + g * SSD_STATE:c_off + (g + 1) * SSD_STATE].astype(BF16)
        cb = _dot_nt(c_g, b_g)
        for hh in range(SSD_HPG):
            h = g * SSD_HPG + hh
            ac_col = acum_col[:, MISC_DT + h:MISC_DT + h + 1]
            ac_row = acum_row[h:h + 1, :]
            seg = jnp.where(lower, ac_col - ac_row, -jnp.inf)
            w = cb * jnp.exp(seg) * dt_row[h:h + 1, :]
            x_h = xa[:, h * SSD_HEAD_DIM:(h + 1) * SSD_HEAD_DIM]
            hs = h_scr[h]
            y_h = _dot(w.astype(BF16), x_h.astype(BF16)) + _dot_nt(c_g, hs.astype(BF16)) * jnp.exp(ac_col)
            y_scr[:, h * SSD_HEAD_DIM:(h + 1) * SSD_HEAD_DIM] = y_h
            ac_last = ac_col[CHUNK - 1:CHUNK, :]
            wend = jnp.exp(ac_last - ac_col) * dt_col[:, MISC_DT + h:MISC_DT + h + 1]
            xw = (x_h * wend).astype(BF16)
            h_scr[h] = jnp.exp(ac_last) * hs + _dot_tn(xw, b_g)

    y = y_scr[...] + dskip_ref[...] * xa
    y = y * _silu(za_ref[0])
    gw = GROUP_W // SSD_GROUPS
    ng = ng_ref[...]
    for g in range(SSD_GROUPS):
        ya_ref[0, :, g * gw:(g + 1) * gw] = _rms(y[:, g * gw:(g + 1) * gw], ng[:, g * gw:(g + 1) * gw])


def _lane_pad(v, offset):
    return jnp.pad(v, ((0, 0), (offset, LANES - offset - v.shape[1])))[:, None, :]


def _mlstm_chunk(xm_ref, o_ref, zb_ref, misc_ref, cw_ref, cb_ref, wqt_ref, wk_ref, wvt_ref, gb_row_ref,
                 gb_col_ref, ng_ref, yb_ref, c_scr, n_scr, m_scr, ext_scr):
    u = xm_ref[0]
    ext_scr[SUBLANES:SUBLANES + CHUNK, :] = u
    cw = cw_ref[...]
    conv = cb_ref[...] + cw[3:4] * u
    for j in range(CONV_W - 1):
        back = CONV_W - 1 - j
        conv = conv + cw[j:j + 1] * ext_scr[SUBLANES - back:SUBLANES - back + CHUNK, :]
    ext_scr[0:SUBLANES, :] = u[CHUNK - SUBLANES:CHUNK, :]
    xmc = _silu(conv)

    misc = misc_ref[0]
    misc_t = misc.T
    lower, upper = _tri_masks()
    gates_col = misc + gb_row_ref[...]
    gates_row = misc_t[MISC_I:MISC_I + 2 * ML_HEADS, :] + gb_col_ref[...]
    lf_col = -_softplus(-gates_col)
    lf_row = -_softplus(-gates_row)
    bcum_col = jnp.dot(lower.astype(F32), lf_col, precision=HIGHEST, preferred_element_type=F32)
    bcum_row = jnp.dot(lf_row, upper.astype(F32), precision=HIGHEST, preferred_element_type=F32)
    ng = ng_ref[...]
    o_all = o_ref[0]
    zb = zb_ref[0]
    m_all = m_scr[...]
    n_all = n_scr[...]
    m_rows, n_rows = [], []
    for h in range(ML_HEADS):
        sl = slice(h * ML_HEAD_DIM, (h + 1) * ML_HEAD_DIM)
        li_r = gates_row[h:h + 1, :]
        bc_r = bcum_row[ML_HEADS + h:ML_HEADS + h + 1, :]
        col = gates_col[:, MISC_I + h:MISC_I + h + 1] - bcum_col[:, MISC_F + h:MISC_F + h + 1]
        m_prev = m_all[h:h + 1, 0:1]
        lw = jnp.where(upper, bc_r + col, -jnp.inf)
        l_prev = bc_r + m_prev
        m_t = jnp.maximum(l_prev, jnp.max(lw, axis=0, keepdims=True))
        xmc_h = xmc[:, sl].astype(BF16)
        qtb = _dot_nt(wqt_ref[h], xmc_h).astype(BF16)
        k = _dot(xmc_h, wk_ref[h]) * (ML_HEAD_DIM ** -0.5)
        kb = k.astype(BF16)
        vt = _dot_nt(wvt_ref[h], u[:, sl].astype(BF16))
        sc = _dot(kb, qtb) * jnp.exp(lw - m_t)
        g_prev = jnp.exp(l_prev - m_t)
        cs = c_scr[h]
        ns = n_all[h:h + 1, :]
        ns8 = jnp.broadcast_to(ns, (SUBLANES, ML_HEAD_DIM)).astype(BF16)
        num = _dot(vt.astype(BF16), sc.astype(BF16)) + g_prev * _dot(cs.astype(BF16), qtb)
        den = jnp.sum(sc, axis=0, keepdims=True) + g_prev * _dot(ns8, qtb)[0:1]
        hout_t = num / jnp.maximum(jnp.abs(den), jnp.exp(-m_t))
        m_new = m_t[:, CHUNK - 1:CHUNK]
        bc_last = bc_r[:, CHUNK - 1:CHUNK]
        wend = jnp.exp(bc_last - bc_r + li_r - m_new)
        decay = jnp.exp(bc_last + m_prev - m_new)
        c_scr[h] = decay * cs + _dot((vt * wend).astype(BF16), kb)
        wend8 = jnp.broadcast_to(wend, (SUBLANES, CHUNK)).astype(BF16)
        n_rows.append(decay * ns + _dot(wend8, kb)[0:1])
        m_rows.append(jnp.broadcast_to(m_new, (1, LANES)))
        hb = jax.nn.sigmoid(o_all[:, sl]) * hout_t.T
        yb_ref[0, :, sl] = _rms(hb, ng[:, sl]) * _silu(zb[:, sl])
    pad = [jnp.zeros((SUBLANES - ML_HEADS, LANES), F32)]
    n_scr[...] = jnp.concatenate(n_rows + pad, axis=0)
    m_scr[...] = jnp.concatenate(m_rows + pad, axis=0)


SSD_PARAMS = ("conv_a_w", "conv_a_b", "dtb_row", "alog_row", "dtb_col", "alog_col", "dskip_row", "ssd_norm_g")
MLSTM_PARAMS = ("conv_b_w", "conv_b_b", "w_mq_t", "w_mk", "w_mv_t", "gb_row", "gb_col", "ml_norm_g")


def _scan_kernel(xbc_ref, misc_ref, za_ref, xm_ref, o_ref, zb_ref, *rest):
    ns, nm = len(SSD_PARAMS), len(MLSTM_PARAMS)
    ssd_params, ml_params = rest[:ns], rest[ns:ns + nm]
    ya_ref, hfin_ref, sconv_ref, yb_ref, cfin_ref, nfin_ref, mfin_ref, mconv_ref = rest[ns + nm:ns + nm + 8]
    h_scr, sext_scr, y_scr, c_scr, n_scr, m_scr, mext_scr = rest[ns + nm + 8:]
    c = pl.program_id(1)
    nc = pl.num_programs(1)

    @pl.when(c == 0)
    def _():
        h_scr[...] = jnp.zeros_like(h_scr)
        sext_scr[0:SUBLANES, :] = jnp.zeros((SUBLANES, SSD_CONV_CH), F32)
        c_scr[...] = jnp.zeros_like(c_scr)
        n_scr[...] = jnp.zeros_like(n_scr)
        m_scr[...] = jnp.zeros_like(m_scr)
        mext_scr[0:SUBLANES, :] = jnp.zeros((SUBLANES, GROUP_W), F32)

    _ssd_chunk(xbc_ref, misc_ref, za_ref, *ssd_params, ya_ref, h_scr, sext_scr, y_scr)
    _mlstm_chunk(xm_ref, o_ref, zb_ref, misc_ref, *ml_params, yb_ref, c_scr, n_scr, m_scr, mext_scr)

    @pl.when(c == nc - 1)
    def _():
        hfin_ref[0] = h_scr[...]
        sconv_ref[0] = xbc_ref[0, CHUNK - (CONV_W - 1):CHUNK, :]
        cfin_ref[0] = c_scr[...]
        nfin_ref[0] = n_scr[...]
        mfin_ref[0] = m_scr[...]
        mconv_ref[0] = xm_ref[0, CHUNK - (CONV_W - 1):CHUNK, :]


def _scans_prompt(xbc, xm, o, z, misc, lp, layer):
    b, l, _ = xbc.shape
    nc = l // CHUNK
    row = lambda bi, ci: (bi, ci, 0)
    first = lambda nd: (lambda bi, ci: (bi,) + (0,) * (nd - 1))
    tokens = lambda w: pl.BlockSpec((1, CHUNK, w), row)
    pa, pa_specs = _layer_args(lp, layer, SSD_PARAMS)
    pb, pb_specs = _layer_args(lp, layer, MLSTM_PARAMS)
    outs = pl.pallas_call(
        _scan_kernel,
        grid=(b, nc),
        in_specs=[tokens(SSD_CONV_CH), tokens(MISC_W),
                  pl.BlockSpec((1, CHUNK, GROUP_W), lambda bi, ci: (bi, ci, 0)),
                  tokens(GROUP_W), tokens(GROUP_W),
                  pl.BlockSpec((1, CHUNK, GROUP_W), lambda bi, ci: (bi, ci, 1)),
                  ] + pa_specs + pb_specs,
        out_specs=[tokens(GROUP_W),
                   pl.BlockSpec((1, SSD_HEADS, SSD_HEAD_DIM, SSD_STATE), first(4)),
                   pl.BlockSpec((1, CONV_W - 1, SSD_CONV_CH), first(3)),
                   tokens(GROUP_W),
                   pl.BlockSpec((1, ML_HEADS, ML_HEAD_DIM, ML_HEAD_DIM), first(4)),
                   pl.BlockSpec((1, SUBLANES, LANES), first(3)),
                   pl.BlockSpec((1, SUBLANES, LANES), first(3)),
                   pl.BlockSpec((1, CONV_W - 1, GROUP_W), first(3))],
        out_shape=[jax.ShapeDtypeStruct((b, l, GROUP_W), F32),
                   jax.ShapeDtypeStruct((b, SSD_HEADS, SSD_HEAD_DIM, SSD_STATE), F32),
                   jax.ShapeDtypeStruct((b, CONV_W - 1, SSD_CONV_CH), F32),
                   jax.ShapeDtypeStruct((b, l, GROUP_W), F32),
                   jax.ShapeDtypeStruct((b, ML_HEADS, ML_HEAD_DIM, ML_HEAD_DIM), F32),
                   jax.ShapeDtypeStruct((b, SUBLANES, LANES), F32),
                   jax.ShapeDtypeStruct((b, SUBLANES, LANES), F32),
                   jax.ShapeDtypeStruct((b, CONV_W - 1, GROUP_W), F32)],
        scratch_shapes=[pltpu.VMEM((SSD_HEADS, SSD_HEAD_DIM, SSD_STATE), F32),
                        pltpu.VMEM((SUBLANES + CHUNK, SSD_CONV_CH), F32),
                        pltpu.VMEM((CHUNK, GROUP_W), F32),
                        pltpu.VMEM((ML_HEADS, ML_HEAD_DIM, ML_HEAD_DIM), F32),
                        pltpu.VMEM((SUBLANES, LANES), F32),
                        pltpu.VMEM((SUBLANES, LANES), F32),
                        pltpu.VMEM((SUBLANES + CHUNK, GROUP_W), F32)],
        compiler_params=_cparams(("parallel", "arbitrary")),
        name="scans_prompt",
    )(xbc, misc, z, xm, o, z, *pa, *pb)
    ya, ssm_h, ssm_buf, yb, cfin, nfin, mfin, mconv = outs
    return ya, ssm_h, ssm_buf, yb, cfin, nfin[:, :ML_HEADS, :], mfin[:, :ML_HEADS, 0], mconv


def _rope_rows(r, g, cos2, sinsgn):
    half = MLA_ROPE // 2
    rn = r * lax.rsqrt(jnp.sum(r * r, axis=-1, keepdims=True) * (1.0 / MLA_ROPE) + EPS) * g
    lane = lax.broadcasted_iota(jnp.int32, rn.shape, 1)
    r_sw = jnp.where(lane < half, pltpu.roll(rn, LANES - half, 1), pltpu.roll(rn, half, 1))
    return rn * cos2 + r_sw * sinsgn


def _mla_prep_kernel(cq_ref, ckv_ref, misc_ref, cos_ref, sin_ref, cos_t_ref, sin_t_ref, qg_ref, kvg_ref, wqn_ref,
                     wqr_ref, wuk_ref, qnn_ref, qnr_ref, knn_ref, knr_ref, q_ref, k_ref, v_ref, lat_ref, kr_ref,
                     *, transposed):
    cos2 = cos_ref[...]
    sinsgn = sin_ref[...]
    c_q = _rms(cq_ref[0], qg_ref[...]).astype(BF16)
    if not transposed:
        qn_all = _dot(c_q, wqn_ref[...])
    c_kv = _rms(ckv_ref[0], kvg_ref[...])
    lat_ref[0] = c_kv
    c_kv_b = c_kv.astype(BF16)
    if transposed:
        v_ref[0] = c_kv.T.astype(BF16)
    else:
        v_ref[0] = c_kv_b
    misc = misc_ref[0]
    lane = lax.broadcasted_iota(jnp.int32, misc.shape, 1)
    kr_raw = jnp.where(lane < MLA_ROPE, misc, 0.0)
    kr = _rope_rows(kr_raw, knr_ref[...], cos2, sinsgn)
    kr_ref[0] = kr[:, 0:MLA_ROPE]
    kr_b = kr[:, 0:MLA_ROPE].astype(BF16)
    kn_all = _dot(c_kv_b, wuk_ref[...])
    half = MLA_ROPE // 2
    col_rms = lambda x, g: x * lax.rsqrt(jnp.mean(x * x, axis=0, keepdims=True) + EPS) * g
    for h in range(MLA_HEADS):
        sl = slice(h * MLA_NOPE, (h + 1) * MLA_NOPE)
        if transposed:
            qn_t = col_rms(_dot_nt(wqn_ref[sl, :], c_q), qnn_ref[...])
            qr_t = col_rms(_dot_nt(wqr_ref[h], c_q), qnr_ref[...])
            qr_sw = jnp.concatenate([qr_t[half:], qr_t[:half]], axis=0)
            qr_t = qr_t * cos_t_ref[...] + qr_sw * sin_t_ref[...]
            q_ref[0, h, 0:MLA_NOPE, :] = qn_t.astype(BF16)
            q_ref[0, h, MLA_NOPE:MLA_QK, :] = qr_t.astype(BF16)
        else:
            qn = _rms(qn_all[:, sl], qnn_ref[...])
            qr = _rope_rows(_dot(c_q, wqr_ref[h]), qnr_ref[...], cos2, sinsgn)
            q_ref[0, h, :, 0:MLA_NOPE] = qn.astype(BF16)
            q_ref[0, h, :, MLA_NOPE:MLA_QK] = qr[:, 0:MLA_ROPE].astype(BF16)
        k_ref[0, h, :, 0:MLA_NOPE] = _rms(kn_all[:, sl], knn_ref[...]).astype(BF16)
        k_ref[0, h, :, MLA_NOPE:MLA_QK] = kr_b


def _mla_prep(cq, ckv, misc, rope_tabs, lp, layer, tm, transposed):
    b, l, _ = cq.shape
    cos2, sinsgn, cos_t, sin_t = rope_tabs
    if transposed:
        names = ("q_norm_g", "kv_norm_g", "w_uq_nope_t", "w_uq_rope_t", "w_uk", "qn_nope_g_col", "qn_rope_g_col",
                 "kn_nope_g", "kn_rope_g_pad")
    else:
        names = ("q_norm_g", "kv_norm_g", "w_uq_nope", "w_uq_rope", "w_uk", "qn_nope_g", "qn_rope_g_pad",
                 "kn_nope_g", "kn_rope_g_pad")
    params, param_specs = _layer_args(lp, layer, names)
    row = lambda bi, ti: (bi, ti, 0)
    hrow = lambda bi, ti: (bi, 0, ti, 0)
    if transposed:
        q_spec = pl.BlockSpec((1, MLA_HEADS, MLA_QK, tm), lambda bi, ti: (bi, 0, 0, ti))
        q_shape = jax.ShapeDtypeStruct((b, MLA_HEADS, MLA_QK, l), BF16)
        v_spec = pl.BlockSpec((1, KV_RANK, tm), lambda bi, ti: (bi, 0, ti))
        v_shape = jax.ShapeDtypeStruct((b, KV_RANK, l), BF16)
    else:
        q_spec = pl.BlockSpec((1, MLA_HEADS, tm, MLA_QK), hrow)
        q_shape = jax.ShapeDtypeStruct((b, MLA_HEADS, l, MLA_QK), BF16)
        v_spec = pl.BlockSpec((1, tm, KV_RANK), row)
        v_shape = jax.ShapeDtypeStruct((b, l, KV_RANK), BF16)
    return pl.pallas_call(
        functools.partial(_mla_prep_kernel, transposed=transposed),
        grid=(b, l // tm),
        in_specs=[pl.BlockSpec((1, tm, Q_RANK), row), pl.BlockSpec((1, tm, KV_RANK), row),
                  pl.BlockSpec((1, tm, MISC_W), row),
                  pl.BlockSpec((tm, LANES), lambda bi, ti: (ti, 0)),
                  pl.BlockSpec((tm, LANES), lambda bi, ti: (ti, 0)),
                  pl.BlockSpec((MLA_ROPE, tm), lambda bi, ti: (0, ti)),
                  pl.BlockSpec((MLA_ROPE, tm), lambda bi, ti: (0, ti))] + param_specs,
        out_specs=[q_spec,
                   pl.BlockSpec((1, MLA_HEADS, tm, MLA_QK), hrow),
                   v_spec,
                   pl.BlockSpec((1, tm, KV_RANK), row),
                   pl.BlockSpec((1, tm, MLA_ROPE), row)],
        out_shape=[q_shape,
                   jax.ShapeDtypeStruct((b, MLA_HEADS, l, MLA_QK), BF16),
                   v_shape,
                   jax.ShapeDtypeStruct((b, l, KV_RANK), F32),
                   jax.ShapeDtypeStruct((b, l, MLA_ROPE), F32)],
        compiler_params=_cparams(("parallel", "parallel")),
        name="mla_prep",
    )(cq, ckv, misc, cos2, sinsgn, cos_t, sin_t, *params)


def _mla_attn_kernel(qt_ref, k_ref, vt_ref, zc_ref, wuv_ref, out_ref, m_scr, l_scr, acc_scr, *, tile):
    qi = pl.program_id(1)
    scale = MLA_QK ** -0.5
    m_scr[...] = jnp.full_like(m_scr, -jnp.inf)
    l_scr[...] = jnp.zeros_like(l_scr)
    acc_scr[...] = jnp.zeros_like(acc_scr)

    def kv_block(start, width, diagonal):
        vt = vt_ref[0, :, pl.ds(start, width)]
        scores = [_dot(k_ref[0, h, pl.ds(start, width), :], qt_ref[0, h]) for h in range(MLA_HEADS)]
        for h in range(MLA_HEADS):
            s = scores[h] * scale
            if diagonal:
                kid = lax.broadcasted_iota(jnp.int32, (width, tile), 0)
                qid = lax.broadcasted_iota(jnp.int32, (width, tile), 1)
                s = jnp.where(kid <= qid, s, -jnp.inf)
            m_prev = m_scr[h:h + 1, :]
            m_new = jnp.maximum(m_prev, jnp.max(s, axis=0, keepdims=True))
            alpha = jnp.exp(m_prev - m_new)
            p = jnp.exp(s - m_new)
            l_scr[h:h + 1, :] = alpha * l_scr[h:h + 1, :] + jnp.sum(p, axis=0, keepdims=True)
            acc_scr[h] = alpha * acc_scr[h] + _dot(vt, p.astype(BF16))
            m_scr[h:h + 1, :] = m_new

    def pair_body(p, carry):
        kv_block(pl.multiple_of(p * 2 * tile, 2 * tile), 2 * tile, False)
        return carry

    def single_body(_, carry):
        kv_block(pl.multiple_of((qi - 1) * tile, tile), tile, False)
        return carry

    lax.fori_loop(0, qi // 2, pair_body, 0)
    lax.fori_loop(0, qi % 2, single_body, 0)
    kv_block(pl.multiple_of(qi * tile, tile), tile, True)
    for h in range(MLA_HEADS):
        sl = slice(h * MLA_V, (h + 1) * MLA_V)
        ctx_t = (acc_scr[h] / l_scr[h:h + 1, :]).astype(BF16)
        out_ref[0, :, sl] = _dot_tn(ctx_t, wuv_ref[:, sl]) * _silu(zc_ref[0, :, sl])


def _mla_attn_prompt(qt, k, latt, z, w_uv, layer, tile=256):
    b, hds, _, l = qt.shape
    kern = functools.partial(_mla_attn_kernel, tile=tile)
    return pl.pallas_call(
        kern,
        grid=(b, l // tile),
        in_specs=[pl.BlockSpec((1, hds, MLA_QK, tile), lambda bi, qi: (bi, 0, 0, qi)),
                  pl.BlockSpec((1, hds, l, MLA_QK), lambda bi, qi: (bi, 0, 0, 0)),
                  pl.BlockSpec((1, KV_RANK, l), lambda bi, qi: (bi, 0, 0)),
                  pl.BlockSpec((1, tile, GROUP_W), lambda bi, qi: (bi, qi, 2)),
                  _layer_spec(w_uv, layer)],
        out_specs=pl.BlockSpec((1, tile, GROUP_W), lambda bi, qi: (bi, qi, 0)),
        out_shape=jax.ShapeDtypeStruct((b, l, GROUP_W), F32),
        scratch_shapes=[pltpu.VMEM((SUBLANES, tile), F32), pltpu.VMEM((SUBLANES, tile), F32),
                        pltpu.VMEM((hds, KV_RANK, tile), F32)],
        compiler_params=_cparams(("parallel", "arbitrary")),
        name="mla_attn",
    )(qt, k, latt, z, w_uv)


def _memkv_kernel(mem_ref, g_ref, wk_ref, wv_ref, xkg_ref, mk_ref, mv_ref):
    mem_len = mem_ref.shape[1]
    mem_n = _rms(mem_ref[0], g_ref[0]).astype(BF16)
    kraw = _dot(mem_n, wk_ref[0])
    vraw = _dot(mem_n, wv_ref[0])
    for h in range(X_HEADS):
        sl = slice(h * X_HEAD_DIM, (h + 1) * X_HEAD_DIM)
        rows = pl.ds(h, mem_len, stride=X_HEADS)
        mk_ref[0, 0, rows, :] = _rms(kraw[:, sl], xkg_ref[0])
        mv_ref[0, 0, rows, :] = vraw[:, sl]


def _memkv(mem, mem_norm_g, w_mem_k, w_mem_v, xk_norm_g):
    b, m, _ = mem.shape
    depth = w_mem_k.shape[0]
    lsel = lambda li, bi: (li, 0, 0)
    return pl.pallas_call(
        _memkv_kernel,
        grid=(depth, b),
        in_specs=[pl.BlockSpec((1, m, D_MODEL), lambda li, bi: (bi, 0, 0)),
                  pl.BlockSpec((1, 1, D_MODEL), lsel),
                  pl.BlockSpec((1, D_MODEL, GROUP_W), lsel),
                  pl.BlockSpec((1, D_MODEL, GROUP_W), lsel),
                  pl.BlockSpec((1, 1, X_HEAD_DIM), lsel)],
        out_specs=[pl.BlockSpec((1, 1, m * X_HEADS, X_HEAD_DIM), lambda li, bi: (li, bi, 0, 0)),
                   pl.BlockSpec((1, 1, m * X_HEADS, X_HEAD_DIM), lambda li, bi: (li, bi, 0, 0))],
        out_shape=[jax.ShapeDtypeStruct((depth, b, m * X_HEADS, X_HEAD_DIM), F32),
                   jax.ShapeDtypeStruct((depth, b, m * X_HEADS, X_HEAD_DIM), F32)],
        compiler_params=_cparams(("parallel", "parallel")),
        name="memkv",
    )(mem, mem_norm_g, w_mem_k, w_mem_v, xk_norm_g)


def _xattn_kernel(xq_ref, mk_ref, mv_ref, zd_ref, g_ref, out_ref, *, rows, mem_len, nbatch):
    if rows == 1:
        for bb in range(nbatch):
            xq = xq_ref[bb]
            heads = [xq[:, h * X_HEAD_DIM:(h + 1) * X_HEAD_DIM] for h in range(X_HEADS)]
            qn = _rms(jnp.concatenate(heads + heads, axis=0), g_ref[...]).astype(BF16)
            s = _dot_nt(qn, mk_ref[0, bb].astype(BF16)) * (X_HEAD_DIM ** -0.5)
            row_head = lax.broadcasted_iota(jnp.int32, s.shape, 0) & (X_HEADS - 1)
            col_head = lax.broadcasted_iota(jnp.int32, s.shape, 1) & (X_HEADS - 1)
            s = jnp.where(row_head == col_head, s, -jnp.inf)
            e = jnp.exp(s - jnp.max(s, axis=1, keepdims=True))
            p = e / jnp.sum(e, axis=1, keepdims=True)
            y = _dot(p.astype(BF16), mv_ref[0, bb].astype(BF16))
            out_ref[bb] = jnp.concatenate([y[h:h + 1] for h in range(X_HEADS)], axis=1) * _silu(zd_ref[bb])
        return
    xq = xq_ref[0]
    zd = zd_ref[0]
    for h in range(X_HEADS):
        sl = slice(h * X_HEAD_DIM, (h + 1) * X_HEAD_DIM)
        mk_h = mk_ref[0, 0, pl.ds(h, mem_len, stride=X_HEADS), :]
        mv_h = mv_ref[0, 0, pl.ds(h, mem_len, stride=X_HEADS), :]
        qn = _rms(xq[:, sl], g_ref[...]).astype(BF16)
        s = _dot_nt(qn, mk_h.astype(BF16)) * (X_HEAD_DIM ** -0.5)
        s = s - jnp.max(s, axis=1, keepdims=True)
        e = jnp.exp(s)
        p = e / jnp.sum(e, axis=1, keepdims=True)
        y = _dot(p.astype(BF16), mv_h.astype(BF16))
        out_ref[0, :, sl] = y * _silu(zd[:, sl])


XATTN_DECODE_BATCH = 4


def _xattn(xq, mk, mv, layer, z, g, tm):
    b, l, _ = xq.shape
    mem_len = mk.shape[2] // X_HEADS
    nb = XATTN_DECODE_BATCH if tm == 1 else 1
    kern = functools.partial(_xattn_kernel, rows=tm, mem_len=mem_len, nbatch=nb)
    mem_spec = pl.BlockSpec((1, nb) + mk.shape[2:], lambda bi, ti: (layer, bi, 0, 0))
    return pl.pallas_call(
        kern,
        grid=(b // nb, l // tm),
        in_specs=[pl.BlockSpec((nb, tm, GROUP_W), lambda bi, ti: (bi, ti, 0)),
                  mem_spec, mem_spec,
                  pl.BlockSpec((nb, tm, GROUP_W), lambda bi, ti: (bi, ti, 3)),
                  _layer_spec(g, layer)],
        out_specs=pl.BlockSpec((nb, tm, GROUP_W), lambda bi, ti: (bi, ti, 0)),
        out_shape=jax.ShapeDtypeStruct((b, l, GROUP_W), F32),
        compiler_params=_cparams(("parallel", "parallel")),
        name="xattn",
    )(xq, mk, mv, z, g)


def _outproj_kernel(ya_ref, yb_ref, yc_ref, yd_ref, x_ref, w_ref, fg_ref, out_ref, *, final):
    acc = x_ref[...]
    for i, ref in enumerate((ya_ref, yb_ref, yc_ref, yd_ref)):
        acc = acc + _dot(ref[...].astype(BF16), w_ref[i * GROUP_W:(i + 1) * GROUP_W, :])
    if final:
        acc = _rms(acc, fg_ref[...])
    out_ref[...] = acc


def _outproj(ys, x2d, w_out, layer, final_g, tm, final):
    n = x2d.shape[0]
    kern = functools.partial(_outproj_kernel, final=final)
    yspec = pl.BlockSpec((tm, GROUP_W), lambda i: (i, 0))
    return pl.pallas_call(
        kern,
        grid=(n // tm,),
        in_specs=[yspec, yspec, yspec, yspec,
                  pl.BlockSpec((tm, D_MODEL), lambda i: (i, 0)),
                  _layer_spec(w_out, layer), _const_spec((1, D_MODEL))],
        out_specs=pl.BlockSpec((tm, D_MODEL), lambda i: (i, 0)),
        out_shape=jax.ShapeDtypeStruct((n, D_MODEL), F32),
        compiler_params=_cparams(("parallel",)),
        name="outproj",
    )(*ys, x2d, w_out, final_g)


def _col_form(row, n):
    return jnp.broadcast_to(row, (LANES, n)).T


N_DEC_STATES = 6


DEC_STATE_BATCH = 4


def _dec_state_kernel(*refs):
    params = refs[11:25]
    for bb in range(DEC_STATE_BATCH):
        one = lambda r: r.at[pl.ds(bb, 1)]
        _dec_state_one([one(r) for r in refs[0:5]], [one(r.at[0]) for r in refs[5:5 + N_DEC_STATES]], params,
                       [one(r) for r in refs[-8:-6]], [one(r.at[0]) for r in refs[-N_DEC_STATES:]])


def _dec_state_one(tokens, states, params, ys, new_states):
    xbc_ref, xm_ref, o_ref, z_ref, misc_ref = tokens
    hs_ref, sbuf_ref, cm_ref, nm_ref, mm_ref, mbuf_ref = states
    (cwa_ref, cba_ref, dtb_ref, alog_ref, hexp_ref, dskip_ref, nga_ref,
     cwb_ref, cbb_ref, wq_ref, wk_ref, wv_ref, gb_ref, ngb_ref) = params
    ya_ref, yb_ref = ys
    hs_out, sbuf_out, cm_out, nm_out, mm_out, mbuf_out = new_states
    misc = misc_ref[0]
    z = z_ref[0]

    u = xbc_ref[0]
    buf = sbuf_ref[0]
    cw = cwa_ref[...]
    conv = cba_ref[...] + cw[3:4] * u
    for j in range(CONV_W - 1):
        conv = conv + cw[j:j + 1] * buf[j:j + 1]
    sbuf_out[0, 0:CONV_W - 2, :] = buf[1:CONV_W - 1]
    sbuf_out[0, CONV_W - 2:CONV_W - 1, :] = u
    xbcs = _silu(conv)
    xa = xbcs[:, 0:GROUP_W]
    dt = _softplus(misc + dtb_ref[...])
    a = -jnp.exp(alog_ref[...])
    hexp = hexp_ref[...]
    dt8 = jnp.broadcast_to(dt, (SUBLANES, LANES))
    da8 = jnp.broadcast_to(dt * a, (SUBLANES, LANES))
    dt_x = jnp.dot(dt8, hexp, precision=HIGHEST, preferred_element_type=F32)[0:1]
    decay_x = jnp.exp(jnp.dot(da8, hexp, precision=HIGHEST, preferred_element_type=F32)[0:1])
    hstate = hs_ref[0].reshape(SSD_HEADS * SSD_HEAD_DIM, SSD_STATE)
    gw = GROUP_W // SSD_GROUPS
    c_off = GROUP_W + SSD_GROUPS * SSD_STATE
    dtx_col = _col_form(dt_x * xa, GROUP_W)
    decay_col = _col_form(decay_x, GROUP_W)
    y_parts = []
    for g in range(SSD_GROUPS):
        b_g = xbcs[:, GROUP_W + g * SSD_STATE:GROUP_W + (g + 1) * SSD_STATE]
        c_g = xbcs[:, c_off + g * SSD_STATE:c_off + (g + 1) * SSD_STATE]
        cb = jnp.sum(c_g * b_g, axis=1, keepdims=True)
        rows = slice(g * gw, (g + 1) * gw)
        h_g = hstate[rows]
        c8 = jnp.broadcast_to(c_g, (SUBLANES, SSD_STATE)).astype(BF16)
        ch = _dot_nt(c8, h_g.astype(BF16))[0:1]
        y_parts.append(cb * dt_x[:, rows] * xa[:, rows] + ch * decay_x[:, rows])
        h_new = decay_col[rows] * h_g + dtx_col[rows] * b_g
        hs_out[0, g * SSD_HPG:(g + 1) * SSD_HPG] = h_new.reshape(SSD_HPG, SSD_HEAD_DIM, SSD_STATE)
    y = jnp.concatenate(y_parts, axis=1) + dskip_ref[...] * xa
    y = y * _silu(z[:, 0:GROUP_W])
    nga = nga_ref[...]
    for g in range(SSD_GROUPS):
        rows = slice(g * gw, (g + 1) * gw)
        ya_ref[0, :, rows] = _rms(y[:, rows], nga[:, rows])

    um = xm_ref[0]
    mbuf = mbuf_ref[0]
    cwb = cwb_ref[...]
    convb = cbb_ref[...] + cwb[3:4] * um
    for j in range(CONV_W - 1):
        convb = convb + cwb[j:j + 1] * mbuf[j:j + 1]
    mbuf_out[0, 0:CONV_W - 2, :] = mbuf[1:CONV_W - 1]
    mbuf_out[0, CONV_W - 2:CONV_W - 1, :] = um
    xmc = _silu(convb)
    gates = misc + gb_ref[...]
    lf_all = -_softplus(-gates)
    mm = mm_ref[0]
    ngb = ngb_ref[...]
    o_all = o_ref[0]
    for h in range(ML_HEADS):
        sl = slice(h * ML_HEAD_DIM, (h + 1) * ML_HEAD_DIM)
        li = gates[:, MISC_I + h:MISC_I + h + 1]
        lf = lf_all[:, MISC_F + h:MISC_F + h + 1]
        m_prev = mm[:, h:h + 1]
        l_prev = lf + m_prev
        m_t = jnp.maximum(l_prev, li)
        xmc8 = jnp.broadcast_to(xmc[:, sl], (SUBLANES, ML_HEAD_DIM)).astype(BF16)
        um8 = jnp.broadcast_to(um[:, sl], (SUBLANES, ML_HEAD_DIM)).astype(BF16)
        q8 = _dot(xmc8, wq_ref[h])
        q = q8[0:1]
        k = _dot(xmc8, wk_ref[h])[0:1] * (ML_HEAD_DIM ** -0.5)
        v = _dot(um8, wv_ref[h])[0:1]
        qk = jnp.sum(q * k, axis=1, keepdims=True)
        sc = qk * jnp.exp(li - m_t)
        g_prev = jnp.exp(l_prev - m_t)
        cs = cm_ref[0, h]
        ns = nm_ref[0, h:h + 1, :]
        cq = _dot_nt(q8.astype(BF16), cs.astype(BF16))[0:1]
        num = sc * v + g_prev * cq
        den = sc + g_prev * jnp.sum(ns * q, axis=1, keepdims=True)
        hout = num / jnp.maximum(jnp.abs(den), jnp.exp(-m_t))
        wend = jnp.exp(li - m_t)
        decay = jnp.exp(l_prev - m_t)
        cm_out[0, h] = decay * cs + _col_form(wend * v, ML_HEAD_DIM) * k
        nm_out[0, h:h + 1, :] = decay * ns + wend * k
        mm_out[0, :, h:h + 1] = m_t
        hb = jax.nn.sigmoid(o_all[:, sl]) * hout
        yb_ref[0, :, sl] = _rms(hb, ngb[:, sl]) * _silu(z[:, GROUP_W + h * ML_HEAD_DIM:GROUP_W + (h + 1) * ML_HEAD_DIM])


def _dec_state(xbc, xm, o, z, misc, states, layer, lp):
    b = xbc.shape[0]
    nb = DEC_STATE_BATCH
    tok = lambda w: pl.BlockSpec((nb, 1, w), lambda bi: (bi, 0, 0))
    at_layer = lambda t: pl.BlockSpec((1, nb) + t.shape[2:], lambda bi: (layer, bi) + (0,) * (t.ndim - 2))
    state_specs = [at_layer(t) for t in states]
    pa, pa_specs = _layer_args(lp, layer, ("conv_a_w", "conv_a_b", "dtb_row", "alog_row"))
    pb, pb_specs = _layer_args(lp, layer, ("dskip_row", "ssd_norm_g", "conv_b_w", "conv_b_b", "w_mq", "w_mk", "w_mv",
                                           "gb_row", "ml_norm_g"))
    params = pa + [lp["head_expand"]] + pb
    param_specs = pa_specs + [_const_spec((LANES, GROUP_W))] + pb_specs
    inputs = [xbc, xm, o, z, misc] + list(states) + params
    in_specs = [tok(SSD_CONV_CH), tok(GROUP_W), tok(GROUP_W), tok(D_MIX), tok(MISC_W)] + state_specs + param_specs
    aliases = {5 + i: 2 + i for i in range(N_DEC_STATES)}
    outs = pl.pallas_call(
        _dec_state_kernel,
        grid=(b // nb,),
        in_specs=in_specs,
        out_specs=[tok(GROUP_W), tok(GROUP_W)] + state_specs,
        out_shape=[jax.ShapeDtypeStruct((b, 1, GROUP_W), F32)] * 2
        + [jax.ShapeDtypeStruct(t.shape, F32) for t in states],
        input_output_aliases=aliases,
        compiler_params=_cparams(("parallel",)),
        name="dec_state",
    )(*inputs)
    return outs[0], outs[1], tuple(outs[2:])


def _mla_decode_kernel(pt_ref, lat_hbm, kr_hbm, qn_ref, qr_ref, qfull_ref, kfull_ref, vnew_ref, zc_ref,
                       wukt_ref, wuv_ref, kng_ref, out_ref,
                       lat_buf, kr_buf, sem, qa_scr, m_scr, l_scr, acc_scr, *, layer, pps, page, nsub):
    b = pl.program_id(0)
    s = pl.program_id(1)
    ns = pl.num_programs(1)
    nb = pl.num_programs(0)
    step = b * ns + s
    last = nb * ns - 1
    slot = step % DECODE_SLOTS
    sub = pps * page // nsub

    def copies(bb, ss, sl):
        out = []
        for p in range(pps):
            pg = pt_ref[bb, ss * pps + p]
            out.append(pltpu.make_async_copy(lat_hbm.at[layer, pg], lat_buf.at[sl, pl.ds(p * page, page), :],
                                             sem.at[0, sl]))
            out.append(pltpu.make_async_copy(kr_hbm.at[layer, pg], kr_buf.at[sl, :, pl.ds(p * page, page)],
                                             sem.at[1, sl]))
        return out

    def start_step(target, sl):
        tgt = jnp.minimum(target, last)
        for cp in copies(tgt // ns, tgt % ns, sl):
            cp.start()

    @pl.when(step == 0)
    def _():
        start_step(step, 0)
        start_step(step + 1, 1)

    scale = MLA_QK ** -0.5

    @pl.when(s == 0)
    def _():
        qng = (qn_ref[0].astype(F32) * kng_ref[...]).astype(BF16)
        for h in range(MLA_HEADS):
            full = _dot(qng, wukt_ref[h * MLA_NOPE:(h + 1) * MLA_NOPE, :])
            qa_scr[h:h + 1, :] = full[h:h + 1]
            qa_scr[MLA_HEADS + h:MLA_HEADS + h + 1, :] = full[h:h + 1]
        m_scr[...] = jnp.full_like(m_scr, -jnp.inf)
        l_scr[...] = jnp.zeros_like(l_scr)
        acc_scr[...] = jnp.zeros_like(acc_scr)

    for cp in copies(b, s, slot):
        cp.wait()

    qa = qa_scr[...].astype(BF16)
    qr = qr_ref[0]
    wukt = wukt_ref[...]

    def scores(i):
        latb = lat_buf[slot, i * sub:(i + 1) * sub, :].astype(BF16)
        krb = kr_buf[slot, :, i * sub:(i + 1) * sub].astype(BF16)
        kraw_t = _dot_nt(wukt, latb)
        sa = _dot_nt(qa, latb)
        sr = _dot(qr, krb)
        rinv_rows = []
        for h in range(MLA_HEADS):
            kh = kraw_t[h * MLA_NOPE:(h + 1) * MLA_NOPE, :]
            ssq = jnp.sum(kh * kh, axis=0, keepdims=True)
            rinv_rows.append(lax.rsqrt(ssq * (1.0 / MLA_NOPE) + EPS))
        rinv = jnp.concatenate(rinv_rows + rinv_rows, axis=0)
        return latb, (sa * rinv + sr) * scale

    m = m_scr[...]
    l = l_scr[...]
    acc = acc_scr[...]
    pending = scores(0)
    for i in range(nsub):
        latb, sc = pending
        if i + 1 < nsub:
            pending = scores(i + 1)
        m_new = jnp.maximum(m, jnp.max(sc, axis=1, keepdims=True))
        alpha = jnp.exp(m - m_new)
        p = jnp.exp(sc - m_new)
        l = alpha * l + jnp.sum(p, axis=1, keepdims=True)
        acc = alpha * acc + _dot(p.astype(BF16), latb)
        m = m_new
    m_scr[...] = m
    l_scr[...] = l
    acc_scr[...] = acc

    start_step(step + 2, (step + 2) % DECODE_SLOTS)

    @pl.when(s == ns - 1)
    def _():
        s_new = jnp.sum(qfull_ref[0].astype(F32) * kfull_ref[0].astype(F32), axis=1, keepdims=True) * scale
        m_fin = jnp.maximum(m, s_new)
        alpha2 = jnp.exp(m - m_fin)
        p_new = jnp.exp(s_new - m_fin)
        l_fin = alpha2 * l + p_new
        acc_fin = alpha2 * acc + p_new * vnew_ref[0].astype(F32)
        ctx = (acc_fin / l_fin).astype(BF16)
        yfull = _dot(ctx, wuv_ref[...])
        parts = [yfull[h:h + 1, h * MLA_V:(h + 1) * MLA_V] for h in range(MLA_HEADS)]
        out_ref[0] = jnp.concatenate(parts, axis=1) * _silu(zc_ref[0])

    @pl.when(step == last)
    def _():
        for ahead in (1, 2):
            for cp in copies(b, s, (step + ahead) % DECODE_SLOTS):
                cp.wait()


DECODE_PAGES_PER_STEP = 32
DECODE_SUB_BLOCKS = 2
DECODE_SLOTS = 3


def _mla_decode(page_table, cache_lat, cache_kr, layer, qn8, qr8, qfull8, kfull8, vnew, z, lp,
                pps=DECODE_PAGES_PER_STEP, nsub=DECODE_SUB_BLOCKS):
    b, n_pages = page_table.shape
    page = cache_lat.shape[2]
    ns = n_pages // pps
    t = pps * page
    kern = functools.partial(_mla_decode_kernel, layer=layer, pps=pps, page=page, nsub=nsub)
    per_b = lambda shape: pl.BlockSpec((1,) + shape, lambda bi, si, pt: (bi, 0, 0))
    params, param_specs = _layer_args(lp, layer, ("w_uk_t", "w_uv", "kn_nope_g"))
    grid_spec = pltpu.PrefetchScalarGridSpec(
        num_scalar_prefetch=1,
        grid=(b, ns),
        in_specs=[pl.BlockSpec(memory_space=pl.ANY), pl.BlockSpec(memory_space=pl.ANY),
                  per_b((SUBLANES, MLA_NOPE)), per_b((SUBLANES, MLA_ROPE)),
                  per_b((SUBLANES, MLA_QK)), per_b((SUBLANES, MLA_QK)), per_b((1, KV_RANK)),
                  pl.BlockSpec((1, 1, GROUP_W), lambda bi, si, pt: (bi, 0, 2)),
                  ] + param_specs,
        out_specs=pl.BlockSpec((1, 1, GROUP_W), lambda bi, si, pt: (bi, 0, 0)),
        scratch_shapes=[pltpu.VMEM((DECODE_SLOTS, t, KV_RANK), F32), pltpu.VMEM((DECODE_SLOTS, MLA_ROPE, t), F32),
                        pltpu.SemaphoreType.DMA((2, DECODE_SLOTS)),
                        pltpu.VMEM((SUBLANES, KV_RANK), F32),
                        pltpu.VMEM((SUBLANES, 1), F32), pltpu.VMEM((SUBLANES, 1), F32),
                        pltpu.VMEM((SUBLANES, KV_RANK), F32)],
    )
    return pl.pallas_call(
        kern,
        grid_spec=grid_spec,
        out_shape=jax.ShapeDtypeStruct((b, 1, GROUP_W), F32),
        compiler_params=_cparams(("arbitrary", "arbitrary")),
        name="mla_decode",
    )(page_table, cache_lat, cache_kr, qn8, qr8, qfull8, kfull8, vnew, z, *params)


def _pack_w_in(w_in):
    names = ("z", "xbc", "dt", "xm", "i", "f", "o", "cq", "ckv", "kr", "xq")
    offs = {}
    o = 0
    for nme, w in zip(names, IN_SIZES):
        offs[nme] = (o, o + w)
        o += w
    col = lambda nme: w_in[:, :, offs[nme][0]:offs[nme][1]].astype(BF16)
    pad = jnp.zeros(w_in.shape[:2] + (MISC_W - (MISC_F + ML_HEADS),), BF16)
    misc = jnp.concatenate([col("kr"), col("dt"), col("i"), col("f"), pad], axis=-1)
    return [misc if nme == "misc" else col(nme) for nme, _ in PACK_SEGS]


def _stack_params(p):
    depth = p["norm_g"].shape[0]
    row = lambda a: a[:, None, :]
    lp = {}
    for nme in ("norm_g", "conv_a_b", "ssd_norm_g", "conv_b_b", "ml_norm_g", "q_norm_g", "kv_norm_g", "qn_nope_g",
                "kn_nope_g", "xq_norm_g"):
        lp[nme] = row(p[nme])
    lp["conv_a_w"] = p["conv_a_w"]
    lp["conv_b_w"] = p["conv_b_w"]
    lp["dtb_row"] = _lane_pad(p["dt_bias"], MISC_DT)
    lp["alog_row"] = _lane_pad(p["a_log"], MISC_DT)
    lp["dtb_col"] = p["dt_bias"][:, :, None]
    lp["alog_col"] = p["a_log"][:, :, None]
    lp["dskip_row"] = row(jnp.repeat(p["d_skip"], SSD_HEAD_DIM, axis=1))
    lp["w_mq"] = p["w_mq"].astype(BF16)
    lp["w_mk"] = p["w_mk"].astype(BF16)
    lp["w_mv"] = p["w_mv"].astype(BF16)
    lp["w_mq_t"] = jnp.swapaxes(lp["w_mq"], 2, 3)
    lp["w_mv_t"] = jnp.swapaxes(lp["w_mv"], 2, 3)
    gate_b = jnp.concatenate([p["b_i"], p["b_f"]], axis=1)
    lp["gb_row"] = _lane_pad(gate_b, MISC_I)
    lp["gb_col"] = gate_b[:, :, None]
    w_uq = p["w_uq"]
    lp["w_uq_nope"] = w_uq[..., :MLA_NOPE].reshape(depth, Q_RANK, MLA_HEADS * MLA_NOPE).astype(BF16)
    rope_pad = lambda t: jnp.pad(t, [(0, 0)] * (t.ndim - 1) + [(0, LANES - MLA_ROPE)])
    lp["w_uq_rope"] = rope_pad(jnp.transpose(w_uq[..., MLA_NOPE:], (0, 2, 1, 3))).astype(BF16)
    w_uk = p["w_uk"].reshape(depth, KV_RANK, MLA_HEADS * MLA_NOPE)
    lp["w_uk"] = w_uk.astype(BF16)
    lp["w_uk_t"] = jnp.swapaxes(w_uk, 1, 2).astype(BF16)
    lp["w_uv"] = p["w_uv"].reshape(depth, KV_RANK, MLA_HEADS * MLA_V).astype(BF16)
    lp["w_uq_nope_t"] = jnp.swapaxes(lp["w_uq_nope"], 1, 2)
    lp["w_uq_rope_t"] = jnp.transpose(w_uq[..., MLA_NOPE:], (0, 2, 3, 1)).astype(BF16)
    lp["qn_nope_g_col"] = p["qn_nope_g"][:, :, None]
    lp["qn_rope_g_col"] = p["qn_rope_g"][:, :, None]
    lp["qn_rope_g_pad"] = rope_pad(row(p["qn_rope_g"]))
    lp["kn_rope_g_pad"] = rope_pad(row(p["kn_rope_g"]))
    lp["w_out"] = p["w_out"].astype(BF16)
    head_of_lane = jnp.arange(LANES)[:, None] - MISC_DT
    head_of_chan = jnp.arange(GROUP_W)[None, :] // SSD_HEAD_DIM
    lp["head_expand"] = (head_of_lane == head_of_chan).astype(F32)
    return lp


def _rope_tables(pos):
    half = MLA_ROPE // 2
    inv = ROPE_BASE ** (-jnp.arange(half, dtype=F32) * 2.0 / MLA_ROPE)
    ang = pos.astype(F32)[:, None] * inv[None, :]
    cos = jnp.cos(ang)
    sin = jnp.sin(ang)
    zero = jnp.zeros((pos.shape[0], LANES - MLA_ROPE), F32)
    cos2 = jnp.concatenate([cos, cos], axis=1)
    sinsgn = jnp.concatenate([-sin, sin], axis=1)
    return jnp.concatenate([cos2, zero], axis=1), jnp.concatenate([sinsgn, zero], axis=1), cos2.T, sinsgn.T


def _prompt_layer(lp, w_slabs, layer, x, mk_all, mv_all, rope_tabs, final_g, final):
    b, l, _ = x.shape
    n = b * l
    x2d = x.reshape(n, D_MODEL)
    z, xbc, xm, o, cq, ckv, xq, misc = _inproj(x2d, lp["norm_g"], w_slabs, layer, tm=512)
    r3 = lambda t: t.reshape(b, l, t.shape[-1])
    z, xbc, xm, o, cq, ckv, xq, misc = map(r3, (z, xbc, xm, o, cq, ckv, xq, misc))
    ya, ssm_h, ssm_buf, yb, ml_c, ml_n, ml_m, ml_buf = _scans_prompt(xbc, xm, o, z, misc, lp, layer)
    qt, k, latt, lat, kr = _mla_prep(cq, ckv, misc, rope_tabs, lp, layer, tm=256, transposed=True)
    yc = _mla_attn_prompt(qt, k, latt, z, lp["w_uv"], layer)
    yd = _xattn(xq, mk_all, mv_all, layer, z, lp["xq_norm_g"], tm=512)
    ys = [t.reshape(n, GROUP_W) for t in (ya, yb, yc, yd)]
    x_new = _outproj(ys, x2d, lp["w_out"], layer, final_g, tm=512, final=final).reshape(b, l, D_MODEL)
    return x_new, (lat, kr, ssm_h, ssm_buf, ml_c, ml_n, ml_m, ml_buf)


def _sample_layer(lp, w_slabs, layer, x, mem_k, mem_v, cache_lat, cache_kr, page_table, states,
                  rope_tabs, final_g, final, pps=DECODE_PAGES_PER_STEP):
    b = x.shape[0]
    x2d = x.reshape(b, D_MODEL)
    z, xbc, xm, o, cq, ckv, xq, misc = _inproj(x2d, lp["norm_g"], w_slabs, layer, tm=b)
    tok = lambda t: t.reshape(b, 1, t.shape[-1])
    ya, yb, st_new = _dec_state(tok(xbc), tok(xm), tok(o), tok(z), tok(misc), states, layer, lp)
    one = lambda t: t.reshape(1, b, t.shape[-1])
    q, k, latb, lat, kr = _mla_prep(one(cq), one(ckv), one(misc), rope_tabs, lp, layer, tm=b, transposed=False)
    qh = jnp.transpose(q[0], (1, 0, 2))
    kh = jnp.transpose(k[0], (1, 0, 2))
    dup = lambda t: jnp.concatenate([t, t], axis=1)
    qfull8 = dup(qh)
    kfull8 = dup(kh)
    qn8 = qfull8[:, :, :MLA_NOPE]
    qr8 = qfull8[:, :, MLA_NOPE:]
    vnew = latb.reshape(b, 1, KV_RANK)
    yc = _mla_decode(page_table, cache_lat, cache_kr, layer, qn8, qr8, qfull8, kfull8, vnew, tok(z), lp, pps=pps)
    yd = _xattn(tok(xq), mem_k, mem_v, layer, tok(z), lp["xq_norm_g"], tm=1)
    ys = [t.reshape(b, GROUP_W) for t in (ya, yb, yc, yd)]
    x_new = _outproj(ys, x2d, lp["w_out"], layer, final_g, tm=b, final=final).reshape(b, 1, D_MODEL)
    return x_new, lat.reshape(b, 1, KV_RANK), kr.reshape(b, 1, MLA_ROPE), st_new


def kernel(x_prompt, x_sample, mem_prompt, cache_mla_latent, cache_mla_krope, cache_mem_k, cache_mem_v, state_ssm, state_ssm_conv, state_mlstm_C, state_mlstm_n, state_mlstm_m, state_mlstm_conv, page_table, norm_g, w_in, conv_a_w, conv_a_b, dt_bias, a_log, d_skip, ssd_norm_g, conv_b_w, conv_b_b, w_mq, w_mk, w_mv, b_i, b_f, ml_norm_g, q_norm_g, kv_norm_g, w_uq, w_uk, w_uv, qn_nope_g, qn_rope_g, kn_nope_g, kn_rope_g, mem_norm_g, w_mem_k, w_mem_v, xk_norm_g, xq_norm_g, w_out, final_norm_g):
    params = dict(norm_g=norm_g, conv_a_w=conv_a_w, conv_a_b=conv_a_b, dt_bias=dt_bias, a_log=a_log, d_skip=d_skip,
                  ssd_norm_g=ssd_norm_g, conv_b_w=conv_b_w, conv_b_b=conv_b_b, w_mq=w_mq, w_mk=w_mk, w_mv=w_mv,
                  b_i=b_i, b_f=b_f, ml_norm_g=ml_norm_g, q_norm_g=q_norm_g, kv_norm_g=kv_norm_g, w_uq=w_uq,
                  w_uk=w_uk, w_uv=w_uv, qn_nope_g=qn_nope_g, qn_rope_g=qn_rope_g, kn_nope_g=kn_nope_g,
                  kn_rope_g=kn_rope_g, xq_norm_g=xq_norm_g, w_out=w_out)
    depth = w_in.shape[0]
    bp, lp_len, _ = x_prompt.shape
    bs, ls_len, _ = x_sample.shape
    n_pages = page_table.shape[1]
    past_len = n_pages * cache_mla_latent.shape[2]
    mem_len = mem_prompt.shape[1]
    w_slabs = _pack_w_in(w_in)
    final_g = final_norm_g[None, :]

    cache_kr_t = jnp.swapaxes(cache_mla_krope, 2, 3)
    rope_p = _rope_tables(jnp.arange(lp_len, dtype=jnp.int32))
    rope_s = _rope_tables(jnp.full((bs,), past_len, dtype=jnp.int32))

    mk_all, mv_all = _memkv(mem_prompt, mem_norm_g[:, None, :], w_mem_k.astype(BF16), w_mem_v.astype(BF16),
                            xk_norm_g[:, None, :])

    mem_rows = cache_mem_k.shape[2] * X_HEADS
    cache_mk = cache_mem_k.reshape(depth, bs, mem_rows, X_HEAD_DIM)
    cache_mv = cache_mem_v.reshape(depth, bs, mem_rows, X_HEAD_DIM)
    states = (state_ssm, state_ssm_conv, state_mlstm_C, state_mlstm_n,
              state_mlstm_m.reshape(depth, bs, 1, ML_HEADS), state_mlstm_conv)

    hp, hs = x_prompt, x_sample
    p_states, s_lats, s_krs = [], [], []
    lp = _stack_params(params)
    for l in range(depth):
        final = l == depth - 1
        hp, st_p = _prompt_layer(lp, w_slabs, l, hp, mk_all, mv_all, rope_p, final_g, final)
        p_states.append(st_p)
        hs, s_lat_l, s_kr_l, states = _sample_layer(lp, w_slabs, l, hs, cache_mk, cache_mv, cache_mla_latent,
                                                    cache_kr_t, page_table, states, rope_s, final_g, final)
        s_lats.append(s_lat_l)
        s_krs.append(s_kr_l)

    p_out = [jnp.stack([st[i] for st in p_states]) for i in range(8)]
    p_mem_k = mk_all.reshape(depth, bp, mem_len, X_HEADS, X_HEAD_DIM)
    p_mem_v = mv_all.reshape(depth, bp, mem_len, X_HEADS, X_HEAD_DIM)
    (p_lat, p_kr, p_ssm, p_sconv, p_mc, p_mn, p_mm, p_mconv) = p_out
    s_ssm, s_sconv, s_mc, s_mn, s_mm, s_mconv = states
    return (hp, hs, p_lat, p_kr, p_mem_k, p_mem_v, p_ssm, p_sconv, p_mc, p_mn, p_mm, p_mconv,
            jnp.stack(s_lats), jnp.stack(s_krs), s_ssm, s_sconv, s_mc, s_mn, s_mm.reshape(depth, bs, ML_HEADS),
            s_mconv)
```

```python
import functools
import math

import jax
import jax.numpy as jnp
from jax import lax
from jax.experimental import pallas as pl
from jax.experimental.pallas import tpu as pltpu

F32 = jnp.float32
BF16 = jnp.bfloat16
HIGHEST = lax.Precision.HIGHEST

D_MODEL = 1024
D_MIX = 2 * D_MODEL
GROUP_W = D_MIX // 4
SSD_HEAD_DIM = 64
SSD_HEADS = GROUP_W // SSD_HEAD_DIM
SSD_GROUPS = 2
SSD_HPG = SSD_HEADS // SSD_GROUPS
SSD_STATE = 128
CONV_W = 4
SSD_CONV_CH = GROUP_W + 2 * SSD_GROUPS * SSD_STATE
ML_HEADS = 4
ML_HEAD_DIM = GROUP_W // ML_HEADS
MLA_HEADS = 4
MLA_NOPE = 128
MLA_ROPE = 64
MLA_QK = MLA_NOPE + MLA_ROPE
MLA_V = GROUP_W // MLA_HEADS
Q_RANK = 3 * D_MODEL // 8
KV_RANK = D_MODEL // 4
ROPE_BASE = 10000.0
X_HEADS = 4
X_HEAD_DIM = GROUP_W // X_HEADS
CHUNK = 256
EPS = 1e-6
IN_SIZES = (D_MIX, SSD_CONV_CH, SSD_HEADS, GROUP_W, ML_HEADS, ML_HEADS, GROUP_W, Q_RANK, KV_RANK, MLA_ROPE, GROUP_W)

LANES = 128
SUBLANES = 8
VMEM_LIMIT = 56 * 1024 * 1024

MISC_W = LANES
MISC_KR = 0
MISC_DT = MLA_ROPE
MISC_I = MISC_DT + SSD_HEADS
MISC_F = MISC_I + ML_HEADS
PACK_SEGS = (("z", D_MIX), ("xbc", SSD_CONV_CH), ("xm", GROUP_W), ("o", GROUP_W), ("cq", Q_RANK),
             ("ckv", KV_RANK), ("xq", GROUP_W), ("misc", MISC_W))
PACK_W = sum(w for _, w in PACK_SEGS)


def _cparams(sem):
    return pltpu.CompilerParams(dimension_semantics=sem, vmem_limit_bytes=VMEM_LIMIT)


def _dot(a, b):
    return jnp.dot(a, b, preferred_element_type=F32)


def _dot_nt(a, b, precision=None):
    return lax.dot_general(a, b, (((1,), (1,)), ((), ())), preferred_element_type=F32, precision=precision)


def _dot_tn(a, b):
    return lax.dot_general(a, b, (((0,), (0,)), ((), ())), preferred_element_type=F32)


def _rms(x, g):
    return x * lax.rsqrt(jnp.mean(x * x, axis=-1, keepdims=True) + EPS) * g


def _silu(x):
    return x * jax.nn.sigmoid(x)


def _softplus(x):
    return jnp.maximum(x, 0.0) + jnp.log1p(jnp.exp(-jnp.abs(x)))


def _const_spec(shape):
    nd = len(shape)
    return pl.BlockSpec(shape, lambda *_: (0,) * nd)


def _layer_spec(arr, layer):
    nd = arr.ndim
    return pl.BlockSpec((None,) + arr.shape[1:], lambda *_: (layer,) + (0,) * (nd - 1))


def _layer_args(lp, layer, names):
    return [lp[n] for n in names], [_layer_spec(lp[n], layer) for n in names]


def _inproj_kernel(x_ref, g_ref, *refs):
    nseg = len(PACK_SEGS)
    w_refs, out_refs = refs[:nseg], refs[nseg:]
    x = x_ref[...]
    u = _rms(x, g_ref[...]).astype(BF16)
    for w_ref, out_ref in zip(w_refs, out_refs):
        out_ref[...] = _dot(u, w_ref[0])


def _inproj(x2d, g, w_slabs, layer, tm):
    n = x2d.shape[0]
    out_shape = [jax.ShapeDtypeStruct((n, w), F32) for _, w in PACK_SEGS]
    out_specs = [pl.BlockSpec((tm, w), lambda i: (i, 0)) for _, w in PACK_SEGS]
    w_specs = [pl.BlockSpec((1, D_MODEL, w), lambda i: (layer, 0, 0), pipeline_mode=pl.Buffered(1))
               for _, w in PACK_SEGS]
    return pl.pallas_call(
        _inproj_kernel,
        grid=(n // tm,),
        in_specs=[pl.BlockSpec((tm, D_MODEL), lambda i: (i, 0)), _layer_spec(g, layer)] + w_specs,
        out_specs=out_specs,
        out_shape=out_shape,
        compiler_params=_cparams(("parallel",)),
        name="inproj",
    )(x2d, g, *w_slabs)


def _tri_masks():
    r = lax.broadcasted_iota(jnp.int32, (CHUNK, CHUNK), 0)
    c = lax.broadcasted_iota(jnp.int32, (CHUNK, CHUNK), 1)
    return r >= c, c >= r


def _ssd_chunk(xbc_ref, misc_ref, za_ref, cw_ref, cb_ref, dtb_row_ref, alog_row_ref, dtb_col_ref,
               alog_col_ref, dskip_ref, ng_ref, ya_ref, h_scr, ext_scr, y_scr):
    u = xbc_ref[0]
    ext_scr[SUBLANES:SUBLANES + CHUNK, :] = u
    cw = cw_ref[...]
    conv = cb_ref[...] + cw[3:4] * u
    for j in range(CONV_W - 1):
        back = CONV_W - 1 - j
        conv = conv + cw[j:j + 1] * ext_scr[SUBLANES - back:SUBLANES - back + CHUNK, :]
    ext_scr[0:SUBLANES, :] = u[CHUNK - SUBLANES:CHUNK, :]
    xbcs = _silu(conv)
    xa = xbcs[:, 0:GROUP_W]

    misc = misc_ref[0]
    misc_t = misc.T
    lower, upper = _tri_masks()
    lower_f = lower.astype(F32)
    upper_f = upper.astype(F32)
    dt_col = _softplus(misc + dtb_row_ref[...])
    dt_row = _softplus(misc_t[MISC_DT:MISC_DT + SSD_HEADS, :] + dtb_col_ref[...])
    a_row = -jnp.exp(alog_row_ref[...])
    a_col = -jnp.exp(alog_col_ref[...])
    acum_col = jnp.dot(lower_f, dt_col * a_row, precision=HIGHEST, preferred_element_type=F32)
    acum_row = jnp.dot(dt_row * a_col, upper_f, precision=HIGHEST, preferred_element_type=F32)

    for g in range(SSD_GROUPS):
        b_g = xbcs[:, GROUP_W + g * SSD_STATE:GROUP_W + (g + 1) * SSD_STATE].astype(BF16)
        c_off = GROUP_W + SSD_GROUPS * SSD_STATE
        c_g = xbcs[:, c_off + g * SSD_STATE:c_off + (g + 1) * SSD_STATE].astype(BF16)
        cb = _dot_nt(c_g, b_g)
        for hh in range(SSD_HPG):
            h = g * SSD_HPG + hh
            ac_col = acum_col[:, MISC_DT + h:MISC_DT + h + 1]
            ac_row = acum_row[h:h + 1, :]
            seg = jnp.where(lower, ac_col - ac_row, -jnp.inf)
            w = cb * jnp.exp(seg) * dt_row[h:h + 1, :]
            x_h = xa[:, h * SSD_HEAD_DIM:(h + 1) * SSD_HEAD_DIM]
            hs = h_scr[h]
            y_h = _dot(w.astype(BF16), x_h.astype(BF16)) + _dot_nt(c_g, hs.astype(BF16)) * jnp.exp(ac_col)
            y_scr[:, h * SSD_HEAD_DIM:(h + 1) * SSD_HEAD_DIM] = y_h
            ac_last = ac_col[CHUNK - 1:CHUNK, :]
            wend = jnp.exp(ac_last - ac_col) * dt_col[:, MISC_DT + h:MISC_DT + h + 1]
            xw = (x_h * wend).astype(BF16)
            h_scr[h] = jnp.exp(ac_last) * hs + _dot_tn(xw, b_g)

    y = y_scr[...] + dskip_ref[...] * xa
    y = y * _silu(za_ref[0])
    gw = GROUP_W // SSD_GROUPS
    ng = ng_ref[...]
    for g in range(SSD_GROUPS):
        ya_ref[0, :, g * gw:(g + 1) * gw] = _rms(y[:, g * gw:(g + 1) * gw], ng[:, g * gw:(g + 1) * gw])


def _lane_pad(v, offset):
    return jnp.pad(v, ((0, 0), (offset, LANES - offset - v.shape[1])))[:, None, :]


def _mlstm_chunk(xm_ref, o_ref, zb_ref, misc_ref, cw_ref, cb_ref, wqt_ref, wk_ref, wvt_ref, gb_row_ref,
                 gb_col_ref, ng_ref, yb_ref, c_scr, n_scr, m_scr, ext_scr):
    u = xm_ref[0]
    ext_scr[SUBLANES:SUBLANES + CHUNK, :] = u
    cw = cw_ref[...]
    conv = cb_ref[...] + cw[3:4] * u
    for j in range(CONV_W - 1):
        back = CONV_W - 1 - j
        conv = conv + cw[j:j + 1] * ext_scr[SUBLANES - back:SUBLANES - back + CHUNK, :]
    ext_scr[0:SUBLANES, :] = u[CHUNK - SUBLANES:CHUNK, :]
    xmc = _silu(conv)

    misc = misc_ref[0]
    misc_t = misc.T
    lower, upper = _tri_masks()
    gates_col = misc + gb_row_ref[...]
    gates_row = misc_t[MISC_I:MISC_I + 2 * ML_HEADS, :] + gb_col_ref[...]
    lf_col = -_softplus(-gates_col)
    lf_row = -_softplus(-gates_row)
    bcum_col = jnp.dot(lower.astype(F32), lf_col, precision=HIGHEST, preferred_element_type=F32)
    bcum_row = jnp.dot(lf_row, upper.astype(F32), precision=HIGHEST, preferred_element_type=F32)
    ng = ng_ref[...]
    o_all = o_ref[0]
    zb = zb_ref[0]
    m_all = m_scr[...]
    n_all = n_scr[...]
    m_rows, n_rows = [], []
    for h in range(ML_HEADS):
        sl = slice(h * ML_HEAD_DIM, (h + 1) * ML_HEAD_DIM)
        li_r = gates_row[h:h + 1, :]
        bc_r = bcum_row[ML_HEADS + h:ML_HEADS + h + 1, :]
        col = gates_col[:, MISC_I + h:MISC_I + h + 1] - bcum_col[:, MISC_F + h:MISC_F + h + 1]
        m_prev = m_all[h:h + 1, 0:1]
        lw = jnp.where(upper, bc_r + col, -jnp.inf)
        l_prev = bc_r + m_prev
        m_t = jnp.maximum(l_prev, jnp.max(lw, axis=0, keepdims=True))
        xmc_h = xmc[:, sl].astype(BF16)
        qtb = _dot_nt(wqt_ref[h], xmc_h).astype(BF16)
        k = _dot(xmc_h, wk_ref[h]) * (ML_HEAD_DIM ** -0.5)
        kb = k.astype(BF16)
        vt = _dot_nt(wvt_ref[h], u[:, sl].astype(BF16))
        sc = _dot(kb, qtb) * jnp.exp(lw - m_t)
        g_prev = jnp.exp(l_prev - m_t)
        cs = c_scr[h]
        ns = n_all[h:h + 1, :]
        ns8 = jnp.broadcast_to(ns, (SUBLANES, ML_HEAD_DIM)).astype(BF16)
        num = _dot(vt.astype(BF16), sc.astype(BF16)) + g_prev * _dot(cs.astype(BF16), qtb)
        den = jnp.sum(sc, axis=0, keepdims=True) + g_prev * _dot(ns8, qtb)[0:1]
        hout_t = num / jnp.maximum(jnp.abs(den), jnp.exp(-m_t))
        m_new = m_t[:, CHUNK - 1:CHUNK]
        bc_last = bc_r[:, CHUNK - 1:CHUNK]
        wend = jnp.exp(bc_last - bc_r + li_r - m_new)
        decay = jnp.exp(bc_last + m_prev - m_new)
        c_scr[h] = decay * cs + _dot((vt * wend).astype(BF16), kb)
        wend8 = jnp.broadcast_to(wend, (SUBLANES, CHUNK)).astype(BF16)
        n_rows.append(decay * ns + _dot(wend8, kb)[0:1])
        m_rows.append(jnp.broadcast_to(m_new, (1, LANES)))
        hb = jax.nn.sigmoid(o_all[:, sl]) * hout_t.T
        yb_ref[0, :, sl] = _rms(hb, ng[:, sl]) * _silu(zb[:, sl])
    pad = [jnp.zeros((SUBLANES - ML_HEADS, LANES), F32)]
    n_scr[...] = jnp.concatenate(n_rows + pad, axis=0)
    m_scr[...] = jnp.concatenate(m_rows + pad, axis=0)


SSD_PARAMS = ("conv_a_w", "conv_a_b", "dtb_row", "alog_row", "dtb_col", "alog_col", "dskip_row", "ssd_norm_g")
MLSTM_PARAMS = ("conv_b_w", "conv_b_b", "w_mq_t", "w_mk", "w_mv_t", "gb_row", "gb_col", "ml_norm_g")


def _scan_kernel(xbc_ref, misc_ref, za_ref, xm_ref, o_ref, zb_ref, *rest):
    ns, nm = len(SSD_PARAMS), len(MLSTM_PARAMS)
    ssd_params, ml_params = rest[:ns], rest[ns:ns + nm]
    ya_ref, hfin_ref, sconv_ref, yb_ref, cfin_ref, nfin_ref, mfin_ref, mconv_ref = rest[ns + nm:ns + nm + 8]
    h_scr, sext_scr, y_scr, c_scr, n_scr, m_scr, mext_scr = rest[ns + nm + 8:]
    c = pl.program_id(1)
    nc = pl.num_programs(1)

    @pl.when(c == 0)
    def _():
        h_scr[...] = jnp.zeros_like(h_scr)
        sext_scr[0:SUBLANES, :] = jnp.zeros((SUBLANES, SSD_CONV_CH), F32)
        c_scr[...] = jnp.zeros_like(c_scr)
        n_scr[...] = jnp.zeros_like(n_scr)
        m_scr[...] = jnp.zeros_like(m_scr)
        mext_scr[0:SUBLANES, :] = jnp.zeros((SUBLANES, GROUP_W), F32)

    _ssd_chunk(xbc_ref, misc_ref, za_ref, *ssd_params, ya_ref, h_scr, sext_scr, y_scr)
    _mlstm_chunk(xm_ref, o_ref, zb_ref, misc_ref, *ml_params, yb_ref, c_scr, n_scr, m_scr, mext_scr)

    @pl.when(c == nc - 1)
    def _():
        hfin_ref[0] = h_scr[...]
        sconv_ref[0] = xbc_ref[0, CHUNK - (CONV_W - 1):CHUNK, :]
        cfin_ref[0] = c_scr[...]
        nfin_ref[0] = n_scr[...]
        mfin_ref[0] = m_scr[...]
        mconv_ref[0] = xm_ref[0, CHUNK - (CONV_W - 1):CHUNK, :]


def _scans_prompt(xbc, xm, o, z, misc, lp, layer):
    b, l, _ = xbc.shape
    nc = l // CHUNK
    row = lambda bi, ci: (bi, ci, 0)
    first = lambda nd: (lambda bi, ci: (bi,) + (0,) * (nd - 1))
    tokens = lambda w: pl.BlockSpec((1, CHUNK, w), row)
    pa, pa_specs = _layer_args(lp, layer, SSD_PARAMS)
    pb, pb_specs = _layer_args(lp, layer, MLSTM_PARAMS)
    outs = pl.pallas_call(
        _scan_kernel,
        grid=(b, nc),
        in_specs=[tokens(SSD_CONV_CH), tokens(MISC_W),
                  pl.BlockSpec((1, CHUNK, GROUP_W), lambda bi, ci: (bi, ci, 0)),
                  tokens(GROUP_W), tokens(GROUP_W),
                  pl.BlockSpec((1, CHUNK, GROUP_W), lambda bi, ci: (bi, ci, 1)),
                  ] + pa_specs + pb_specs,
        out_specs=[tokens(GROUP_W),
                   pl.BlockSpec((1, SSD_HEADS, SSD_HEAD_DIM, SSD_STATE), first(4)),
                   pl.BlockSpec((1, CONV_W - 1, SSD_CONV_CH), first(3)),
                   tokens(GROUP_W),
                   pl.BlockSpec((1, ML_HEADS, ML_HEAD_DIM, ML_HEAD_DIM), first(4)),
                   pl.BlockSpec((1, SUBLANES, LANES), first(3)),
                   pl.BlockSpec((1, SUBLANES, LANES), first(3)),
                   pl.BlockSpec((1, CONV_W - 1, GROUP_W), first(3))],
        out_shape=[jax.ShapeDtypeStruct((b, l, GROUP_W), F32),
                   jax.ShapeDtypeStruct((b, SSD_HEADS, SSD_HEAD_DIM, SSD_STATE), F32),
                   jax.ShapeDtypeStruct((b, CONV_W - 1, SSD_CONV_CH), F32),
                   jax.ShapeDtypeStruct((b, l, GROUP_W), F32),
                   jax.ShapeDtypeStruct((b, ML_HEADS, ML_HEAD_DIM, ML_HEAD_DIM), F32),
                   jax.ShapeDtypeStruct((b, SUBLANES, LANES), F32),
                   jax.ShapeDtypeStruct((b, SUBLANES, LANES), F32),
                   jax.ShapeDtypeStruct((b, CONV_W - 1, GROUP_W), F32)],
        scratch_shapes=[pltpu.VMEM((SSD_HEADS, SSD_HEAD_DIM, SSD_STATE), F32),
                        pltpu.VMEM((SUBLANES + CHUNK, SSD_CONV_CH), F32),
                        pltpu.VMEM((CHUNK, GROUP_W), F32),
                        pltpu.VMEM((ML_HEADS, ML_HEAD_DIM, ML_HEAD_DIM), F32),
                        pltpu.VMEM((SUBLANES, LANES), F32),
                        pltpu.VMEM((SUBLANES, LANES), F32),
                        pltpu.VMEM((SUBLANES + CHUNK, GROUP_W), F32)],
        compiler_params=_cparams(("parallel", "arbitrary")),
        name="scans_prompt",
    )(xbc, misc, z, xm, o, z, *pa, *pb)
    ya, ssm_h, ssm_buf, yb, cfin, nfin, mfin, mconv = outs
    return ya, ssm_h, ssm_buf, yb, cfin, nfin[:, :ML_HEADS, :], mfin[:, :ML_HEADS, 0], mconv


def _rope_rows(r, g, cos2, sinsgn):
    half = MLA_ROPE // 2
    rn = r * lax.rsqrt(jnp.sum(r * r, axis=-1, keepdims=True) * (1.0 / MLA_ROPE) + EPS) * g
    lane = lax.broadcasted_iota(jnp.int32, rn.shape, 1)
    r_sw = jnp.where(lane < half, pltpu.roll(rn, LANES - half, 1), pltpu.roll(rn, half, 1))
    return rn * cos2 + r_sw * sinsgn


def _mla_prep_kernel(cq_ref, ckv_ref, misc_ref, cos_ref, sin_ref, cos_t_ref, sin_t_ref, qg_ref, kvg_ref, wqn_ref,
                     wqr_ref, wuk_ref, qnn_ref, qnr_ref, knn_ref, knr_ref, q_ref, k_ref, v_ref, lat_ref, kr_ref,
                     *, transposed):
    cos2 = cos_ref[...]
    sinsgn = sin_ref[...]
    c_q = _rms(cq_ref[0], qg_ref[...]).astype(BF16)
    if not transposed:
        qn_all = _dot(c_q, wqn_ref[...])
    c_kv = _rms(ckv_ref[0], kvg_ref[...])
    lat_ref[0] = c_kv
    c_kv_b = c_kv.astype(BF16)
    if transposed:
        v_ref[0] = c_kv.T.astype(BF16)
    else:
        v_ref[0] = c_kv_b
    misc = misc_ref[0]
    lane = lax.broadcasted_iota(jnp.int32, misc.shape, 1)
    kr_raw = jnp.where(lane < MLA_ROPE, misc, 0.0)
    kr = _rope_rows(kr_raw, knr_ref[...], cos2, sinsgn)
    kr_ref[0] = kr[:, 0:MLA_ROPE]
    kr_b = kr[:, 0:MLA_ROPE].astype(BF16)
    kn_all = _dot(c_kv_b, wuk_ref[...])
    half = MLA_ROPE // 2
    col_rms = lambda x, g: x * lax.rsqrt(jnp.mean(x * x, axis=0, keepdims=True) + EPS) * g
    for h in range(MLA_HEADS):
        sl = slice(h * MLA_NOPE, (h + 1) * MLA_NOPE)
        if transposed:
            qn_t = col_rms(_dot_nt(wqn_ref[sl, :], c_q), qnn_ref[...])
            qr_t = col_rms(_dot_nt(wqr_ref[h], c_q), qnr_ref[...])
            qr_sw = jnp.concatenate([qr_t[half:], qr_t[:half]], axis=0)
            qr_t = qr_t * cos_t_ref[...] + qr_sw * sin_t_ref[...]
            q_ref[0, h, 0:MLA_NOPE, :] = qn_t.astype(BF16)
            q_ref[0, h, MLA_NOPE:MLA_QK, :] = qr_t.astype(BF16)
        else:
            qn = _rms(qn_all[:, sl], qnn_ref[...])
            qr = _rope_rows(_dot(c_q, wqr_ref[h]), qnr_ref[...], cos2, sinsgn)
            q_ref[0, h, :, 0:MLA_NOPE] = qn.astype(BF16)
            q_ref[0, h, :, MLA_NOPE:MLA_QK] = qr[:, 0:MLA_ROPE].astype(BF16)
        k_ref[0, h, :, 0:MLA_NOPE] = _rms(kn_all[:, sl], knn_ref[...]).astype(BF16)
        k_ref[0, h, :, MLA_NOPE:MLA_QK] = kr_b


def _mla_prep(cq, ckv, misc, rope_tabs, lp, layer, tm, transposed):
    b, l, _ = cq.shape
    cos2, sinsgn, cos_t, sin_t = rope_tabs
    if transposed:
        names = ("q_norm_g", "kv_norm_g", "w_uq_nope_t", "w_uq_rope_t", "w_uk", "qn_nope_g_col", "qn_rope_g_col",
                 "kn_nope_g", "kn_rope_g_pad")
    else:
        names = ("q_norm_g", "kv_norm_g", "w_uq_nope", "w_uq_rope", "w_uk", "qn_nope_g", "qn_rope_g_pad",
                 "kn_nope_g", "kn_rope_g_pad")
    params, param_specs = _layer_args(lp, layer, names)
    row = lambda bi, ti: (bi, ti, 0)
    hrow = lambda bi, ti: (bi, 0, ti, 0)
    if transposed:
        q_spec = pl.BlockSpec((1, MLA_HEADS, MLA_QK, tm), lambda bi, ti: (bi, 0, 0, ti))
        q_shape = jax.ShapeDtypeStruct((b, MLA_HEADS, MLA_QK, l), BF16)
        v_spec = pl.BlockSpec((1, KV_RANK, tm), lambda bi, ti: (bi, 0, ti))
        v_shape = jax.ShapeDtypeStruct((b, KV_RANK, l), BF16)
    else:
        q_spec = pl.BlockSpec((1, MLA_HEADS, tm, MLA_QK), hrow)
        q_shape = jax.ShapeDtypeStruct((b, MLA_HEADS, l, MLA_QK), BF16)
        v_spec = pl.BlockSpec((1, tm, KV_RANK), row)
        v_shape = jax.ShapeDtypeStruct((b, l, KV_RANK), BF16)
    return pl.pallas_call(
        functools.partial(_mla_prep_kernel, transposed=transposed),
        grid=(b, l // tm),
        in_specs=[pl.BlockSpec((1, tm, Q_RANK), row), pl.BlockSpec((1, tm, KV_RANK), row),
                  pl.BlockSpec((1, tm, MISC_W), row),
                  pl.BlockSpec((tm, LANES), lambda bi, ti: (ti, 0)),
                  pl.BlockSpec((tm, LANES), lambda bi, ti: (ti, 0)),
                  pl.BlockSpec((MLA_ROPE, tm), lambda bi, ti: (0, ti)),
                  pl.BlockSpec((MLA_ROPE, tm), lambda bi, ti: (0, ti))] + param_specs,
        out_specs=[q_spec,
                   pl.BlockSpec((1, MLA_HEADS, tm, MLA_QK), hrow),
                   v_spec,
                   pl.BlockSpec((1, tm, KV_RANK), row),
                   pl.BlockSpec((1, tm, MLA_ROPE), row)],
        out_shape=[q_shape,
                   jax.ShapeDtypeStruct((b, MLA_HEADS, l, MLA_QK), BF16),
                   v_shape,
                   jax.ShapeDtypeStruct((b, l, KV_RANK), F32),
                   jax.ShapeDtypeStruct((b, l, MLA_ROPE), F32)],
        compiler_params=_cparams(("parallel", "parallel")),
        name="mla_prep",
    )(cq, ckv, misc, cos2, sinsgn, cos_t, sin_t, *params)


def _mla_attn_kernel(qt_ref, k_ref, vt_ref, zc_ref, wuv_ref, out_ref, m_scr, l_scr, acc_scr, *, tile):
    qi = pl.program_id(1)
    scale = MLA_QK ** -0.5
    m_scr[...] = jnp.full_like(m_scr, -jnp.inf)
    l_scr[...] = jnp.zeros_like(l_scr)
    acc_scr[...] = jnp.zeros_like(acc_scr)

    def kv_block(start, width, diagonal):
        vt = vt_ref[0, :, pl.ds(start, width)]
        scores = [_dot(k_ref[0, h, pl.ds(start, width), :], qt_ref[0, h]) for h in range(MLA_HEADS)]
        for h in range(MLA_HEADS):
            s = scores[h] * scale
            if diagonal:
                kid = lax.broadcasted_iota(jnp.int32, (width, tile), 0)
                qid = lax.broadcasted_iota(jnp.int32, (width, tile), 1)
                s = jnp.where(kid <= qid, s, -jnp.inf)
            m_prev = m_scr[h:h + 1, :]
            m_new = jnp.maximum(m_prev, jnp.max(s, axis=0, keepdims=True))
            alpha = jnp.exp(m_prev - m_new)
            p = jnp.exp(s - m_new)
            l_scr[h:h + 1, :] = alpha * l_scr[h:h + 1, :] + jnp.sum(p, axis=0, keepdims=True)
            acc_scr[h] = alpha * acc_scr[h] + _dot(vt, p.astype(BF16))
            m_scr[h:h + 1, :] = m_new

    def pair_body(p, carry):
        kv_block(pl.multiple_of(p * 2 * tile, 2 * tile), 2 * tile, False)
        return carry

    def single_body(_, carry):
        kv_block(pl.multiple_of((qi - 1) * tile, tile), tile, False)
        return carry

    lax.fori_loop(0, qi // 2, pair_body, 0)
    lax.fori_loop(0, qi % 2, single_body, 0)
    kv_block(pl.multiple_of(qi * tile, tile), tile, True)
    for h in range(MLA_HEADS):
        sl = slice(h * MLA_V, (h + 1) * MLA_V)
        ctx_t = (acc_scr[h] / l_scr[h:h + 1, :]).astype(BF16)
        out_ref[0, :, sl] = _dot_tn(ctx_t, wuv_ref[:, sl]) * _silu(zc_ref[0, :, sl])


def _mla_attn_prompt(qt, k, latt, z, w_uv, layer, tile=256):
    b, hds, _, l = qt.shape
    kern = functools.partial(_mla_attn_kernel, tile=tile)
    return pl.pallas_call(
        kern,
        grid=(b, l // tile),
        in_specs=[pl.BlockSpec((1, hds, MLA_QK, tile), lambda bi, qi: (bi, 0, 0, qi)),
                  pl.BlockSpec((1, hds, l, MLA_QK), lambda bi, qi: (bi, 0, 0, 0)),
                  pl.BlockSpec((1, KV_RANK, l), lambda bi, qi: (bi, 0, 0)),
                  pl.BlockSpec((1, tile, GROUP_W), lambda bi, qi: (bi, qi, 2)),
                  _layer_spec(w_uv, layer)],
        out_specs=pl.BlockSpec((1, tile, GROUP_W), lambda bi, qi: (bi, qi, 0)),
        out_shape=jax.ShapeDtypeStruct((b, l, GROUP_W), F32),
        scratch_shapes=[pltpu.VMEM((SUBLANES, tile), F32), pltpu.VMEM((SUBLANES, tile), F32),
                        pltpu.VMEM((hds, KV_RANK, tile), F32)],
        compiler_params=_cparams(("parallel", "arbitrary")),
        name="mla_attn",
    )(qt, k, latt, z, w_uv)


def _memkv_kernel(mem_ref, g_ref, wk_ref, wv_ref, xkg_ref, mk_ref, mv_ref):
    mem_len = mem_ref.shape[1]
    mem_n = _rms(mem_ref[0], g_ref[0]).astype(BF16)
    kraw = _dot(mem_n, wk_ref[0])
    vraw = _dot(mem_n, wv_ref[0])
    for h in range(X_HEADS):
        sl = slice(h * X_HEAD_DIM, (h + 1) * X_HEAD_DIM)
        rows = pl.ds(h, mem_len, stride=X_HEADS)
        mk_ref[0, 0, rows, :] = _rms(kraw[:, sl], xkg_ref[0])
        mv_ref[0, 0, rows, :] = vraw[:, sl]


def _memkv(mem, mem_norm_g, w_mem_k, w_mem_v, xk_norm_g):
    b, m, _ = mem.shape
    depth = w_mem_k.shape[0]
    lsel = lambda li, bi: (li, 0, 0)
    return pl.pallas_call(
        _memkv_kernel,
        grid=(depth, b),
        in_specs=[pl.BlockSpec((1, m, D_MODEL), lambda li, bi: (bi, 0, 0)),
                  pl.BlockSpec((1, 1, D_MODEL), lsel),
                  pl.BlockSpec((1, D_MODEL, GROUP_W), lsel),
                  pl.BlockSpec((1, D_MODEL, GROUP_W), lsel),
                  pl.BlockSpec((1, 1, X_HEAD_DIM), lsel)],
        out_specs=[pl.BlockSpec((1, 1, m * X_HEADS, X_HEAD_DIM), lambda li, bi: (li, bi, 0, 0)),
                   pl.BlockSpec((1, 1, m * X_HEADS, X_HEAD_DIM), lambda li, bi: (li, bi, 0, 0))],
        out_shape=[jax.ShapeDtypeStruct((depth, b, m * X_HEADS, X_HEAD_DIM), F32),
                   jax.ShapeDtypeStruct((depth, b, m * X_HEADS, X_HEAD_DIM), F32)],
        compiler_params=_cparams(("parallel", "parallel")),
        name="memkv",
    )(mem, mem_norm_g, w_mem_k, w_mem_v, xk_norm_g)


def _xattn_kernel(xq_ref, mk_ref, mv_ref, zd_ref, g_ref, out_ref, *, rows, mem_len, nbatch):
    if rows == 1:
        for bb in range(nbatch):
            xq = xq_ref[bb]
            heads = [xq[:, h * X_HEAD_DIM:(h + 1) * X_HEAD_DIM] for h in range(X_HEADS)]
            qn = _rms(jnp.concatenate(heads + heads, axis=0), g_ref[...]).astype(BF16)
            s = _dot_nt(qn, mk_ref[0, bb].astype(BF16)) * (X_HEAD_DIM ** -0.5)
            row_head = lax.broadcasted_iota(jnp.int32, s.shape, 0) & (X_HEADS - 1)
            col_head = lax.broadcasted_iota(jnp.int32, s.shape, 1) & (X_HEADS - 1)
            s = jnp.where(row_head == col_head, s, -jnp.inf)
            e = jnp.exp(s - jnp.max(s, axis=1, keepdims=True))
            p = e / jnp.sum(e, axis=1, keepdims=True)
            y = _dot(p.astype(BF16), mv_ref[0, bb].astype(BF16))
            out_ref[bb] = jnp.concatenate([y[h:h + 1] for h in range(X_HEADS)], axis=1) * _silu(zd_ref[bb])
        return
    xq = xq_ref[0]
    zd = zd_ref[0]
    for h in range(X_HEADS):
        sl = slice(h * X_HEAD_DIM, (h + 1) * X_HEAD_DIM)
        mk_h = mk_ref[0, 0, pl.ds(h, mem_len, stride=X_HEADS), :]
        mv_h = mv_ref[0, 0, pl.ds(h, mem_len, stride=X_HEADS), :]
        qn = _rms(xq[:, sl], g_ref[...]).astype(BF16)
        s = _dot_nt(qn, mk_h.astype(BF16)) * (X_HEAD_DIM ** -0.5)
        s = s - jnp.max(s, axis=1, keepdims=True)
        e = jnp.exp(s)
        p = e / jnp.sum(e, axis=1, keepdims=True)
        y = _dot(p.astype(BF16), mv_h.astype(BF16))
        out_ref[0, :, sl] = y * _silu(zd[:, sl])


XATTN_DECODE_BATCH = 4


def _xattn(xq, mk, mv, layer, z, g, tm):
    b, l, _ = xq.shape
    mem_len = mk.shape[2] // X_HEADS
    nb = XATTN_DECODE_BATCH if tm == 1 else 1
    kern = functools.partial(_xattn_kernel, rows=tm, mem_len=mem_len, nbatch=nb)
    mem_spec = pl.BlockSpec((1, nb) + mk.shape[2:], lambda bi, ti: (layer, bi, 0, 0))
    return pl.pallas_call(
        kern,
        grid=(b // nb, l // tm),
        in_specs=[pl.BlockSpec((nb, tm, GROUP_W), lambda bi, ti: (bi, ti, 0)),
                  mem_spec, mem_spec,
                  pl.BlockSpec((nb, tm, GROUP_W), lambda bi, ti: (bi, ti, 3)),
                  _layer_spec(g, layer)],
        out_specs=pl.BlockSpec((nb, tm, GROUP_W), lambda bi, ti: (bi, ti, 0)),
        out_shape=jax.ShapeDtypeStruct((b, l, GROUP_W), F32),
        compiler_params=_cparams(("parallel", "parallel")),
        name="xattn",
    )(xq, mk, mv, z, g)


def _outproj_kernel(ya_ref, yb_ref, yc_ref, yd_ref, x_ref, w_ref, fg_ref, out_ref, *, final):
    acc = x_ref[...]
    for i, ref in enumerate((ya_ref, yb_ref, yc_ref, yd_ref)):
        acc = acc + _dot(ref[...].astype(BF16), w_ref[i * GROUP_W:(i + 1) * GROUP_W, :])
    if final:
        acc = _rms(acc, fg_ref[...])
    out_ref[...] = acc


def _outproj(ys, x2d, w_out, layer, final_g, tm, final):
    n = x2d.shape[0]
    kern = functools.partial(_outproj_kernel, final=final)
    yspec = pl.BlockSpec((tm, GROUP_W), lambda i: (i, 0))
    return pl.pallas_call(
        kern,
        grid=(n // tm,),
        in_specs=[yspec, yspec, yspec, yspec,
                  pl.BlockSpec((tm, D_MODEL), lambda i: (i, 0)),
                  _layer_spec(w_out, layer), _const_spec((1, D_MODEL))],
        out_specs=pl.BlockSpec((tm, D_MODEL), lambda i: (i, 0)),
        out_shape=jax.ShapeDtypeStruct((n, D_MODEL), F32),
        compiler_params=_cparams(("parallel",)),
        name="outproj",
    )(*ys, x2d, w_out, final_g)


def _col_form(row, n):
    return jnp.broadcast_to(row, (LANES, n)).T


N_DEC_STATES = 6


DEC_STATE_BATCH = 4


def _dec_state_kernel(*refs):
    params = refs[11:25]
    for bb in range(DEC_STATE_BATCH):
        one = lambda r: r.at[pl.ds(bb, 1)]
        _dec_state_one([one(r) for r in refs[0:5]], [one(r.at[0]) for r in refs[5:5 + N_DEC_STATES]], params,
                       [one(r) for r in refs[-8:-6]], [one(r.at[0]) for r in refs[-N_DEC_STATES:]])


def _dec_state_one(tokens, states, params, ys, new_states):
    xbc_ref, xm_ref, o_ref, z_ref, misc_ref = tokens
    hs_ref, sbuf_ref, cm_ref, nm_ref, mm_ref, mbuf_ref = states
    (cwa_ref, cba_ref, dtb_ref, alog_ref, hexp_ref, dskip_ref, nga_ref,
     cwb_ref, cbb_ref, wq_ref, wk_ref, wv_ref, gb_ref, ngb_ref) = params
    ya_ref, yb_ref = ys
    hs_out, sbuf_out, cm_out, nm_out, mm_out, mbuf_out = new_states
    misc = misc_ref[0]
    z = z_ref[0]

    u = xbc_ref[0]
    buf = sbuf_ref[0]
    cw = cwa_ref[...]
    conv = cba_ref[...] + cw[3:4] * u
    for j in range(CONV_W - 1):
        conv = conv + cw[j:j + 1] * buf[j:j + 1]
    sbuf_out[0, 0:CONV_W - 2, :] = buf[1:CONV_W - 1]
    sbuf_out[0, CONV_W - 2:CONV_W - 1, :] = u
    xbcs = _silu(conv)
    xa = xbcs[:, 0:GROUP_W]
    dt = _softplus(misc + dtb_ref[...])
    a = -jnp.exp(alog_ref[...])
    hexp = hexp_ref[...]
    dt8 = jnp.broadcast_to(dt, (SUBLANES, LANES))
    da8 = jnp.broadcast_to(dt * a, (SUBLANES, LANES))
    dt_x = jnp.dot(dt8, hexp, precision=HIGHEST, preferred_element_type=F32)[0:1]
    decay_x = jnp.exp(jnp.dot(da8, hexp, precision=HIGHEST, preferred_element_type=F32)[0:1])
    hstate = hs_ref[0].reshape(SSD_HEADS * SSD_HEAD_DIM, SSD_STATE)
    gw = GROUP_W // SSD_GROUPS
    c_off = GROUP_W + SSD_GROUPS * SSD_STATE
    dtx_col = _col_form(dt_x * xa, GROUP_W)
    decay_col = _col_form(decay_x, GROUP_W)
    y_parts = []
    for g in range(SSD_GROUPS):
        b_g = xbcs[:, GROUP_W + g * SSD_STATE:GROUP_W + (g + 1) * SSD_STATE]
        c_g = xbcs[:, c_off + g * SSD_STATE:c_off + (g + 1) * SSD_STATE]
        cb = jnp.sum(c_g * b_g, axis=1, keepdims=True)
        rows = slice(g * gw, (g + 1) * gw)
        h_g = hstate[rows]
        c8 = jnp.broadcast_to(c_g, (SUBLANES, SSD_STATE)).astype(BF16)
        ch = _dot_nt(c8, h_g.astype(BF16))[0:1]
        y_parts.append(cb * dt_x[:, rows] * xa[:, rows] + ch * decay_x[:, rows])
        h_new = decay_col[rows] * h_g + dtx_col[rows] * b_g
        hs_out[0, g * SSD_HPG:(g + 1) * SSD_HPG] = h_new.reshape(SSD_HPG, SSD_HEAD_DIM, SSD_STATE)
    y = jnp.concatenate(y_parts, axis=1) + dskip_ref[...] * xa
    y = y * _silu(z[:, 0:GROUP_W])
    nga = nga_ref[...]
    for g in range(SSD_GROUPS):
        rows = slice(g * gw, (g + 1) * gw)
        ya_ref[0, :, rows] = _rms(y[:, rows], nga[:, rows])

    um = xm_ref[0]
    mbuf = mbuf_ref[0]
    cwb = cwb_ref[...]
    convb = cbb_ref[...] + cwb[3:4] * um
    for j in range(CONV_W - 1):
        convb = convb + cwb[j:j + 1] * mbuf[j:j + 1]
    mbuf_out[0, 0:CONV_W - 2, :] = mbuf[1:CONV_W - 1]
    mbuf_out[0, CONV_W - 2:CONV_W - 1, :] = um
    xmc = _silu(convb)
    gates = misc + gb_ref[...]
    lf_all = -_softplus(-gates)
    mm = mm_ref[0]
    ngb = ngb_ref[...]
    o_all = o_ref[0]
    for h in range(ML_HEADS):
        sl = slice(h * ML_HEAD_DIM, (h + 1) * ML_HEAD_DIM)
        li = gates[:, MISC_I + h:MISC_I + h + 1]
        lf = lf_all[:, MISC_F + h:MISC_F + h + 1]
        m_prev = mm[:, h:h + 1]
        l_prev = lf + m_prev
        m_t = jnp.maximum(l_prev, li)
        xmc8 = jnp.broadcast_to(xmc[:, sl], (SUBLANES, ML_HEAD_DIM)).astype(BF16)
        um8 = jnp.broadcast_to(um[:, sl], (SUBLANES, ML_HEAD_DIM)).astype(BF16)
        q8 = _dot(xmc8, wq_ref[h])
        q = q8[0:1]
        k = _dot(xmc8, wk_ref[h])[0:1] * (ML_HEAD_DIM ** -0.5)
        v = _dot(um8, wv_ref[h])[0:1]
        qk = jnp.sum(q * k, axis=1, keepdims=True)
        sc = qk * jnp.exp(li - m_t)
        g_prev = jnp.exp(l_prev - m_t)
        cs = cm_ref[0, h]
        ns = nm_ref[0, h:h + 1, :]
        cq = _dot_nt(q8.astype(BF16), cs.astype(BF16))[0:1]
        num = sc * v + g_prev * cq
        den = sc + g_prev * jnp.sum(ns * q, axis=1, keepdims=True)
        hout = num / jnp.maximum(jnp.abs(den), jnp.exp(-m_t))
        wend = jnp.exp(li - m_t)
        decay = jnp.exp(l_prev - m_t)
        cm_out[0, h] = decay * cs + _col_form(wend * v, ML_HEAD_DIM) * k
        nm_out[0, h:h + 1, :] = decay * ns + wend * k
        mm_out[0, :, h:h + 1] = m_t
        hb = jax.nn.sigmoid(o_all[:, sl]) * hout
        yb_ref[0, :, sl] = _rms(hb, ngb[:, sl]) * _silu(z[:, GROUP_W + h * ML_HEAD_DIM:GROUP_W + (h + 1) * ML_HEAD_DIM])


def _dec_state(xbc, xm, o, z, misc, states, layer, lp):
    b = xbc.shape[0]
    nb = DEC_STATE_BATCH
    tok = lambda w: pl.BlockSpec((nb, 1, w), lambda bi: (bi, 0, 0))
    at_layer = lambda t: pl.BlockSpec((1, nb) + t.shape[2:], lambda bi: (layer, bi) + (0,) * (t.ndim - 2))
    state_specs = [at_layer(t) for t in states]
    pa, pa_specs = _layer_args(lp, layer, ("conv_a_w", "conv_a_b", "dtb_row", "alog_row"))
    pb, pb_specs = _layer_args(lp, layer, ("dskip_row", "ssd_norm_g", "conv_b_w", "conv_b_b", "w_mq", "w_mk", "w_mv",
                                           "gb_row", "ml_norm_g"))
    params = pa + [lp["head_expand"]] + pb
    param_specs = pa_specs + [_const_spec((LANES, GROUP_W))] + pb_specs
    inputs = [xbc, xm, o, z, misc] + list(states) + params
    in_specs = [tok(SSD_CONV_CH), tok(GROUP_W), tok(GROUP_W), tok(D_MIX), tok(MISC_W)] + state_specs + param_specs
    aliases = {5 + i: 2 + i for i in range(N_DEC_STATES)}
    outs = pl.pallas_call(
        _dec_state_kernel,
        grid=(b // nb,),
        in_specs=in_specs,
        out_specs=[tok(GROUP_W), tok(GROUP_W)] + state_specs,
        out_shape=[jax.ShapeDtypeStruct((b, 1, GROUP_W), F32)] * 2
        + [jax.ShapeDtypeStruct(t.shape, F32) for t in states],
        input_output_aliases=aliases,
        compiler_params=_cparams(("parallel",)),
        name="dec_state",
    )(*inputs)
    return outs[0], outs[1], tuple(outs[2:])


def _mla_decode_kernel(pt_ref, lat_hbm, kr_hbm, qn_ref, qr_ref, qfull_ref, kfull_ref, vnew_ref, zc_ref,
                       wukt_ref, wuv_ref, kng_ref, out_ref,
                       lat_buf, kr_buf, sem, qa_scr, m_scr, l_scr, acc_scr, *, layer, pps, page, nsub):
    b = pl.program_id(0)
    s = pl.program_id(1)
    ns = pl.num_programs(1)
    nb = pl.num_programs(0)
    step = b * ns + s
    last = nb * ns - 1
    slot = step % DECODE_SLOTS
    sub = pps * page // nsub

    def copies(bb, ss, sl):
        out = []
        for p in range(pps):
            pg = pt_ref[bb, ss * pps + p]
            out.append(pltpu.make_async_copy(lat_hbm.at[layer, pg], lat_buf.at[sl, pl.ds(p * page, page), :],
                                             sem.at[0, sl]))
            out.append(pltpu.make_async_copy(kr_hbm.at[layer, pg], kr_buf.at[sl, :, pl.ds(p * page, page)],
                                             sem.at[1, sl]))
        return out

    def start_step(target, sl):
        tgt = jnp.minimum(target, last)
        for cp in copies(tgt // ns, tgt % ns, sl):
            cp.start()

    @pl.when(step == 0)
    def _():
        start_step(step, 0)
        start_step(step + 1, 1)

    scale = MLA_QK ** -0.5

    @pl.when(s == 0)
    def _():
        qng = (qn_ref[0].astype(F32) * kng_ref[...]).astype(BF16)
        for h in range(MLA_HEADS):
            full = _dot(qng, wukt_ref[h * MLA_NOPE:(h + 1) * MLA_NOPE, :])
            qa_scr[h:h + 1, :] = full[h:h + 1]
            qa_scr[MLA_HEADS + h:MLA_HEADS + h + 1, :] = full[h:h + 1]
        m_scr[...] = jnp.full_like(m_scr, -jnp.inf)
        l_scr[...] = jnp.zeros_like(l_scr)
        acc_scr[...] = jnp.zeros_like(acc_scr)

    for cp in copies(b, s, slot):
        cp.wait()

    qa = qa_scr[...].astype(BF16)
    qr = qr_ref[0]
    wukt = wukt_ref[...]

    def scores(i):
        latb = lat_buf[slot, i * sub:(i + 1) * sub, :].astype(BF16)
        krb = kr_buf[slot, :, i * sub:(i + 1) * sub].astype(BF16)
        kraw_t = _dot_nt(wukt, latb)
        sa = _dot_nt(qa, latb)
        sr = _dot(qr, krb)
        rinv_rows = []
        for h in range(MLA_HEADS):
            kh = kraw_t[h * MLA_NOPE:(h + 1) * MLA_NOPE, :]
            ssq = jnp.sum(kh * kh, axis=0, keepdims=True)
            rinv_rows.append(lax.rsqrt(ssq * (1.0 / MLA_NOPE) + EPS))
        rinv = jnp.concatenate(rinv_rows + rinv_rows, axis=0)
        return latb, (sa * rinv + sr) * scale

    m = m_scr[...]
    l = l_scr[...]
    acc = acc_scr[...]
    pending = scores(0)
    for i in range(nsub):
        latb, sc = pending
        if i + 1 < nsub:
            pending = scores(i + 1)
        m_new = jnp.maximum(m, jnp.max(sc, axis=1, keepdims=True))
        alpha = jnp.exp(m - m_new)
        p = jnp.exp(sc - m_new)
        l = alpha * l + jnp.sum(p, axis=1, keepdims=True)
        acc = alpha * acc + _dot(p.astype(BF16), latb)
        m = m_new
    m_scr[...] = m
    l_scr[...] = l
    acc_scr[...] = acc

    start_step(step + 2, (step + 2) % DECODE_SLOTS)

    @pl.when(s == ns - 1)
    def _():
        s_new = jnp.sum(qfull_ref[0].astype(F32) * kfull_ref[0].astype(F32), axis=1, keepdims=True) * scale
        m_fin = jnp.maximum(m, s_new)
        alpha2 = jnp.exp(m - m_fin)
        p_new = jnp.exp(s_new - m_fin)
        l_fin = alpha2 * l + p_new
        acc_fin = alpha2 * acc + p_new * vnew_ref[0].astype(F32)
        ctx = (acc_fin / l_fin).astype(BF16)
        yfull = _dot(ctx, wuv_ref[...])
        parts = [yfull[h:h + 1, h * MLA_V:(h + 1) * MLA_V] for h in range(MLA_HEADS)]
        out_ref[0] = jnp.concatenate(parts, axis=1) * _silu(zc_ref[0])

    @pl.when(step == last)
    def _():
        for ahead in (1, 2):
            for cp in copies(b, s, (step + ahead) % DECODE_SLOTS):
                cp.wait()


DECODE_PAGES_PER_STEP = 64
DECODE_SUB_BLOCKS = 2
DECODE_SLOTS = 3


def _mla_decode(page_table, cache_lat, cache_kr, layer, qn8, qr8, qfull8, kfull8, vnew, z, lp,
                pps=DECODE_PAGES_PER_STEP, nsub=DECODE_SUB_BLOCKS):
    b, n_pages = page_table.shape
    page = cache_lat.shape[2]
    ns = n_pages // pps
    t = pps * page
    kern = functools.partial(_mla_decode_kernel, layer=layer, pps=pps, page=page, nsub=nsub)
    per_b = lambda shape: pl.BlockSpec((1,) + shape, lambda bi, si, pt: (bi, 0, 0))
    params, param_specs = _layer_args(lp, layer, ("w_uk_t", "w_uv", "kn_nope_g"))
    grid_spec = pltpu.PrefetchScalarGridSpec(
        num_scalar_prefetch=1,
        grid=(b, ns),
        in_specs=[pl.BlockSpec(memory_space=pl.ANY), pl.BlockSpec(memory_space=pl.ANY),
                  per_b((SUBLANES, MLA_NOPE)), per_b((SUBLANES, MLA_ROPE)),
                  per_b((SUBLANES, MLA_QK)), per_b((SUBLANES, MLA_QK)), per_b((1, KV_RANK)),
                  pl.BlockSpec((1, 1, GROUP_W), lambda bi, si, pt: (bi, 0, 2)),
                  ] + param_specs,
        out_specs=pl.BlockSpec((1, 1, GROUP_W), lambda bi, si, pt: (bi, 0, 0)),
        scratch_shapes=[pltpu.VMEM((DECODE_SLOTS, t, KV_RANK), F32), pltpu.VMEM((DECODE_SLOTS, MLA_ROPE, t), F32),
                        pltpu.SemaphoreType.DMA((2, DECODE_SLOTS)),
                        pltpu.VMEM((SUBLANES, KV_RANK), F32),
                        pltpu.VMEM((SUBLANES, 1), F32), pltpu.VMEM((SUBLANES, 1), F32),
                        pltpu.VMEM((SUBLANES, KV_RANK), F32)],
    )
    return pl.pallas_call(
        kern,
        grid_spec=grid_spec,
        out_shape=jax.ShapeDtypeStruct((b, 1, GROUP_W), F32),
        compiler_params=_cparams(("arbitrary", "arbitrary")),
        name="mla_decode",
    )(page_table, cache_lat, cache_kr, qn8, qr8, qfull8, kfull8, vnew, z, *params)


def _pack_w_in(w_in):
    names = ("z", "xbc", "dt", "xm", "i", "f", "o", "cq", "ckv", "kr", "xq")
    offs = {}
    o = 0
    for nme, w in zip(names, IN_SIZES):
        offs[nme] = (o, o + w)
        o += w
    col = lambda nme: w_in[:, :, offs[nme][0]:offs[nme][1]].astype(BF16)
    pad = jnp.zeros(w_in.shape[:2] + (MISC_W - (MISC_F + ML_HEADS),), BF16)
    misc = jnp.concatenate([col("kr"), col("dt"), col("i"), col("f"), pad], axis=-1)
    return [misc if nme == "misc" else col(nme) for nme, _ in PACK_SEGS]


def _stack_params(p):
    depth = p["norm_g"].shape[0]
    row = lambda a: a[:, None, :]
    lp = {}
    for nme in ("norm_g", "conv_a_b", "ssd_norm_g", "conv_b_b", "ml_norm_g", "q_norm_g", "kv_norm_g", "qn_nope_g",
                "kn_nope_g", "xq_norm_g"):
        lp[nme] = row(p[nme])
    lp["conv_a_w"] = p["conv_a_w"]
    lp["conv_b_w"] = p["conv_b_w"]
    lp["dtb_row"] = _lane_pad(p["dt_bias"], MISC_DT)
    lp["alog_row"] = _lane_pad(p["a_log"], MISC_DT)
    lp["dtb_col"] = p["dt_bias"][:, :, None]
    lp["alog_col"] = p["a_log"][:, :, None]
    lp["dskip_row"] = row(jnp.repeat(p["d_skip"], SSD_HEAD_DIM, axis=1))
    lp["w_mq"] = p["w_mq"].astype(BF16)
    lp["w_mk"] = p["w_mk"].astype(BF16)
    lp["w_mv"] = p["w_mv"].astype(BF16)
    lp["w_mq_t"] = jnp.swapaxes(lp["w_mq"], 2, 3)
    lp["w_mv_t"] = jnp.swapaxes(lp["w_mv"], 2, 3)
    gate_b = jnp.concatenate([p["b_i"], p["b_f"]], axis=1)
    lp["gb_row"] = _lane_pad(gate_b, MISC_I)
    lp["gb_col"] = gate_b[:, :, None]
    w_uq = p["w_uq"]
    lp["w_uq_nope"] = w_uq[..., :MLA_NOPE].reshape(depth, Q_RANK, MLA_HEADS * MLA_NOPE).astype(BF16)
    rope_pad = lambda t: jnp.pad(t, [(0, 0)] * (t.ndim - 1) + [(0, LANES - MLA_ROPE)])
    lp["w_uq_rope"] = rope_pad(jnp.transpose(w_uq[..., MLA_NOPE:], (0, 2, 1, 3))).astype(BF16)
    w_uk = p["w_uk"].reshape(depth, KV_RANK, MLA_HEADS * MLA_NOPE)
    lp["w_uk"] = w_uk.astype(BF16)
    lp["w_uk_t"] = jnp.swapaxes(w_uk, 1, 2).astype(BF16)
    lp["w_uv"] = p["w_uv"].reshape(depth, KV_RANK, MLA_HEADS * MLA_V).astype(BF16)
    lp["w_uq_nope_t"] = jnp.swapaxes(lp["w_uq_nope"], 1, 2)
    lp["w_uq_rope_t"] = jnp.transpose(w_uq[..., MLA_NOPE:], (0, 2, 3, 1)).astype(BF16)
    lp["qn_nope_g_col"] = p["qn_nope_g"][:, :, None]
    lp["qn_rope_g_col"] = p["qn_rope_g"][:, :, None]
    lp["qn_rope_g_pad"] = rope_pad(row(p["qn_rope_g"]))
    lp["kn_rope_g_pad"] = rope_pad(row(p["kn_rope_g"]))
    lp["w_out"] = p["w_out"].astype(BF16)
    head_of_lane = jnp.arange(LANES)[:, None] - MISC_DT
    head_of_chan = jnp.arange(GROUP_W)[None, :] // SSD_HEAD_DIM
    lp["head_expand"] = (head_of_lane == head_of_chan).astype(F32)
    return lp


def _rope_tables(pos):
    half = MLA_ROPE // 2
    inv = ROPE_BASE ** (-jnp.arange(half, dtype=F32) * 2.0 / MLA_ROPE)
    ang = pos.astype(F32)[:, None] * inv[None, :]
    cos = jnp.cos(ang)
    sin = jnp.sin(ang)
    zero = jnp.zeros((pos.shape[0], LANES - MLA_ROPE), F32)
    cos2 = jnp.concatenate([cos, cos], axis=1)
    sinsgn = jnp.concatenate([-sin, sin], axis=1)
    return jnp.concatenate([cos2, zero], axis=1), jnp.concatenate([sinsgn, zero], axis=1), cos2.T, sinsgn.T


def _prompt_layer(lp, w_slabs, layer, x, mk_all, mv_all, rope_tabs, final_g, final):
    b, l, _ = x.shape
    n = b * l
    x2d = x.reshape(n, D_MODEL)
    z, xbc, xm, o, cq, ckv, xq, misc = _inproj(x2d, lp["norm_g"], w_slabs, layer, tm=512)
    r3 = lambda t: t.reshape(b, l, t.shape[-1])
    z, xbc, xm, o, cq, ckv, xq, misc = map(r3, (z, xbc, xm, o, cq, ckv, xq, misc))
    ya, ssm_h, ssm_buf, yb, ml_c, ml_n, ml_m, ml_buf = _scans_prompt(xbc, xm, o, z, misc, lp, layer)
    qt, k, latt, lat, kr = _mla_prep(cq, ckv, misc, rope_tabs, lp, layer, tm=512, transposed=True)
    yc = _mla_attn_prompt(qt, k, latt, z, lp["w_uv"], layer)
    yd = _xattn(xq, mk_all, mv_all, layer, z, lp["xq_norm_g"], tm=512)
    ys = [t.reshape(n, GROUP_W) for t in (ya, yb, yc, yd)]
    x_new = _outproj(ys, x2d, lp["w_out"], layer, final_g, tm=512, final=final).reshape(b, l, D_MODEL)
    return x_new, (lat, kr, ssm_h, ssm_buf, ml_c, ml_n, ml_m, ml_buf)


def _sample_layer(lp, w_slabs, layer, x, mem_k, mem_v, cache_lat, cache_kr, page_table, states,
                  rope_tabs, final_g, final, pps=DECODE_PAGES_PER_STEP):
    b = x.shape[0]
    x2d = x.reshape(b, D_MODEL)
    z, xbc, xm, o, cq, ckv, xq, misc = _inproj(x2d, lp["norm_g"], w_slabs, layer, tm=b)
    tok = lambda t: t.reshape(b, 1, t.shape[-1])
    ya, yb, st_new = _dec_state(tok(xbc), tok(xm), tok(o), tok(z), tok(misc), states, layer, lp)
    one = lambda t: t.reshape(1, b, t.shape[-1])
    q, k, latb, lat, kr = _mla_prep(one(cq), one(ckv), one(misc), rope_tabs, lp, layer, tm=b, transposed=False)
    qh = jnp.transpose(q[0], (1, 0, 2))
    kh = jnp.transpose(k[0], (1, 0, 2))
    dup = lambda t: jnp.concatenate([t, t], axis=1)
    qfull8 = dup(qh)
    kfull8 = dup(kh)
    qn8 = qfull8[:, :, :MLA_NOPE]
    qr8 = qfull8[:, :, MLA_NOPE:]
    vnew = latb.reshape(b, 1, KV_RANK)
    yc = _mla_decode(page_table, cache_lat, cache_kr, layer, qn8, qr8, qfull8, kfull8, vnew, tok(z), lp, pps=pps)
    yd = _xattn(tok(xq), mem_k, mem_v, layer, tok(z), lp["xq_norm_g"], tm=1)
    ys = [t.reshape(b, GROUP_W) for t in (ya, yb, yc, yd)]
    x_new = _outproj(ys, x2d, lp["w_out"], layer, final_g, tm=b, final=final).reshape(b, 1, D_MODEL)
    return x_new, lat.reshape(b, 1, KV_RANK), kr.reshape(b, 1, MLA_ROPE), st_new


def kernel(x_prompt, x_sample, mem_prompt, cache_mla_latent, cache_mla_krope, cache_mem_k, cache_mem_v, state_ssm, state_ssm_conv, state_mlstm_C, state_mlstm_n, state_mlstm_m, state_mlstm_conv, page_table, norm_g, w_in, conv_a_w, conv_a_b, dt_bias, a_log, d_skip, ssd_norm_g, conv_b_w, conv_b_b, w_mq, w_mk, w_mv, b_i, b_f, ml_norm_g, q_norm_g, kv_norm_g, w_uq, w_uk, w_uv, qn_nope_g, qn_rope_g, kn_nope_g, kn_rope_g, mem_norm_g, w_mem_k, w_mem_v, xk_norm_g, xq_norm_g, w_out, final_norm_g):
    params = dict(norm_g=norm_g, conv_a_w=conv_a_w, conv_a_b=conv_a_b, dt_bias=dt_bias, a_log=a_log, d_skip=d_skip,
                  ssd_norm_g=ssd_norm_g, conv_b_w=conv_b_w, conv_b_b=conv_b_b, w_mq=w_mq, w_mk=w_mk, w_mv=w_mv,
                  b_i=b_i, b_f=b_f, ml_norm_g=ml_norm_g, q_norm_g=q_norm_g, kv_norm_g=kv_norm_g, w_uq=w_uq,
                  w_uk=w_uk, w_uv=w_uv, qn_nope_g=qn_nope_g, qn_rope_g=qn_rope_g, kn_nope_g=kn_nope_g,
                  kn_rope_g=kn_rope_g, xq_norm_g=xq_norm_g, w_out=w_out)
    depth = w_in.shape[0]
    bp, lp_len, _ = x_prompt.shape
    bs, ls_len, _ = x_sample.shape
    n_pages = page_table.shape[1]
    past_len = n_pages * cache_mla_latent.shape[2]
    mem_len = mem_prompt.shape[1]
    w_slabs = _pack_w_in(w_in)
    final_g = final_norm_g[None, :]

    cache_kr_t = jnp.swapaxes(cache_mla_krope, 2, 3)
    rope_p = _rope_tables(jnp.arange(lp_len, dtype=jnp.int32))
    rope_s = _rope_tables(jnp.full((bs,), past_len, dtype=jnp.int32))

    mk_all, mv_all = _memkv(mem_prompt, mem_norm_g[:, None, :], w_mem_k.astype(BF16), w_mem_v.astype(BF16),
                            xk_norm_g[:, None, :])

    mem_rows = cache_mem_k.shape[2] * X_HEADS
    cache_mk = cache_mem_k.reshape(depth, bs, mem_rows, X_HEAD_DIM)
    cache_mv = cache_mem_v.reshape(depth, bs, mem_rows, X_HEAD_DIM)
    states = (state_ssm, state_ssm_conv, state_mlstm_C, state_mlstm_n,
              state_mlstm_m.reshape(depth, bs, 1, ML_HEADS), state_mlstm_conv)

    hp, hs = x_prompt, x_sample
    p_states, s_lats, s_krs = [], [], []
    lp = _stack_params(params)
    for l in range(depth):
        final = l == depth - 1
        hp, st_p = _prompt_layer(lp, w_slabs, l, hp, mk_all, mv_all, rope_p, final_g, final)
        p_states.append(st_p)
        hs, s_lat_l, s_kr_l, states = _sample_layer(lp, w_slabs, l, hs, cache_mk, cache_mv, cache_mla_latent,
                                                    cache_kr_t, page_table, states, rope_s, final_g, final)
        s_lats.append(s_lat_l)
        s_krs.append(s_kr_l)

    p_out = [jnp.stack([st[i] for st in p_states]) for i in range(8)]
    p_mem_k = mk_all.reshape(depth, bp, mem_len, X_HEADS, X_HEAD_DIM)
    p_mem_v = mv_all.reshape(depth, bp, mem_len, X_HEADS, X_HEAD_DIM)
    (p_lat, p_kr, p_ssm, p_sconv, p_mc, p_mn, p_mm, p_mconv) = p_out
    s_ssm, s_sconv, s_mc, s_mn, s_mm, s_mconv = states
    return (hp, hs, p_lat, p_kr, p_mem_k, p_mem_v, p_ssm, p_sconv, p_mc, p_mn, p_mm, p_mconv,
            jnp.stack(s_lats), jnp.stack(s_krs), s_ssm, s_sconv, s_mc, s_mn, s_mm.reshape(depth, bs, ML_HEADS),
            s_mconv)
```

```python
import functools
import math

import jax
import jax.numpy as jnp
from jax import lax
from jax.experimental import pallas as pl
from jax.experimental.pallas import tpu as pltpu

F32 = jnp.float32
BF16 = jnp.bfloat16
HIGHEST = lax.Precision.HIGHEST

D_MODEL = 1024
D_MIX = 2 * D_MODEL
GROUP_W = D_MIX // 4
SSD_HEAD_DIM = 64
SSD_HEADS = GROUP_W // SSD_HEAD_DIM
SSD_GROUPS = 2
SSD_HPG = SSD_HEADS // SSD_GROUPS
SSD_STATE = 128
CONV_W = 4
SSD_CONV_CH = GROUP_W + 2 * SSD_GROUPS * SSD_STATE
ML_HEADS = 4
ML_HEAD_DIM = GROUP_W // ML_HEADS
MLA_HEADS = 4
MLA_NOPE = 128
MLA_ROPE = 64
MLA_QK = MLA_NOPE + MLA_ROPE
MLA_V = GROUP_W // MLA_HEADS
Q_RANK = 3 * D_MODEL // 8
KV_RANK = D_MODEL // 4
ROPE_BASE = 10000.0
X_HEADS = 4
X_HEAD_DIM = GROUP_W // X_HEADS
CHUNK = 256
EPS = 1e-6
IN_SIZES = (D_MIX, SSD_CONV_CH, SSD_HEADS, GROUP_W, ML_HEADS, ML_HEADS, GROUP_W, Q_RANK, KV_RANK, MLA_ROPE, GROUP_W)

LANES = 128
SUBLANES = 8
VMEM_LIMIT = 56 * 1024 * 1024

MISC_W = LANES
MISC_KR = 0
MISC_DT = MLA_ROPE
MISC_I = MISC_DT + SSD_HEADS
MISC_F = MISC_I + ML_HEADS
PACK_SEGS = (("z", D_MIX), ("xbc", SSD_CONV_CH), ("xm", GROUP_W), ("o", GROUP_W), ("cq", Q_RANK),
             ("ckv", KV_RANK), ("xq", GROUP_W), ("misc", MISC_W))
PACK_W = sum(w for _, w in PACK_SEGS)


def _cparams(sem):
    return pltpu.CompilerParams(dimension_semantics=sem, vmem_limit_bytes=VMEM_LIMIT)


def _dot(a, b):
    return jnp.dot(a, b, preferred_element_type=F32)


def _dot_nt(a, b, precision=None):
    return lax.dot_general(a, b, (((1,), (1,)), ((), ())), preferred_element_type=F32, precision=precision)


def _dot_tn(a, b):
    return lax.dot_general(a, b, (((0,), (0,)), ((), ())), preferred_element_type=F32)


def _rms(x, g):
    return x * lax.rsqrt(jnp.mean(x * x, axis=-1, keepdims=True) + EPS) * g


def _silu(x):
    return x * jax.nn.sigmoid(x)


def _softplus(x):
    return jnp.maximum(x, 0.0) + jnp.log1p(jnp.exp(-jnp.abs(x)))


def _const_spec(shape):
    nd = len(shape)
    return pl.BlockSpec(shape, lambda *_: (0,) * nd)


def _layer_spec(arr, layer):
    nd = arr.ndim
    return pl.BlockSpec((None,) + arr.shape[1:], lambda *_: (layer,) + (0,) * (nd - 1))


def _layer_args(lp, layer, names):
    return [lp[n] for n in names], [_layer_spec(lp[n], layer) for n in names]


def _inproj_kernel(x_ref, g_ref, *refs):
    nseg = len(PACK_SEGS)
    w_refs, out_refs = refs[:nseg], refs[nseg:]
    x = x_ref[...]
    u = _rms(x, g_ref[...]).astype(BF16)
    for w_ref, out_ref in zip(w_refs, out_refs):
        out_ref[...] = _dot(u, w_ref[0])


def _inproj(x2d, g, w_slabs, layer, tm):
    n = x2d.shape[0]
    out_shape = [jax.ShapeDtypeStruct((n, w), F32) for _, w in PACK_SEGS]
    out_specs = [pl.BlockSpec((tm, w), lambda i: (i, 0)) for _, w in PACK_SEGS]
    w_specs = [pl.BlockSpec((1, D_MODEL, w), lambda i: (layer, 0, 0), pipeline_mode=pl.Buffered(1))
               for _, w in PACK_SEGS]
    return pl.pallas_call(
        _inproj_kernel,
        grid=(n // tm,),
        in_specs=[pl.BlockSpec((tm, D_MODEL), lambda i: (i, 0)), _layer_spec(g, layer)] + w_specs,
        out_specs=out_specs,
        out_shape=out_shape,
        compiler_params=_cparams(("parallel",)),
        name="inproj",
    )(x2d, g, *w_slabs)


def _tri_masks(n):
    r = lax.broadcasted_iota(jnp.int32, (n, n), 0)
    c = lax.broadcasted_iota(jnp.int32, (n, n), 1)
    return r >= c, c >= r


SSD_SUB_CHUNK = CHUNK


def _ssd_chunk(xbc_ref, misc_ref, za_ref, cw_ref, cb_ref, dtb_row_ref, alog_row_ref, dtb_col_ref,
               alog_col_ref, dskip_ref, ng_ref, ya_ref, h_scr, ext_scr, y_scr):
    u = xbc_ref[0]
    ext_scr[SUBLANES:SUBLANES + CHUNK, :] = u
    cw = cw_ref[...]
    conv = cb_ref[...] + cw[3:4] * u
    for j in range(CONV_W - 1):
        back = CONV_W - 1 - j
        conv = conv + cw[j:j + 1] * ext_scr[SUBLANES - back:SUBLANES - back + CHUNK, :]
    ext_scr[0:SUBLANES, :] = u[CHUNK - SUBLANES:CHUNK, :]
    xbcs = _silu(conv)
    xa = xbcs[:, 0:GROUP_W]

    misc = misc_ref[0]
    misc_t = misc.T
    n = SSD_SUB_CHUNK
    lower, upper = _tri_masks(n)
    lower_f = lower.astype(F32)
    upper_f = upper.astype(F32)
    dt_col = _softplus(misc + dtb_row_ref[...])
    dt_row = _softplus(misc_t[MISC_DT:MISC_DT + SSD_HEADS, :] + dtb_col_ref[...])
    dta_col = dt_col * -jnp.exp(alog_row_ref[...])
    dta_row = dt_row * -jnp.exp(alog_col_ref[...])
    c_off = GROUP_W + SSD_GROUPS * SSD_STATE

    for sub in range(CHUNK // n):
        rs = slice(sub * n, (sub + 1) * n)
        acum_col = jnp.dot(lower_f, dta_col[rs], precision=HIGHEST, preferred_element_type=F32)
        acum_row = jnp.dot(dta_row[:, rs], upper_f, precision=HIGHEST, preferred_element_type=F32)
        for g in range(SSD_GROUPS):
            b_g = xbcs[rs, GROUP_W + g * SSD_STATE:GROUP_W + (g + 1) * SSD_STATE].astype(BF16)
            c_g = xbcs[rs, c_off + g * SSD_STATE:c_off + (g + 1) * SSD_STATE].astype(BF16)
            cb = _dot_nt(c_g, b_g)
            for hh in range(SSD_HPG):
                h = g * SSD_HPG + hh
                cols = slice(h * SSD_HEAD_DIM, (h + 1) * SSD_HEAD_DIM)
                ac_col = acum_col[:, MISC_DT + h:MISC_DT + h + 1]
                ac_row = acum_row[h:h + 1, :]
                seg = jnp.where(lower, ac_col - ac_row, -jnp.inf)
                w = cb * jnp.exp(seg) * dt_row[h:h + 1, rs]
                x_h = xa[rs, cols]
                hs = h_scr[h]
                y_h = _dot(w.astype(BF16), x_h.astype(BF16)) + _dot_nt(c_g, hs.astype(BF16)) * jnp.exp(ac_col)
                y_scr[rs, cols] = y_h
                ac_last = ac_col[n - 1:n, :]
                wend = jnp.exp(ac_last - ac_col) * dt_col[rs, MISC_DT + h:MISC_DT + h + 1]
                xw = (x_h * wend).astype(BF16)
                h_scr[h] = jnp.exp(ac_last) * hs + _dot_tn(xw, b_g)

    y = y_scr[...] + dskip_ref[...] * xa
    y = y * _silu(za_ref[0])
    gw = GROUP_W // SSD_GROUPS
    ng = ng_ref[...]
    for g in range(SSD_GROUPS):
        ya_ref[0, :, g * gw:(g + 1) * gw] = _rms(y[:, g * gw:(g + 1) * gw], ng[:, g * gw:(g + 1) * gw])


def _lane_pad(v, offset):
    return jnp.pad(v, ((0, 0), (offset, LANES - offset - v.shape[1])))[:, None, :]


def _mlstm_chunk(xm_ref, o_ref, zb_ref, misc_ref, cw_ref, cb_ref, wqt_ref, wk_ref, wvt_ref, gb_row_ref,
                 gb_col_ref, ng_ref, yb_ref, c_scr, n_scr, m_scr, ext_scr):
    u = xm_ref[0]
    ext_scr[SUBLANES:SUBLANES + CHUNK, :] = u
    cw = cw_ref[...]
    conv = cb_ref[...] + cw[3:4] * u
    for j in range(CONV_W - 1):
        back = CONV_W - 1 - j
        conv = conv + cw[j:j + 1] * ext_scr[SUBLANES - back:SUBLANES - back + CHUNK, :]
    ext_scr[0:SUBLANES, :] = u[CHUNK - SUBLANES:CHUNK, :]
    xmc = _silu(conv)

    misc = misc_ref[0]
    misc_t = misc.T
    lower, upper = _tri_masks(CHUNK)
    gates_col = misc + gb_row_ref[...]
    gates_row = misc_t[MISC_I:MISC_I + 2 * ML_HEADS, :] + gb_col_ref[...]
    lf_col = -_softplus(-gates_col)
    lf_row = -_softplus(-gates_row)
    bcum_col = jnp.dot(lower.astype(F32), lf_col, precision=HIGHEST, preferred_element_type=F32)
    bcum_row = jnp.dot(lf_row, upper.astype(F32), precision=HIGHEST, preferred_element_type=F32)
    ng = ng_ref[...]
    o_all = o_ref[0]
    zb = zb_ref[0]
    m_all = m_scr[...]
    n_all = n_scr[...]
    m_rows, n_rows = [], []
    for h in range(ML_HEADS):
        sl = slice(h * ML_HEAD_DIM, (h + 1) * ML_HEAD_DIM)
        li_r = gates_row[h:h + 1, :]
        bc_r = bcum_row[ML_HEADS + h:ML_HEADS + h + 1, :]
        col = gates_col[:, MISC_I + h:MISC_I + h + 1] - bcum_col[:, MISC_F + h:MISC_F + h + 1]
        m_prev = m_all[h:h + 1, 0:1]
        lw = jnp.where(upper, bc_r + col, -jnp.inf)
        l_prev = bc_r + m_prev
        m_t = jnp.maximum(l_prev, jnp.max(lw, axis=0, keepdims=True))
        xmc_h = xmc[:, sl].astype(BF16)
        qtb = _dot_nt(wqt_ref[h], xmc_h).astype(BF16)
        k = _dot(xmc_h, wk_ref[h]) * (ML_HEAD_DIM ** -0.5)
        kb = k.astype(BF16)
        vt = _dot_nt(wvt_ref[h], u[:, sl].astype(BF16))
        sc = _dot(kb, qtb) * jnp.exp(lw - m_t)
        g_prev = jnp.exp(l_prev - m_t)
        cs = c_scr[h]
        ns = n_all[h:h + 1, :]
        ns8 = jnp.broadcast_to(ns, (SUBLANES, ML_HEAD_DIM)).astype(BF16)
        num = _dot(vt.astype(BF16), sc.astype(BF16)) + g_prev * _dot(cs.astype(BF16), qtb)
        den = jnp.sum(sc, axis=0, keepdims=True) + g_prev * _dot(ns8, qtb)[0:1]
        hout_t = num / jnp.maximum(jnp.abs(den), jnp.exp(-m_t))
        m_new = m_t[:, CHUNK - 1:CHUNK]
        bc_last = bc_r[:, CHUNK - 1:CHUNK]
        wend = jnp.exp(bc_last - bc_r + li_r - m_new)
        decay = jnp.exp(bc_last + m_prev - m_new)
        c_scr[h] = decay * cs + _dot((vt * wend).astype(BF16), kb)
        wend8 = jnp.broadcast_to(wend, (SUBLANES, CHUNK)).astype(BF16)
        n_rows.append(decay * ns + _dot(wend8, kb)[0:1])
        m_rows.append(jnp.broadcast_to(m_new, (1, LANES)))
        hb = jax.nn.sigmoid(o_all[:, sl]) * hout_t.T
        yb_ref[0, :, sl] = _rms(hb, ng[:, sl]) * _silu(zb[:, sl])
    pad = [jnp.zeros((SUBLANES - ML_HEADS, LANES), F32)]
    n_scr[...] = jnp.concatenate(n_rows + pad, axis=0)
    m_scr[...] = jnp.concatenate(m_rows + pad, axis=0)


SSD_PARAMS = ("conv_a_w", "conv_a_b", "dtb_row", "alog_row", "dtb_col", "alog_col", "dskip_row", "ssd_norm_g")
MLSTM_PARAMS = ("conv_b_w", "conv_b_b", "w_mq_t", "w_mk", "w_mv_t", "gb_row", "gb_col", "ml_norm_g")


def _scan_kernel(xbc_ref, misc_ref, za_ref, xm_ref, o_ref, zb_ref, *rest):
    ns, nm = len(SSD_PARAMS), len(MLSTM_PARAMS)
    ssd_params, ml_params = rest[:ns], rest[ns:ns + nm]
    ya_ref, hfin_ref, sconv_ref, yb_ref, cfin_ref, nfin_ref, mfin_ref, mconv_ref = rest[ns + nm:ns + nm + 8]
    h_scr, sext_scr, y_scr, c_scr, n_scr, m_scr, mext_scr = rest[ns + nm + 8:]
    c = pl.program_id(1)
    nc = pl.num_programs(1)

    @pl.when(c == 0)
    def _():
        h_scr[...] = jnp.zeros_like(h_scr)
        sext_scr[0:SUBLANES, :] = jnp.zeros((SUBLANES, SSD_CONV_CH), F32)
        c_scr[...] = jnp.zeros_like(c_scr)
        n_scr[...] = jnp.zeros_like(n_scr)
        m_scr[...] = jnp.zeros_like(m_scr)
        mext_scr[0:SUBLANES, :] = jnp.zeros((SUBLANES, GROUP_W), F32)

    _ssd_chunk(xbc_ref, misc_ref, za_ref, *ssd_params, ya_ref, h_scr, sext_scr, y_scr)
    _mlstm_chunk(xm_ref, o_ref, zb_ref, misc_ref, *ml_params, yb_ref, c_scr, n_scr, m_scr, mext_scr)

    @pl.when(c == nc - 1)
    def _():
        hfin_ref[0] = h_scr[...]
        sconv_ref[0] = xbc_ref[0, CHUNK - (CONV_W - 1):CHUNK, :]
        cfin_ref[0] = c_scr[...]
        nfin_ref[0] = n_scr[...]
        mfin_ref[0] = m_scr[...]
        mconv_ref[0] = xm_ref[0, CHUNK - (CONV_W - 1):CHUNK, :]


def _scans_prompt(xbc, xm, o, z, misc, lp, layer):
    b, l, _ = xbc.shape
    nc = l // CHUNK
    row = lambda bi, ci: (bi, ci, 0)
    first = lambda nd: (lambda bi, ci: (bi,) + (0,) * (nd - 1))
    tokens = lambda w: pl.BlockSpec((1, CHUNK, w), row)
    pa, pa_specs = _layer_args(lp, layer, SSD_PARAMS)
    pb, pb_specs = _layer_args(lp, layer, MLSTM_PARAMS)
    outs = pl.pallas_call(
        _scan_kernel,
        grid=(b, nc),
        in_specs=[tokens(SSD_CONV_CH), tokens(MISC_W),
                  pl.BlockSpec((1, CHUNK, GROUP_W), lambda bi, ci: (bi, ci, 0)),
                  tokens(GROUP_W), tokens(GROUP_W),
                  pl.BlockSpec((1, CHUNK, GROUP_W), lambda bi, ci: (bi, ci, 1)),
                  ] + pa_specs + pb_specs,
        out_specs=[tokens(GROUP_W),
                   pl.BlockSpec((1, SSD_HEADS, SSD_HEAD_DIM, SSD_STATE), first(4)),
                   pl.BlockSpec((1, CONV_W - 1, SSD_CONV_CH), first(3)),
                   tokens(GROUP_W),
                   pl.BlockSpec((1, ML_HEADS, ML_HEAD_DIM, ML_HEAD_DIM), first(4)),
                   pl.BlockSpec((1, SUBLANES, LANES), first(3)),
                   pl.BlockSpec((1, SUBLANES, LANES), first(3)),
                   pl.BlockSpec((1, CONV_W - 1, GROUP_W), first(3))],
        out_shape=[jax.ShapeDtypeStruct((b, l, GROUP_W), F32),
                   jax.ShapeDtypeStruct((b, SSD_HEADS, SSD_HEAD_DIM, SSD_STATE), F32),
                   jax.ShapeDtypeStruct((b, CONV_W - 1, SSD_CONV_CH), F32),
                   jax.ShapeDtypeStruct((b, l, GROUP_W), F32),
                   jax.ShapeDtypeStruct((b, ML_HEADS, ML_HEAD_DIM, ML_HEAD_DIM), F32),
                   jax.ShapeDtypeStruct((b, SUBLANES, LANES), F32),
                   jax.ShapeDtypeStruct((b, SUBLANES, LANES), F32),
                   jax.ShapeDtypeStruct((b, CONV_W - 1, GROUP_W), F32)],
        scratch_shapes=[pltpu.VMEM((SSD_HEADS, SSD_HEAD_DIM, SSD_STATE), F32),
                        pltpu.VMEM((SUBLANES + CHUNK, SSD_CONV_CH), F32),
                        pltpu.VMEM((CHUNK, GROUP_W), F32),
                        pltpu.VMEM((ML_HEADS, ML_HEAD_DIM, ML_HEAD_DIM), F32),
                        pltpu.VMEM((SUBLANES, LANES), F32),
                        pltpu.VMEM((SUBLANES, LANES), F32),
                        pltpu.VMEM((SUBLANES + CHUNK, GROUP_W), F32)],
        compiler_params=_cparams(("parallel", "arbitrary")),
        name="scans_prompt",
    )(xbc, misc, z, xm, o, z, *pa, *pb)
    ya, ssm_h, ssm_buf, yb, cfin, nfin, mfin, mconv = outs
    return ya, ssm_h, ssm_buf, yb, cfin, nfin[:, :ML_HEADS, :], mfin[:, :ML_HEADS, 0], mconv


def _rope_rows(r, g, cos2, sinsgn):
    half = MLA_ROPE // 2
    rn = r * lax.rsqrt(jnp.sum(r * r, axis=-1, keepdims=True) * (1.0 / MLA_ROPE) + EPS) * g
    lane = lax.broadcasted_iota(jnp.int32, rn.shape, 1)
    r_sw = jnp.where(lane < half, pltpu.roll(rn, LANES - half, 1), pltpu.roll(rn, half, 1))
    return rn * cos2 + r_sw * sinsgn


def _mla_prep_kernel(cq_ref, ckv_ref, misc_ref, cos_ref, sin_ref, cos_t_ref, sin_t_ref, qg_ref, kvg_ref, wqn_ref,
                     wqr_ref, wuk_ref, qnn_ref, qnr_ref, knn_ref, knr_ref, q_ref, k_ref, v_ref, lat_ref, kr_ref,
                     *, transposed):
    cos2 = cos_ref[...]
    sinsgn = sin_ref[...]
    c_q = _rms(cq_ref[0], qg_ref[...]).astype(BF16)
    if not transposed:
        qn_all = _dot(c_q, wqn_ref[...])
    c_kv = _rms(ckv_ref[0], kvg_ref[...])
    lat_ref[0] = c_kv
    c_kv_b = c_kv.astype(BF16)
    if transposed:
        v_ref[0] = c_kv.T.astype(BF16)
    else:
        v_ref[0] = c_kv_b
    misc = misc_ref[0]
    lane = lax.broadcasted_iota(jnp.int32, misc.shape, 1)
    kr_raw = jnp.where(lane < MLA_ROPE, misc, 0.0)
    kr = _rope_rows(kr_raw, knr_ref[...], cos2, sinsgn)
    kr_ref[0] = kr[:, 0:MLA_ROPE]
    kr_b = kr[:, 0:MLA_ROPE].astype(BF16)
    kn_all = _dot(c_kv_b, wuk_ref[...])
    half = MLA_ROPE // 2
    col_rms = lambda x, g: x * lax.rsqrt(jnp.mean(x * x, axis=0, keepdims=True) + EPS) * g
    for h in range(MLA_HEADS):
        sl = slice(h * MLA_NOPE, (h + 1) * MLA_NOPE)
        if transposed:
            qn_t = col_rms(_dot_nt(wqn_ref[sl, :], c_q), qnn_ref[...])
            qr_t = col_rms(_dot_nt(wqr_ref[h], c_q), qnr_ref[...])
            qr_sw = jnp.concatenate([qr_t[half:], qr_t[:half]], axis=0)
            qr_t = qr_t * cos_t_ref[...] + qr_sw * sin_t_ref[...]
            q_ref[0, h, 0:MLA_NOPE, :] = qn_t.astype(BF16)
            q_ref[0, h, MLA_NOPE:MLA_QK, :] = qr_t.astype(BF16)
        else:
            qn = _rms(qn_all[:, sl], qnn_ref[...])
            qr = _rope_rows(_dot(c_q, wqr_ref[h]), qnr_ref[...], cos2, sinsgn)
            q_ref[0, h, :, 0:MLA_NOPE] = qn.astype(BF16)
            q_ref[0, h, :, MLA_NOPE:MLA_QK] = qr[:, 0:MLA_ROPE].astype(BF16)
        k_ref[0, h, :, 0:MLA_NOPE] = _rms(kn_all[:, sl], knn_ref[...]).astype(BF16)
        k_ref[0, h, :, MLA_NOPE:MLA_QK] = kr_b


def _mla_prep(cq, ckv, misc, rope_tabs, lp, layer, tm, transposed):
    b, l, _ = cq.shape
    cos2, sinsgn, cos_t, sin_t = rope_tabs
    if transposed:
        names = ("q_norm_g", "kv_norm_g", "w_uq_nope_t", "w_uq_rope_t", "w_uk", "qn_nope_g_col", "qn_rope_g_col",
                 "kn_nope_g", "kn_rope_g_pad")
    else:
        names = ("q_norm_g", "kv_norm_g", "w_uq_nope", "w_uq_rope", "w_uk", "qn_nope_g", "qn_rope_g_pad",
                 "kn_nope_g", "kn_rope_g_pad")
    params, param_specs = _layer_args(lp, layer, names)
    row = lambda bi, ti: (bi, ti, 0)
    hrow = lambda bi, ti: (bi, 0, ti, 0)
    if transposed:
        q_spec = pl.BlockSpec((1, MLA_HEADS, MLA_QK, tm), lambda bi, ti: (bi, 0, 0, ti))
        q_shape = jax.ShapeDtypeStruct((b, MLA_HEADS, MLA_QK, l), BF16)
        v_spec = pl.BlockSpec((1, KV_RANK, tm), lambda bi, ti: (bi, 0, ti))
        v_shape = jax.ShapeDtypeStruct((b, KV_RANK, l), BF16)
    else:
        q_spec = pl.BlockSpec((1, MLA_HEADS, tm, MLA_QK), hrow)
        q_shape = jax.ShapeDtypeStruct((b, MLA_HEADS, l, MLA_QK), BF16)
        v_spec = pl.BlockSpec((1, tm, KV_RANK), row)
        v_shape = jax.ShapeDtypeStruct((b, l, KV_RANK), BF16)
    return pl.pallas_call(
        functools.partial(_mla_prep_kernel, transposed=transposed),
        grid=(b, l // tm),
        in_specs=[pl.BlockSpec((1, tm, Q_RANK), row), pl.BlockSpec((1, tm, KV_RANK), row),
                  pl.BlockSpec((1, tm, MISC_W), row),
                  pl.BlockSpec((tm, LANES), lambda bi, ti: (ti, 0)),
                  pl.BlockSpec((tm, LANES), lambda bi, ti: (ti, 0)),
                  pl.BlockSpec((MLA_ROPE, tm), lambda bi, ti: (0, ti)),
                  pl.BlockSpec((MLA_ROPE, tm), lambda bi, ti: (0, ti))] + param_specs,
        out_specs=[q_spec,
                   pl.BlockSpec((1, MLA_HEADS, tm, MLA_QK), hrow),
                   v_spec,
                   pl.BlockSpec((1, tm, KV_RANK), row),
                   pl.BlockSpec((1, tm, MLA_ROPE), row)],
        out_shape=[q_shape,
                   jax.ShapeDtypeStruct((b, MLA_HEADS, l, MLA_QK), BF16),
                   v_shape,
                   jax.ShapeDtypeStruct((b, l, KV_RANK), F32),
                   jax.ShapeDtypeStruct((b, l, MLA_ROPE), F32)],
        compiler_params=_cparams(("parallel", "parallel")),
        name="mla_prep",
    )(cq, ckv, misc, cos2, sinsgn, cos_t, sin_t, *params)


def _mla_attn_kernel(qt_ref, k_ref, vt_ref, zc_ref, wuv_ref, out_ref, m_scr, l_scr, acc_scr, *, tile):
    qi = pl.program_id(1)
    scale = MLA_QK ** -0.5
    m_scr[...] = jnp.full_like(m_scr, -jnp.inf)
    l_scr[...] = jnp.zeros_like(l_scr)
    acc_scr[...] = jnp.zeros_like(acc_scr)

    def kv_block(start, width, diagonal):
        vt = vt_ref[0, :, pl.ds(start, width)]
        scores = [_dot(k_ref[0, h, pl.ds(start, width), :], qt_ref[0, h]) for h in range(MLA_HEADS)]
        for h in range(MLA_HEADS):
            s = scores[h] * scale
            if diagonal:
                kid = lax.broadcasted_iota(jnp.int32, (width, tile), 0)
                qid = lax.broadcasted_iota(jnp.int32, (width, tile), 1)
                s = jnp.where(kid <= qid, s, -jnp.inf)
            m_prev = m_scr[h:h + 1, :]
            m_new = jnp.maximum(m_prev, jnp.max(s, axis=0, keepdims=True))
            alpha = jnp.exp(m_prev - m_new)
            p = jnp.exp(s - m_new)
            l_scr[h:h + 1, :] = alpha * l_scr[h:h + 1, :] + jnp.sum(p, axis=0, keepdims=True)
            acc_scr[h] = alpha * acc_scr[h] + _dot(vt, p.astype(BF16))
            m_scr[h:h + 1, :] = m_new

    def pair_body(p, carry):
        kv_block(pl.multiple_of(p * 2 * tile, 2 * tile), 2 * tile, False)
        return carry

    def single_body(_, carry):
        kv_block(pl.multiple_of((qi - 1) * tile, tile), tile, False)
        return carry

    lax.fori_loop(0, qi // 2, pair_body, 0)
    lax.fori_loop(0, qi % 2, single_body, 0)
    kv_block(pl.multiple_of(qi * tile, tile), tile, True)
    for h in range(MLA_HEADS):
        sl = slice(h * MLA_V, (h + 1) * MLA_V)
        ctx_t = (acc_scr[h] / l_scr[h:h + 1, :]).astype(BF16)
        out_ref[0, :, sl] = _dot_tn(ctx_t, wuv_ref[:, sl]) * _silu(zc_ref[0, :, sl])


def _mla_attn_prompt(qt, k, latt, z, w_uv, layer, tile=256):
    b, hds, _, l = qt.shape
    kern = functools.partial(_mla_attn_kernel, tile=tile)
    return pl.pallas_call(
        kern,
        grid=(b, l // tile),
        in_specs=[pl.BlockSpec((1, hds, MLA_QK, tile), lambda bi, qi: (bi, 0, 0, qi)),
                  pl.BlockSpec((1, hds, l, MLA_QK), lambda bi, qi: (bi, 0, 0, 0)),
                  pl.BlockSpec((1, KV_RANK, l), lambda bi, qi: (bi, 0, 0)),
                  pl.BlockSpec((1, tile, GROUP_W), lambda bi, qi: (bi, qi, 2)),
                  _layer_spec(w_uv, layer)],
        out_specs=pl.BlockSpec((1, tile, GROUP_W), lambda bi, qi: (bi, qi, 0)),
        out_shape=jax.ShapeDtypeStruct((b, l, GROUP_W), F32),
        scratch_shapes=[pltpu.VMEM((SUBLANES, tile), F32), pltpu.VMEM((SUBLANES, tile), F32),
                        pltpu.VMEM((hds, KV_RANK, tile), F32)],
        compiler_params=_cparams(("parallel", "arbitrary")),
        name="mla_attn",
    )(qt, k, latt, z, w_uv)


def _memkv_kernel(mem_ref, g_ref, wk_ref, wv_ref, xkg_ref, mk_ref, mv_ref):
    mem_len = mem_ref.shape[1]
    mem_n = _rms(mem_ref[0], g_ref[0]).astype(BF16)
    kraw = _dot(mem_n, wk_ref[0])
    vraw = _dot(mem_n, wv_ref[0])
    for h in range(X_HEADS):
        sl = slice(h * X_HEAD_DIM, (h + 1) * X_HEAD_DIM)
        rows = pl.ds(h, mem_len, stride=X_HEADS)
        mk_ref[0, 0, rows, :] = _rms(kraw[:, sl], xkg_ref[0])
        mv_ref[0, 0, rows, :] = vraw[:, sl]


def _memkv(mem, mem_norm_g, w_mem_k, w_mem_v, xk_norm_g):
    b, m, _ = mem.shape
    depth = w_mem_k.shape[0]
    lsel = lambda li, bi: (li, 0, 0)
    return pl.pallas_call(
        _memkv_kernel,
        grid=(depth, b),
        in_specs=[pl.BlockSpec((1, m, D_MODEL), lambda li, bi: (bi, 0, 0)),
                  pl.BlockSpec((1, 1, D_MODEL), lsel),
                  pl.BlockSpec((1, D_MODEL, GROUP_W), lsel),
                  pl.BlockSpec((1, D_MODEL, GROUP_W), lsel),
                  pl.BlockSpec((1, 1, X_HEAD_DIM), lsel)],
        out_specs=[pl.BlockSpec((1, 1, m * X_HEADS, X_HEAD_DIM), lambda li, bi: (li, bi, 0, 0)),
                   pl.BlockSpec((1, 1, m * X_HEADS, X_HEAD_DIM), lambda li, bi: (li, bi, 0, 0))],
        out_shape=[jax.ShapeDtypeStruct((depth, b, m * X_HEADS, X_HEAD_DIM), F32),
                   jax.ShapeDtypeStruct((depth, b, m * X_HEADS, X_HEAD_DIM), F32)],
        compiler_params=_cparams(("parallel", "parallel")),
        name="memkv",
    )(mem, mem_norm_g, w_mem_k, w_mem_v, xk_norm_g)


def _xattn_kernel(xq_ref, mk_ref, mv_ref, zd_ref, g_ref, out_ref, *, rows, mem_len, nbatch):
    if rows == 1:
        for bb in range(nbatch):
            xq = xq_ref[bb]
            heads = [xq[:, h * X_HEAD_DIM:(h + 1) * X_HEAD_DIM] for h in range(X_HEADS)]
            qn = _rms(jnp.concatenate(heads + heads, axis=0), g_ref[...]).astype(BF16)
            s = _dot_nt(qn, mk_ref[0, bb].astype(BF16)) * (X_HEAD_DIM ** -0.5)
            row_head = lax.broadcasted_iota(jnp.int32, s.shape, 0) & (X_HEADS - 1)
            col_head = lax.broadcasted_iota(jnp.int32, s.shape, 1) & (X_HEADS - 1)
            s = jnp.where(row_head == col_head, s, -jnp.inf)
            e = jnp.exp(s - jnp.max(s, axis=1, keepdims=True))
            p = e / jnp.sum(e, axis=1, keepdims=True)
            y = _dot(p.astype(BF16), mv_ref[0, bb].astype(BF16))
            out_ref[bb] = jnp.concatenate([y[h:h + 1] for h in range(X_HEADS)], axis=1) * _silu(zd_ref[bb])
        return
    xq = xq_ref[0]
    zd = zd_ref[0]
    for h in range(X_HEADS):
        sl = slice(h * X_HEAD_DIM, (h + 1) * X_HEAD_DIM)
        mk_h = mk_ref[0, 0, pl.ds(h, mem_len, stride=X_HEADS), :]
        mv_h = mv_ref[0, 0, pl.ds(h, mem_len, stride=X_HEADS), :]
        qn = _rms(xq[:, sl], g_ref[...]).astype(BF16)
        s = _dot_nt(qn, mk_h.astype(BF16)) * (X_HEAD_DIM ** -0.5)
        s = s - jnp.max(s, axis=1, keepdims=True)
        e = jnp.exp(s)
        p = e / jnp.sum(e, axis=1, keepdims=True)
        y = _dot(p.astype(BF16), mv_h.astype(BF16))
        out_ref[0, :, sl] = y * _silu(zd[:, sl])


XATTN_DECODE_BATCH = 4


def _xattn(xq, mk, mv, layer, z, g, tm):
    b, l, _ = xq.shape
    mem_len = mk.shape[2] // X_HEADS
    nb = XATTN_DECODE_BATCH if tm == 1 else 1
    kern = functools.partial(_xattn_kernel, rows=tm, mem_len=mem_len, nbatch=nb)
    mem_spec = pl.BlockSpec((1, nb) + mk.shape[2:], lambda bi, ti: (layer, bi, 0, 0))
    return pl.pallas_call(
        kern,
        grid=(b // nb, l // tm),
        in_specs=[pl.BlockSpec((nb, tm, GROUP_W), lambda bi, ti: (bi, ti, 0)),
                  mem_spec, mem_spec,
                  pl.BlockSpec((nb, tm, GROUP_W), lambda bi, ti: (bi, ti, 3)),
                  _layer_spec(g, layer)],
        out_specs=pl.BlockSpec((nb, tm, GROUP_W), lambda bi, ti: (bi, ti, 0)),
        out_shape=jax.ShapeDtypeStruct((b, l, GROUP_W), F32),
        compiler_params=_cparams(("parallel", "parallel")),
        name="xattn",
    )(xq, mk, mv, z, g)


def _outproj_kernel(ya_ref, yb_ref, yc_ref, yd_ref, x_ref, w_ref, fg_ref, out_ref, *, final):
    acc = x_ref[...]
    for i, ref in enumerate((ya_ref, yb_ref, yc_ref, yd_ref)):
        acc = acc + _dot(ref[...].astype(BF16), w_ref[i * GROUP_W:(i + 1) * GROUP_W, :])
    if final:
        acc = _rms(acc, fg_ref[...])
    out_ref[...] = acc


def _outproj(ys, x2d, w_out, layer, final_g, tm, final):
    n = x2d.shape[0]
    kern = functools.partial(_outproj_kernel, final=final)
    yspec = pl.BlockSpec((tm, GROUP_W), lambda i: (i, 0))
    return pl.pallas_call(
        kern,
        grid=(n // tm,),
        in_specs=[yspec, yspec, yspec, yspec,
                  pl.BlockSpec((tm, D_MODEL), lambda i: (i, 0)),
                  _layer_spec(w_out, layer), _const_spec((1, D_MODEL))],
        out_specs=pl.BlockSpec((tm, D_MODEL), lambda i: (i, 0)),
        out_shape=jax.ShapeDtypeStruct((n, D_MODEL), F32),
        compiler_params=_cparams(("parallel",)),
        name="outproj",
    )(*ys, x2d, w_out, final_g)


def _col_form(row, n):
    return jnp.broadcast_to(row, (LANES, n)).T


N_DEC_STATES = 6


DEC_STATE_BATCH = 4


def _dec_state_kernel(*refs):
    params = refs[11:25]
    for bb in range(DEC_STATE_BATCH):
        one = lambda r: r.at[pl.ds(bb, 1)]
        _dec_state_one([one(r) for r in refs[0:5]], [one(r.at[0]) for r in refs[5:5 + N_DEC_STATES]], params,
                       [one(r) for r in refs[-8:-6]], [one(r.at[0]) for r in refs[-N_DEC_STATES:]])


def _dec_state_one(tokens, states, params, ys, new_states):
    xbc_ref, xm_ref, o_ref, z_ref, misc_ref = tokens
    hs_ref, sbuf_ref, cm_ref, nm_ref, mm_ref, mbuf_ref = states
    (cwa_ref, cba_ref, dtb_ref, alog_ref, hexp_ref, dskip_ref, nga_ref,
     cwb_ref, cbb_ref, wq_ref, wk_ref, wv_ref, gb_ref, ngb_ref) = params
    ya_ref, yb_ref = ys
    hs_out, sbuf_out, cm_out, nm_out, mm_out, mbuf_out = new_states
    misc = misc_ref[0]
    z = z_ref[0]

    u = xbc_ref[0]
    buf = sbuf_ref[0]
    cw = cwa_ref[...]
    conv = cba_ref[...] + cw[3:4] * u
    for j in range(CONV_W - 1):
        conv = conv + cw[j:j + 1] * buf[j:j + 1]
    sbuf_out[0, 0:CONV_W - 2, :] = buf[1:CONV_W - 1]
    sbuf_out[0, CONV_W - 2:CONV_W - 1, :] = u
    xbcs = _silu(conv)
    xa = xbcs[:, 0:GROUP_W]
    dt = _softplus(misc + dtb_ref[...])
    a = -jnp.exp(alog_ref[...])
    hexp = hexp_ref[...]
    dt8 = jnp.broadcast_to(dt, (SUBLANES, LANES))
    da8 = jnp.broadcast_to(dt * a, (SUBLANES, LANES))
    dt_x = jnp.dot(dt8, hexp, precision=HIGHEST, preferred_element_type=F32)[0:1]
    decay_x = jnp.exp(jnp.dot(da8, hexp, precision=HIGHEST, preferred_element_type=F32)[0:1])
    hstate = hs_ref[0].reshape(SSD_HEADS * SSD_HEAD_DIM, SSD_STATE)
    gw = GROUP_W // SSD_GROUPS
    c_off = GROUP_W + SSD_GROUPS * SSD_STATE
    dtx_col = _col_form(dt_x * xa, GROUP_W)
    decay_col = _col_form(decay_x, GROUP_W)
    y_parts = []
    for g in range(SSD_GROUPS):
        b_g = xbcs[:, GROUP_W + g * SSD_STATE:GROUP_W + (g + 1) * SSD_STATE]
        c_g = xbcs[:, c_off + g * SSD_STATE:c_off + (g + 1) * SSD_STATE]
        cb = jnp.sum(c_g * b_g, axis=1, keepdims=True)
        rows = slice(g * gw, (g + 1) * gw)
        h_g = hstate[rows]
        c8 = jnp.broadcast_to(c_g, (SUBLANES, SSD_STATE)).astype(BF16)
        ch = _dot_nt(c8, h_g.astype(BF16))[0:1]
        y_parts.append(cb * dt_x[:, rows] * xa[:, rows] + ch * decay_x[:, rows])
        h_new = decay_col[rows] * h_g + dtx_col[rows] * b_g
        hs_out[0, g * SSD_HPG:(g + 1) * SSD_HPG] = h_new.reshape(SSD_HPG, SSD_HEAD_DIM, SSD_STATE)
    y = jnp.concatenate(y_parts, axis=1) + dskip_ref[...] * xa
    y = y * _silu(z[:, 0:GROUP_W])
    nga = nga_ref[...]
    for g in range(SSD_GROUPS):
        rows = slice(g * gw, (g + 1) * gw)
        ya_ref[0, :, rows] = _rms(y[:, rows], nga[:, rows])

    um = xm_ref[0]
    mbuf = mbuf_ref[0]
    cwb = cwb_ref[...]
    convb = cbb_ref[...] + cwb[3:4] * um
    for j in range(CONV_W - 1):
        convb = convb + cwb[j:j + 1] * mbuf[j:j + 1]
    mbuf_out[0, 0:CONV_W - 2, :] = mbuf[1:CONV_W - 1]
    mbuf_out[0, CONV_W - 2:CONV_W - 1, :] = um
    xmc = _silu(convb)
    gates = misc + gb_ref[...]
    lf_all = -_softplus(-gates)
    mm = mm_ref[0]
    ngb = ngb_ref[...]
    o_all = o_ref[0]
    for h in range(ML_HEADS):
        sl = slice(h * ML_HEAD_DIM, (h + 1) * ML_HEAD_DIM)
        li = gates[:, MISC_I + h:MISC_I + h + 1]
        lf = lf_all[:, MISC_F + h:MISC_F + h + 1]
        m_prev = mm[:, h:h + 1]
        l_prev = lf + m_prev
        m_t = jnp.maximum(l_prev, li)
        xmc8 = jnp.broadcast_to(xmc[:, sl], (SUBLANES, ML_HEAD_DIM)).astype(BF16)
        um8 = jnp.broadcast_to(um[:, sl], (SUBLANES, ML_HEAD_DIM)).astype(BF16)
        q8 = _dot(xmc8, wq_ref[h])
        q = q8[0:1]
        k = _dot(xmc8, wk_ref[h])[0:1] * (ML_HEAD_DIM ** -0.5)
        v = _dot(um8, wv_ref[h])[0:1]
        qk = jnp.sum(q * k, axis=1, keepdims=True)
        sc = qk * jnp.exp(li - m_t)
        g_prev = jnp.exp(l_prev - m_t)
        cs = cm_ref[0, h]
        ns = nm_ref[0, h:h + 1, :]
        cq = _dot_nt(q8.astype(BF16), cs.astype(BF16))[0:1]
        num = sc * v + g_prev * cq
        den = sc + g_prev * jnp.sum(ns * q, axis=1, keepdims=True)
        hout = num / jnp.maximum(jnp.abs(den), jnp.exp(-m_t))
        wend = jnp.exp(li - m_t)
        decay = jnp.exp(l_prev - m_t)
        cm_out[0, h] = decay * cs + _col_form(wend * v, ML_HEAD_DIM) * k
        nm_out[0, h:h + 1, :] = decay * ns + wend * k
        mm_out[0, :, h:h + 1] = m_t
        hb = jax.nn.sigmoid(o_all[:, sl]) * hout
        yb_ref[0, :, sl] = _rms(hb, ngb[:, sl]) * _silu(z[:, GROUP_W + h * ML_HEAD_DIM:GROUP_W + (h + 1) * ML_HEAD_DIM])


def _dec_state(xbc, xm, o, z, misc, states, layer, lp):
    b = xbc.shape[0]
    nb = DEC_STATE_BATCH
    tok = lambda w: pl.BlockSpec((nb, 1, w), lambda bi: (bi, 0, 0))
    at_layer = lambda t: pl.BlockSpec((1, nb) + t.shape[2:], lambda bi: (layer, bi) + (0,) * (t.ndim - 2))
    state_specs = [at_layer(t) for t in states]
    pa, pa_specs = _layer_args(lp, layer, ("conv_a_w", "conv_a_b", "dtb_row", "alog_row"))
    pb, pb_specs = _layer_args(lp, layer, ("dskip_row", "ssd_norm_g", "conv_b_w", "conv_b_b", "w_mq", "w_mk", "w_mv",
                                           "gb_row", "ml_norm_g"))
    params = pa + [lp["head_expand"]] + pb
    param_specs = pa_specs + [_const_spec((LANES, GROUP_W))] + pb_specs
    inputs = [xbc, xm, o, z, misc] + list(states) + params
    in_specs = [tok(SSD_CONV_CH), tok(GROUP_W), tok(GROUP_W), tok(D_MIX), tok(MISC_W)] + state_specs + param_specs
    aliases = {5 + i: 2 + i for i in range(N_DEC_STATES)}
    outs = pl.pallas_call(
        _dec_state_kernel,
        grid=(b // nb,),
        in_specs=in_specs,
        out_specs=[tok(GROUP_W), tok(GROUP_W)] + state_specs,
        out_shape=[jax.ShapeDtypeStruct((b, 1, GROUP_W), F32)] * 2
        + [jax.ShapeDtypeStruct(t.shape, F32) for t in states],
        input_output_aliases=aliases,
        compiler_params=_cparams(("parallel",)),
        name="dec_state",
    )(*inputs)
    return outs[0], outs[1], tuple(outs[2:])


def _mla_decode_kernel(pt_ref, lat_hbm, kr_hbm, qn_ref, qr_ref, qfull_ref, kfull_ref, vnew_ref, zc_ref,
                       wukt_ref, wuv_ref, kng_ref, out_ref,
                       lat_buf, kr_buf, sem, qa_scr, m_scr, l_scr, acc_scr, *, layer, pps, page, nsub):
    b = pl.program_id(0)
    s = pl.program_id(1)
    ns = pl.num_programs(1)
    nb = pl.num_programs(0)
    step = b * ns + s
    last = nb * ns - 1
    slot = step % DECODE_SLOTS
    sub = pps * page // nsub

    def copies(bb, ss, sl):
        out = []
        for p in range(pps):
            pg = pt_ref[bb, ss * pps + p]
            out.append(pltpu.make_async_copy(lat_hbm.at[layer, pg], lat_buf.at[sl, pl.ds(p * page, page), :],
                                             sem.at[0, sl]))
            out.append(pltpu.make_async_copy(kr_hbm.at[layer, pg], kr_buf.at[sl, :, pl.ds(p * page, page)],
                                             sem.at[1, sl]))
        return out

    def start_step(target, sl):
        tgt = jnp.minimum(target, last)
        for cp in copies(tgt // ns, tgt % ns, sl):
            cp.start()

    @pl.when(step == 0)
    def _():
        start_step(step, 0)
        start_step(step + 1, 1)

    scale = MLA_QK ** -0.5

    @pl.when(s == 0)
    def _():
        qng = (qn_ref[0].astype(F32) * kng_ref[...]).astype(BF16)
        for h in range(MLA_HEADS):
            full = _dot(qng, wukt_ref[h * MLA_NOPE:(h + 1) * MLA_NOPE, :])
            qa_scr[h:h + 1, :] = full[h:h + 1]
            qa_scr[MLA_HEADS + h:MLA_HEADS + h + 1, :] = full[h:h + 1]
        m_scr[...] = jnp.full_like(m_scr, -jnp.inf)
        l_scr[...] = jnp.zeros_like(l_scr)
        acc_scr[...] = jnp.zeros_like(acc_scr)

    for cp in copies(b, s, slot):
        cp.wait()

    qa = qa_scr[...].astype(BF16)
    qr = qr_ref[0]
    wukt = wukt_ref[...]

    def scores(i):
        latb = lat_buf[slot, i * sub:(i + 1) * sub, :].astype(BF16)
        krb = kr_buf[slot, :, i * sub:(i + 1) * sub].astype(BF16)
        kraw_t = _dot_nt(wukt, latb)
        sa = _dot_nt(qa, latb)
        sr = _dot(qr, krb)
        rinv_rows = []
        for h in range(MLA_HEADS):
            kh = kraw_t[h * MLA_NOPE:(h + 1) * MLA_NOPE, :]
            ssq = jnp.sum(kh * kh, axis=0, keepdims=True)
            rinv_rows.append(lax.rsqrt(ssq * (1.0 / MLA_NOPE) + EPS))
        rinv = jnp.concatenate(rinv_rows + rinv_rows, axis=0)
        return latb, (sa * rinv + sr) * scale

    m = m_scr[...]
    l = l_scr[...]
    acc = acc_scr[...]
    pending = scores(0)
    for i in range(nsub):
        latb, sc = pending
        if i + 1 < nsub:
            pending = scores(i + 1)
        m_new = jnp.maximum(m, jnp.max(sc, axis=1, keepdims=True))
        alpha = jnp.exp(m - m_new)
        p = jnp.exp(sc - m_new)
        l = alpha * l + jnp.sum(p, axis=1, keepdims=True)
        acc = alpha * acc + _dot(p.astype(BF16), latb)
        m = m_new
    m_scr[...] = m
    l_scr[...] = l
    acc_scr[...] = acc

    start_step(step + 2, (step + 2) % DECODE_SLOTS)

    @pl.when(s == ns - 1)
    def _():
        s_new = jnp.sum(qfull_ref[0].astype(F32) * kfull_ref[0].astype(F32), axis=1, keepdims=True) * scale
        m_fin = jnp.maximum(m, s_new)
        alpha2 = jnp.exp(m - m_fin)
        p_new = jnp.exp(s_new - m_fin)
        l_fin = alpha2 * l + p_new
        acc_fin = alpha2 * acc + p_new * vnew_ref[0].astype(F32)
        ctx = (acc_fin / l_fin).astype(BF16)
        yfull = _dot(ctx, wuv_ref[...])
        parts = [yfull[h:h + 1, h * MLA_V:(h + 1) * MLA_V] for h in range(MLA_HEADS)]
        out_ref[0] = jnp.concatenate(parts, axis=1) * _silu(zc_ref[0])

    @pl.when(step == last)
    def _():
        for ahead in (1, 2):
            for cp in copies(b, s, (step + ahead) % DECODE_SLOTS):
                cp.wait()


DECODE_PAGES_PER_STEP = 64
DECODE_SUB_BLOCKS = 2
DECODE_SLOTS = 3


def _mla_decode(page_table, cache_lat, cache_kr, layer, qn8, qr8, qfull8, kfull8, vnew, z, lp,
                pps=DECODE_PAGES_PER_STEP, nsub=DECODE_SUB_BLOCKS):
    b, n_pages = page_table.shape
    page = cache_lat.shape[2]
    ns = n_pages // pps
    t = pps * page
    kern = functools.partial(_mla_decode_kernel, layer=layer, pps=pps, page=page, nsub=nsub)
    per_b = lambda shape: pl.BlockSpec((1,) + shape, lambda bi, si, pt: (bi, 0, 0))
    params, param_specs = _layer_args(lp, layer, ("w_uk_t", "w_uv", "kn_nope_g"))
    grid_spec = pltpu.PrefetchScalarGridSpec(
        num_scalar_prefetch=1,
        grid=(b, ns),
        in_specs=[pl.BlockSpec(memory_space=pl.ANY), pl.BlockSpec(memory_space=pl.ANY),
                  per_b((SUBLANES, MLA_NOPE)), per_b((SUBLANES, MLA_ROPE)),
                  per_b((SUBLANES, MLA_QK)), per_b((SUBLANES, MLA_QK)), per_b((1, KV_RANK)),
                  pl.BlockSpec((1, 1, GROUP_W), lambda bi, si, pt: (bi, 0, 2)),
                  ] + param_specs,
        out_specs=pl.BlockSpec((1, 1, GROUP_W), lambda bi, si, pt: (bi, 0, 0)),
        scratch_shapes=[pltpu.VMEM((DECODE_SLOTS, t, KV_RANK), F32), pltpu.VMEM((DECODE_SLOTS, MLA_ROPE, t), F32),
                        pltpu.SemaphoreType.DMA((2, DECODE_SLOTS)),
                        pltpu.VMEM((SUBLANES, KV_RANK), F32),
                        pltpu.VMEM((SUBLANES, 1), F32), pltpu.VMEM((SUBLANES, 1), F32),
                        pltpu.VMEM((SUBLANES, KV_RANK), F32)],
    )
    return pl.pallas_call(
        kern,
        grid_spec=grid_spec,
        out_shape=jax.ShapeDtypeStruct((b, 1, GROUP_W), F32),
        compiler_params=_cparams(("arbitrary", "arbitrary")),
        name="mla_decode",
    )(page_table, cache_lat, cache_kr, qn8, qr8, qfull8, kfull8, vnew, z, *params)


def _pack_w_in(w_in):
    names = ("z", "xbc", "dt", "xm", "i", "f", "o", "cq", "ckv", "kr", "xq")
    offs = {}
    o = 0
    for nme, w in zip(names, IN_SIZES):
        offs[nme] = (o, o + w)
        o += w
    col = lambda nme: w_in[:, :, offs[nme][0]:offs[nme][1]].astype(BF16)
    pad = jnp.zeros(w_in.shape[:2] + (MISC_W - (MISC_F + ML_HEADS),), BF16)
    misc = jnp.concatenate([col("kr"), col("dt"), col("i"), col("f"), pad], axis=-1)
    return [misc if nme == "misc" else col(nme) for nme, _ in PACK_SEGS]


def _stack_params(p):
    depth = p["norm_g"].shape[0]
    row = lambda a: a[:, None, :]
    lp = {}
    for nme in ("norm_g", "conv_a_b", "ssd_norm_g", "conv_b_b", "ml_norm_g", "q_norm_g", "kv_norm_g", "qn_nope_g",
                "kn_nope_g", "xq_norm_g"):
        lp[nme] = row(p[nme])
    lp["conv_a_w"] = p["conv_a_w"]
    lp["conv_b_w"] = p["conv_b_w"]
    lp["dtb_row"] = _lane_pad(p["dt_bias"], MISC_DT)
    lp["alog_row"] = _lane_pad(p["a_log"], MISC_DT)
    lp["dtb_col"] = p["dt_bias"][:, :, None]
    lp["alog_col"] = p["a_log"][:, :, None]
    lp["dskip_row"] = row(jnp.repeat(p["d_skip"], SSD_HEAD_DIM, axis=1))
    lp["w_mq"] = p["w_mq"].astype(BF16)
    lp["w_mk"] = p["w_mk"].astype(BF16)
    lp["w_mv"] = p["w_mv"].astype(BF16)
    lp["w_mq_t"] = jnp.swapaxes(lp["w_mq"], 2, 3)
    lp["w_mv_t"] = jnp.swapaxes(lp["w_mv"], 2, 3)
    gate_b = jnp.concatenate([p["b_i"], p["b_f"]], axis=1)
    lp["gb_row"] = _lane_pad(gate_b, MISC_I)
    lp["gb_col"] = gate_b[:, :, None]
    w_uq = p["w_uq"]
    lp["w_uq_nope"] = w_uq[..., :MLA_NOPE].reshape(depth, Q_RANK, MLA_HEADS * MLA_NOPE).astype(BF16)
    rope_pad = lambda t: jnp.pad(t, [(0, 0)] * (t.ndim - 1) + [(0, LANES - MLA_ROPE)])
    lp["w_uq_rope"] = rope_pad(jnp.transpose(w_uq[..., MLA_NOPE:], (0, 2, 1, 3))).astype(BF16)
    w_uk = p["w_uk"].reshape(depth, KV_RANK, MLA_HEADS * MLA_NOPE)
    lp["w_uk"] = w_uk.astype(BF16)
    lp["w_uk_t"] = jnp.swapaxes(w_uk, 1, 2).astype(BF16)
    lp["w_uv"] = p["w_uv"].reshape(depth, KV_RANK, MLA_HEADS * MLA_V).astype(BF16)
    lp["w_uq_nope_t"] = jnp.swapaxes(lp["w_uq_nope"], 1, 2)
    lp["w_uq_rope_t"] = jnp.transpose(w_uq[..., MLA_NOPE:], (0, 2, 3, 1)).astype(BF16)
    lp["qn_nope_g_col"] = p["qn_nope_g"][:, :, None]
    lp["qn_rope_g_col"] = p["qn_rope_g"][:, :, None]
    lp["qn_rope_g_pad"] = rope_pad(row(p["qn_rope_g"]))
    lp["kn_rope_g_pad"] = rope_pad(row(p["kn_rope_g"]))
    lp["w_out"] = p["w_out"].astype(BF16)
    head_of_lane = jnp.arange(LANES)[:, None] - MISC_DT
    head_of_chan = jnp.arange(GROUP_W)[None, :] // SSD_HEAD_DIM
    lp["head_expand"] = (head_of_lane == head_of_chan).astype(F32)
    return lp


def _rope_tables(pos):
    half = MLA_ROPE // 2
    inv = ROPE_BASE ** (-jnp.arange(half, dtype=F32) * 2.0 / MLA_ROPE)
    ang = pos.astype(F32)[:, None] * inv[None, :]
    cos = jnp.cos(ang)
    sin = jnp.sin(ang)
    zero = jnp.zeros((pos.shape[0], LANES - MLA_ROPE), F32)
    cos2 = jnp.concatenate([cos, cos], axis=1)
    sinsgn = jnp.concatenate([-sin, sin], axis=1)
    return jnp.concatenate([cos2, zero], axis=1), jnp.concatenate([sinsgn, zero], axis=1), cos2.T, sinsgn.T


def _prompt_layer(lp, w_slabs, layer, x, mk_all, mv_all, rope_tabs, final_g, final):
    b, l, _ = x.shape
    n = b * l
    x2d = x.reshape(n, D_MODEL)
    z, xbc, xm, o, cq, ckv, xq, misc = _inproj(x2d, lp["norm_g"], w_slabs, layer, tm=512)
    r3 = lambda t: t.reshape(b, l, t.shape[-1])
    z, xbc, xm, o, cq, ckv, xq, misc = map(r3, (z, xbc, xm, o, cq, ckv, xq, misc))
    ya, ssm_h, ssm_buf, yb, ml_c, ml_n, ml_m, ml_buf = _scans_prompt(xbc, xm, o, z, misc, lp, layer)
    qt, k, latt, lat, kr = _mla_prep(cq, ckv, misc, rope_tabs, lp, layer, tm=min(l, 1024), transposed=True)
    yc = _mla_attn_prompt(qt, k, latt, z, lp["w_uv"], layer)
    yd = _xattn(xq, mk_all, mv_all, layer, z, lp["xq_norm_g"], tm=min(l, 2048))
    ys = [t.reshape(n, GROUP_W) for t in (ya, yb, yc, yd)]
    x_new = _outproj(ys, x2d, lp["w_out"], layer, final_g, tm=1024, final=final).reshape(b, l, D_MODEL)
    return x_new, (lat, kr, ssm_h, ssm_buf, ml_c, ml_n, ml_m, ml_buf)


def _sample_layer(lp, w_slabs, layer, x, mem_k, mem_v, cache_lat, cache_kr, page_table, states,
                  rope_tabs, final_g, final, pps=DECODE_PAGES_PER_STEP):
    b = x.shape[0]
    x2d = x.reshape(b, D_MODEL)
    z, xbc, xm, o, cq, ckv, xq, misc = _inproj(x2d, lp["norm_g"], w_slabs, layer, tm=b)
    tok = lambda t: t.reshape(b, 1, t.shape[-1])
    ya, yb, st_new = _dec_state(tok(xbc), tok(xm), tok(o), tok(z), tok(misc), states, layer, lp)
    one = lambda t: t.reshape(1, b, t.shape[-1])
    q, k, latb, lat, kr = _mla_prep(one(cq), one(ckv), one(misc), rope_tabs, lp, layer, tm=b, transposed=False)
    qh = jnp.transpose(q[0], (1, 0, 2))
    kh = jnp.transpose(k[0], (1, 0, 2))
    dup = lambda t: jnp.concatenate([t, t], axis=1)
    qfull8 = dup(qh)
    kfull8 = dup(kh)
    qn8 = qfull8[:, :, :MLA_NOPE]
    qr8 = qfull8[:, :, MLA_NOPE:]
    vnew = latb.reshape(b, 1, KV_RANK)
    yc = _mla_decode(page_table, cache_lat, cache_kr, layer, qn8, qr8, qfull8, kfull8, vnew, tok(z), lp, pps=pps)
    yd = _xattn(tok(xq), mem_k, mem_v, layer, tok(z), lp["xq_norm_g"], tm=1)
    ys = [t.reshape(b, GROUP_W) for t in (ya, yb, yc, yd)]
    x_new = _outproj(ys, x2d, lp["w_out"], layer, final_g, tm=b, final=final).reshape(b, 1, D_MODEL)
    return x_new, lat.reshape(b, 1, KV_RANK), kr.reshape(b, 1, MLA_ROPE), st_new


def kernel(x_prompt, x_sample, mem_prompt, cache_mla_latent, cache_mla_krope, cache_mem_k, cache_mem_v, state_ssm, state_ssm_conv, state_mlstm_C, state_mlstm_n, state_mlstm_m, state_mlstm_conv, page_table, norm_g, w_in, conv_a_w, conv_a_b, dt_bias, a_log, d_skip, ssd_norm_g, conv_b_w, conv_b_b, w_mq, w_mk, w_mv, b_i, b_f, ml_norm_g, q_norm_g, kv_norm_g, w_uq, w_uk, w_uv, qn_nope_g, qn_rope_g, kn_nope_g, kn_rope_g, mem_norm_g, w_mem_k, w_mem_v, xk_norm_g, xq_norm_g, w_out, final_norm_g):
    params = dict(norm_g=norm_g, conv_a_w=conv_a_w, conv_a_b=conv_a_b, dt_bias=dt_bias, a_log=a_log, d_skip=d_skip,
                  ssd_norm_g=ssd_norm_g, conv_b_w=conv_b_w, conv_b_b=conv_b_b, w_mq=w_mq, w_mk=w_mk, w_mv=w_mv,
                  b_i=b_i, b_f=b_f, ml_norm_g=ml_norm_g, q_norm_g=q_norm_g, kv_norm_g=kv_norm_g, w_uq=w_uq,
                  w_uk=w_uk, w_uv=w_uv, qn_nope_g=qn_nope_g, qn_rope_g=qn_rope_g, kn_nope_g=kn_nope_g,
                  kn_rope_g=kn_rope_g, xq_norm_g=xq_norm_g, w_out=w_out)
    depth = w_in.shape[0]
    bp, lp_len, _ = x_prompt.shape
    bs, ls_len, _ = x_sample.shape
    n_pages = page_table.shape[1]
    past_len = n_pages * cache_mla_latent.shape[2]
    mem_len = mem_prompt.shape[1]
    w_slabs = _pack_w_in(w_in)
    final_g = final_norm_g[None, :]

    cache_kr_t = jnp.swapaxes(cache_mla_krope, 2, 3)
    rope_p = _rope_tables(jnp.arange(lp_len, dtype=jnp.int32))
    rope_s = _rope_tables(jnp.full((bs,), past_len, dtype=jnp.int32))

    mk_all, mv_all = _memkv(mem_prompt, mem_norm_g[:, None, :], w_mem_k.astype(BF16), w_mem_v.astype(BF16),
                            xk_norm_g[:, None, :])

    mem_rows = cache_mem_k.shape[2] * X_HEADS
    cache_mk = cache_mem_k.reshape(depth, bs, mem_rows, X_HEAD_DIM)
    cache_mv = cache_mem_v.reshape(depth, bs, mem_rows, X_HEAD_DIM)
    states = (state_ssm, state_ssm_conv, state_mlstm_C, state_mlstm_n,
              state_mlstm_m.reshape(depth, bs, 1, ML_HEADS), state_mlstm_conv)

    hp, hs = x_prompt, x_sample
    p_states, s_lats, s_krs = [], [], []
    lp = _stack_params(params)
    for l in range(depth):
        final = l == depth - 1
        hp, st_p = _prompt_layer(lp, w_slabs, l, hp, mk_all, mv_all, rope_p, final_g, final)
        p_states.append(st_p)
        hs, s_lat_l, s_kr_l, states = _sample_layer(lp, w_slabs, l, hs, cache_mk, cache_mv, cache_mla_latent,
                                                    cache_kr_t, page_table, states, rope_s, final_g, final)
        s_lats.append(s_lat_l)
        s_krs.append(s_kr_l)

    p_out = [jnp.stack([st[i] for st in p_states]) for i in range(8)]
    p_mem_k = mk_all.reshape(depth, bp, mem_len, X_HEADS, X_HEAD_DIM)
    p_mem_v = mv_all.reshape(depth, bp, mem_len, X_HEADS, X_HEAD_DIM)
    (p_lat, p_kr, p_ssm, p_sconv, p_mc, p_mn, p_mm, p_mconv) = p_out
    s_ssm, s_sconv, s_mc, s_mn, s_mm, s_mconv = states
    return (hp, hs, p_lat, p_kr, p_mem_k, p_mem_v, p_ssm, p_sconv, p_mc, p_mn, p_mm, p_mconv,
            jnp.stack(s_lats), jnp.stack(s_krs), s_ssm, s_sconv, s_mc, s_mn, s_mm.reshape(depth, bs, ML_HEADS),
            s_mconv)
```

```python
import functools
import math

import jax
import jax.numpy as jnp
from jax import lax
from jax.experimental import pallas as pl
from jax.experimental.pallas import tpu as pltpu

F32 = jnp.float32
BF16 = jnp.bfloat16
HIGHEST = lax.Precision.HIGHEST

D_MODEL = 1024
D_MIX = 2 * D_MODEL
GROUP_W = D_MIX // 4
SSD_HEAD_DIM = 64
SSD_HEADS = GROUP_W // SSD_HEAD_DIM
SSD_GROUPS = 2
SSD_HPG = SSD_HEADS // SSD_GROUPS
SSD_STATE = 128
CONV_W = 4
SSD_CONV_CH = GROUP_W + 2 * SSD_GROUPS * SSD_STATE
ML_HEADS = 4
ML_HEAD_DIM = GROUP_W // ML_HEADS
MLA_HEADS = 4
MLA_NOPE = 128
MLA_ROPE = 64
MLA_QK = MLA_NOPE + MLA_ROPE
MLA_V = GROUP_W // MLA_HEADS
Q_RANK = 3 * D_MODEL // 8
KV_RANK = D_MODEL // 4
ROPE_BASE = 10000.0
X_HEADS = 4
X_HEAD_DIM = GROUP_W // X_HEADS
CHUNK = 256
EPS = 1e-6
IN_SIZES = (D_MIX, SSD_CONV_CH, SSD_HEADS, GROUP_W, ML_HEADS, ML_HEADS, GROUP_W, Q_RANK, KV_RANK, MLA_ROPE, GROUP_W)

LANES = 128
SUBLANES = 8
VMEM_LIMIT = 56 * 1024 * 1024

MISC_W = LANES
MISC_KR = 0
MISC_DT = MLA_ROPE
MISC_I = MISC_DT + SSD_HEADS
MISC_F = MISC_I + ML_HEADS
PACK_SEGS = (("z", D_MIX), ("xbc", SSD_CONV_CH), ("xm", GROUP_W), ("o", GROUP_W), ("cq", Q_RANK),
             ("ckv", KV_RANK), ("xq", GROUP_W), ("misc", MISC_W))
PACK_W = sum(w for _, w in PACK_SEGS)


def _cparams(sem):
    return pltpu.CompilerParams(dimension_semantics=sem, vmem_limit_bytes=VMEM_LIMIT)


def _dot(a, b):
    return jnp.dot(a, b, preferred_element_type=F32)


def _dot_nt(a, b, precision=None):
    return lax.dot_general(a, b, (((1,), (1,)), ((), ())), preferred_element_type=F32, precision=precision)


def _dot_tn(a, b):
    return lax.dot_general(a, b, (((0,), (0,)), ((), ())), preferred_element_type=F32)


def _rms(x, g):
    return x * lax.rsqrt(jnp.mean(x * x, axis=-1, keepdims=True) + EPS) * g


def _silu(x):
    return x * jax.nn.sigmoid(x)


def _softplus(x):
    return jnp.maximum(x, 0.0) + jnp.log1p(jnp.exp(-jnp.abs(x)))


def _const_spec(shape):
    nd = len(shape)
    return pl.BlockSpec(shape, lambda *_: (0,) * nd)


def _layer_spec(arr, layer):
    nd = arr.ndim
    return pl.BlockSpec((None,) + arr.shape[1:], lambda *_: (layer,) + (0,) * (nd - 1))


def _layer_args(lp, layer, names):
    return [lp[n] for n in names], [_layer_spec(lp[n], layer) for n in names]


def _inproj_kernel(x_ref, g_ref, *refs):
    nseg = len(PACK_SEGS)
    w_refs, out_refs = refs[:nseg], refs[nseg:]
    x = x_ref[...]
    u = _rms(x, g_ref[...]).astype(BF16)
    for w_ref, out_ref in zip(w_refs, out_refs):
        out_ref[...] = _dot(u, w_ref[0])


def _inproj(x2d, g, w_slabs, layer, tm):
    n = x2d.shape[0]
    out_shape = [jax.ShapeDtypeStruct((n, w), F32) for _, w in PACK_SEGS]
    out_specs = [pl.BlockSpec((tm, w), lambda i: (i, 0)) for _, w in PACK_SEGS]
    w_specs = [pl.BlockSpec((1, D_MODEL, w), lambda i: (layer, 0, 0), pipeline_mode=pl.Buffered(1))
               for _, w in PACK_SEGS]
    return pl.pallas_call(
        _inproj_kernel,
        grid=(n // tm,),
        in_specs=[pl.BlockSpec((tm, D_MODEL), lambda i: (i, 0)), _layer_spec(g, layer)] + w_specs,
        out_specs=out_specs,
        out_shape=out_shape,
        compiler_params=_cparams(("parallel",)),
        name="inproj",
    )(x2d, g, *w_slabs)


def _tri_masks(n):
    r = lax.broadcasted_iota(jnp.int32, (n, n), 0)
    c = lax.broadcasted_iota(jnp.int32, (n, n), 1)
    return r >= c, c >= r


SSD_SUB_CHUNK = CHUNK


def _ssd_chunk(xbc_ref, misc_ref, za_ref, cw_ref, cb_ref, dtb_row_ref, alog_row_ref, dtb_col_ref,
               alog_col_ref, dskip_ref, ng_ref, ya_ref, h_scr, ext_scr, y_scr):
    u = xbc_ref[0]
    ext_scr[SUBLANES:SUBLANES + CHUNK, :] = u
    cw = cw_ref[...]
    conv = cb_ref[...] + cw[3:4] * u
    for j in range(CONV_W - 1):
        back = CONV_W - 1 - j
        conv = conv + cw[j:j + 1] * ext_scr[SUBLANES - back:SUBLANES - back + CHUNK, :]
    ext_scr[0:SUBLANES, :] = u[CHUNK - SUBLANES:CHUNK, :]
    xbcs = _silu(conv)
    xa = xbcs[:, 0:GROUP_W]

    misc = misc_ref[0]
    misc_t = misc.T
    n = SSD_SUB_CHUNK
    lower, upper = _tri_masks(n)
    lower_f = lower.astype(F32)
    upper_f = upper.astype(F32)
    dt_col = _softplus(misc + dtb_row_ref[...])
    dt_row = _softplus(misc_t[MISC_DT:MISC_DT + SSD_HEADS, :] + dtb_col_ref[...])
    dta_col = dt_col * -jnp.exp(alog_row_ref[...])
    dta_row = dt_row * -jnp.exp(alog_col_ref[...])
    c_off = GROUP_W + SSD_GROUPS * SSD_STATE

    for sub in range(CHUNK // n):
        rs = slice(sub * n, (sub + 1) * n)
        acum_col = jnp.dot(lower_f, dta_col[rs], precision=HIGHEST, preferred_element_type=F32)
        acum_row = jnp.dot(dta_row[:, rs], upper_f, precision=HIGHEST, preferred_element_type=F32)
        for g in range(SSD_GROUPS):
            b_g = xbcs[rs, GROUP_W + g * SSD_STATE:GROUP_W + (g + 1) * SSD_STATE].astype(BF16)
            c_g = xbcs[rs, c_off + g * SSD_STATE:c_off + (g + 1) * SSD_STATE].astype(BF16)
            cb = _dot_nt(c_g, b_g)
            for hh in range(SSD_HPG):
                h = g * SSD_HPG + hh
                cols = slice(h * SSD_HEAD_DIM, (h + 1) * SSD_HEAD_DIM)
                ac_col = acum_col[:, MISC_DT + h:MISC_DT + h + 1]
                ac_row = acum_row[h:h + 1, :]
                seg = jnp.where(lower, ac_col - ac_row, -jnp.inf)
                w = cb * jnp.exp(seg) * dt_row[h:h + 1, rs]
                x_h = xa[rs, cols]
                hs = h_scr[h]
                y_h = _dot(w.astype(BF16), x_h.astype(BF16)) + _dot_nt(c_g, hs.astype(BF16)) * jnp.exp(ac_col)
                y_scr[rs, cols] = y_h
                ac_last = ac_col[n - 1:n, :]
                wend = jnp.exp(ac_last - ac_col) * dt_col[rs, MISC_DT + h:MISC_DT + h + 1]
                xw = (x_h * wend).astype(BF16)
                h_scr[h] = jnp.exp(ac_last) * hs + _dot_tn(xw, b_g)

    y = y_scr[...] + dskip_ref[...] * xa
    y = y * _silu(za_ref[0])
    gw = GROUP_W // SSD_GROUPS
    ng = ng_ref[...]
    for g in range(SSD_GROUPS):
        ya_ref[0, :, g * gw:(g + 1) * gw] = _rms(y[:, g * gw:(g + 1) * gw], ng[:, g * gw:(g + 1) * gw])


def _lane_pad(v, offset):
    return jnp.pad(v, ((0, 0), (offset, LANES - offset - v.shape[1])))[:, None, :]


def _mlstm_chunk(xm_ref, o_ref, zb_ref, misc_ref, cw_ref, cb_ref, wqt_ref, wk_ref, wvt_ref, gb_row_ref,
                 gb_col_ref, ng_ref, yb_ref, c_scr, n_scr, m_scr, ext_scr):
    u = xm_ref[0]
    ext_scr[SUBLANES:SUBLANES + CHUNK, :] = u
    cw = cw_ref[...]
    conv = cb_ref[...] + cw[3:4] * u
    for j in range(CONV_W - 1):
        back = CONV_W - 1 - j
        conv = conv + cw[j:j + 1] * ext_scr[SUBLANES - back:SUBLANES - back + CHUNK, :]
    ext_scr[0:SUBLANES, :] = u[CHUNK - SUBLANES:CHUNK, :]
    xmc = _silu(conv)

    misc = misc_ref[0]
    misc_t = misc.T
    lower, upper = _tri_masks(CHUNK)
    gates_col = misc + gb_row_ref[...]
    gates_row = misc_t[MISC_I:MISC_I + 2 * ML_HEADS, :] + gb_col_ref[...]
    lf_col = -_softplus(-gates_col)
    lf_row = -_softplus(-gates_row)
    bcum_col = jnp.dot(lower.astype(F32), lf_col, precision=HIGHEST, preferred_element_type=F32)
    bcum_row = jnp.dot(lf_row, upper.astype(F32), precision=HIGHEST, preferred_element_type=F32)
    ng = ng_ref[...]
    o_all = o_ref[0]
    zb = zb_ref[0]
    m_all = m_scr[...]
    n_all = n_scr[...]
    m_rows, n_rows = [], []
    for h in range(ML_HEADS):
        sl = slice(h * ML_HEAD_DIM, (h + 1) * ML_HEAD_DIM)
        li_r = gates_row[h:h + 1, :]
        bc_r = bcum_row[ML_HEADS + h:ML_HEADS + h + 1, :]
        col = gates_col[:, MISC_I + h:MISC_I + h + 1] - bcum_col[:, MISC_F + h:MISC_F + h + 1]
        m_prev = m_all[h:h + 1, 0:1]
        lw = jnp.where(upper, bc_r + col, -jnp.inf)
        l_prev = bc_r + m_prev
        m_t = jnp.maximum(l_prev, jnp.max(lw, axis=0, keepdims=True))
        xmc_h = xmc[:, sl].astype(BF16)
        qtb = _dot_nt(wqt_ref[h], xmc_h).astype(BF16)
        k = _dot(xmc_h, wk_ref[h]) * (ML_HEAD_DIM ** -0.5)
        kb = k.astype(BF16)
        vt = _dot_nt(wvt_ref[h], u[:, sl].astype(BF16))
        sc = _dot(kb, qtb) * jnp.exp(lw - m_t)
        g_prev = jnp.exp(l_prev - m_t)
        cs = c_scr[h]
        ns = n_all[h:h + 1, :]
        ns8 = jnp.broadcast_to(ns, (SUBLANES, ML_HEAD_DIM)).astype(BF16)
        num = _dot(vt.astype(BF16), sc.astype(BF16)) + g_prev * _dot(cs.astype(BF16), qtb)
        den = jnp.sum(sc, axis=0, keepdims=True) + g_prev * _dot(ns8, qtb)[0:1]
        hout_t = num / jnp.maximum(jnp.abs(den), jnp.exp(-m_t))
        m_new = m_t[:, CHUNK - 1:CHUNK]
        bc_last = bc_r[:, CHUNK - 1:CHUNK]
        wend = jnp.exp(bc_last - bc_r + li_r - m_new)
        decay = jnp.exp(bc_last + m_prev - m_new)
        c_scr[h] = decay * cs + _dot((vt * wend).astype(BF16), kb)
        wend8 = jnp.broadcast_to(wend, (SUBLANES, CHUNK)).astype(BF16)
        n_rows.append(decay * ns + _dot(wend8, kb)[0:1])
        m_rows.append(jnp.broadcast_to(m_new, (1, LANES)))
        hb = jax.nn.sigmoid(o_all[:, sl]) * hout_t.T
        yb_ref[0, :, sl] = _rms(hb, ng[:, sl]) * _silu(zb[:, sl])
    pad = [jnp.zeros((SUBLANES - ML_HEADS, LANES), F32)]
    n_scr[...] = jnp.concatenate(n_rows + pad, axis=0)
    m_scr[...] = jnp.concatenate(m_rows + pad, axis=0)


SSD_PARAMS = ("conv_a_w", "conv_a_b", "dtb_row", "alog_row", "dtb_col", "alog_col", "dskip_row", "ssd_norm_g")
MLSTM_PARAMS = ("conv_b_w", "conv_b_b", "w_mq_t", "w_mk", "w_mv_t", "gb_row", "gb_col", "ml_norm_g")


def _scan_kernel(xbc_ref, misc_ref, za_ref, xm_ref, o_ref, zb_ref, *rest):
    ns, nm = len(SSD_PARAMS), len(MLSTM_PARAMS)
    ssd_params, ml_params = rest[:ns], rest[ns:ns + nm]
    ya_ref, hfin_ref, sconv_ref, yb_ref, cfin_ref, nfin_ref, mfin_ref, mconv_ref = rest[ns + nm:ns + nm + 8]
    h_scr, sext_scr, y_scr, c_scr, n_scr, m_scr, mext_scr = rest[ns + nm + 8:]
    c = pl.program_id(1)
    nc = pl.num_programs(1)

    @pl.when(c == 0)
    def _():
        h_scr[...] = jnp.zeros_like(h_scr)
        sext_scr[0:SUBLANES, :] = jnp.zeros((SUBLANES, SSD_CONV_CH), F32)
        c_scr[...] = jnp.zeros_like(c_scr)
        n_scr[...] = jnp.zeros_like(n_scr)
        m_scr[...] = jnp.zeros_like(m_scr)
        mext_scr[0:SUBLANES, :] = jnp.zeros((SUBLANES, GROUP_W), F32)

    _ssd_chunk(xbc_ref, misc_ref, za_ref, *ssd_params, ya_ref, h_scr, sext_scr, y_scr)
    _mlstm_chunk(xm_ref, o_ref, zb_ref, misc_ref, *ml_params, yb_ref, c_scr, n_scr, m_scr, mext_scr)

    @pl.when(c == nc - 1)
    def _():
        hfin_ref[0] = h_scr[...]
        sconv_ref[0] = xbc_ref[0, CHUNK - (CONV_W - 1):CHUNK, :]
        cfin_ref[0] = c_scr[...]
        nfin_ref[0] = n_scr[...]
        mfin_ref[0] = m_scr[...]
        mconv_ref[0] = xm_ref[0, CHUNK - (CONV_W - 1):CHUNK, :]


def _scans_prompt(xbc, xm, o, z, misc, lp, layer):
    b, l, _ = xbc.shape
    nc = l // CHUNK
    row = lambda bi, ci: (bi, ci, 0)
    first = lambda nd: (lambda bi, ci: (bi,) + (0,) * (nd - 1))
    tokens = lambda w: pl.BlockSpec((1, CHUNK, w), row)
    pa, pa_specs = _layer_args(lp, layer, SSD_PARAMS)
    pb, pb_specs = _layer_args(lp, layer, MLSTM_PARAMS)
    outs = pl.pallas_call(
        _scan_kernel,
        grid=(b, nc),
        in_specs=[tokens(SSD_CONV_CH), tokens(MISC_W),
                  pl.BlockSpec((1, CHUNK, GROUP_W), lambda bi, ci: (bi, ci, 0)),
                  tokens(GROUP_W), tokens(GROUP_W),
                  pl.BlockSpec((1, CHUNK, GROUP_W), lambda bi, ci: (bi, ci, 1)),
                  ] + pa_specs + pb_specs,
        out_specs=[tokens(GROUP_W),
                   pl.BlockSpec((1, SSD_HEADS, SSD_HEAD_DIM, SSD_STATE), first(4)),
                   pl.BlockSpec((1, CONV_W - 1, SSD_CONV_CH), first(3)),
                   tokens(GROUP_W),
                   pl.BlockSpec((1, ML_HEADS, ML_HEAD_DIM, ML_HEAD_DIM), first(4)),
                   pl.BlockSpec((1, SUBLANES, LANES), first(3)),
                   pl.BlockSpec((1, SUBLANES, LANES), first(3)),
                   pl.BlockSpec((1, CONV_W - 1, GROUP_W), first(3))],
        out_shape=[jax.ShapeDtypeStruct((b, l, GROUP_W), F32),
                   jax.ShapeDtypeStruct((b, SSD_HEADS, SSD_HEAD_DIM, SSD_STATE), F32),
                   jax.ShapeDtypeStruct((b, CONV_W - 1, SSD_CONV_CH), F32),
                   jax.ShapeDtypeStruct((b, l, GROUP_W), F32),
                   jax.ShapeDtypeStruct((b, ML_HEADS, ML_HEAD_DIM, ML_HEAD_DIM), F32),
                   jax.ShapeDtypeStruct((b, SUBLANES, LANES), F32),
                   jax.ShapeDtypeStruct((b, SUBLANES, LANES), F32),
                   jax.ShapeDtypeStruct((b, CONV_W - 1, GROUP_W), F32)],
        scratch_shapes=[pltpu.VMEM((SSD_HEADS, SSD_HEAD_DIM, SSD_STATE), F32),
                        pltpu.VMEM((SUBLANES + CHUNK, SSD_CONV_CH), F32),
                        pltpu.VMEM((CHUNK, GROUP_W), F32),
                        pltpu.VMEM((ML_HEADS, ML_HEAD_DIM, ML_HEAD_DIM), F32),
                        pltpu.VMEM((SUBLANES, LANES), F32),
                        pltpu.VMEM((SUBLANES, LANES), F32),
                        pltpu.VMEM((SUBLANES + CHUNK, GROUP_W), F32)],
        compiler_params=_cparams(("parallel", "arbitrary")),
        name="scans_prompt",
    )(xbc, misc, z, xm, o, z, *pa, *pb)
    ya, ssm_h, ssm_buf, yb, cfin, nfin, mfin, mconv = outs
    return ya, ssm_h, ssm_buf, yb, cfin, nfin[:, :ML_HEADS, :], mfin[:, :ML_HEADS, 0], mconv


def _rope_rows(r, g, cos2, sinsgn):
    half = MLA_ROPE // 2
    rn = r * lax.rsqrt(jnp.sum(r * r, axis=-1, keepdims=True) * (1.0 / MLA_ROPE) + EPS) * g
    lane = lax.broadcasted_iota(jnp.int32, rn.shape, 1)
    r_sw = jnp.where(lane < half, pltpu.roll(rn, LANES - half, 1), pltpu.roll(rn, half, 1))
    return rn * cos2 + r_sw * sinsgn


def _mla_prep_kernel(cq_ref, ckv_ref, misc_ref, cos_ref, sin_ref, cos_t_ref, sin_t_ref, qg_ref, kvg_ref, wqn_ref,
                     wqr_ref, wuk_ref, qnn_ref, qnr_ref, knn_ref, knr_ref, q_ref, k_ref, v_ref, lat_ref, kr_ref,
                     *, transposed):
    cos2 = cos_ref[...]
    sinsgn = sin_ref[...]
    c_q = _rms(cq_ref[0], qg_ref[...]).astype(BF16)
    if not transposed:
        qn_all = _dot(c_q, wqn_ref[...])
    c_kv = _rms(ckv_ref[0], kvg_ref[...])
    lat_ref[0] = c_kv
    c_kv_b = c_kv.astype(BF16)
    if transposed:
        v_ref[0] = c_kv.T.astype(BF16)
    else:
        v_ref[0] = c_kv_b
    misc = misc_ref[0]
    lane = lax.broadcasted_iota(jnp.int32, misc.shape, 1)
    kr_raw = jnp.where(lane < MLA_ROPE, misc, 0.0)
    kr = _rope_rows(kr_raw, knr_ref[...], cos2, sinsgn)
    kr_ref[0] = kr[:, 0:MLA_ROPE]
    kr_b = kr[:, 0:MLA_ROPE].astype(BF16)
    kn_all = _dot(c_kv_b, wuk_ref[...])
    half = MLA_ROPE // 2
    col_rms = lambda x, g: x * lax.rsqrt(jnp.mean(x * x, axis=0, keepdims=True) + EPS) * g
    for h in range(MLA_HEADS):
        sl = slice(h * MLA_NOPE, (h + 1) * MLA_NOPE)
        if transposed:
            qn_t = col_rms(_dot_nt(wqn_ref[sl, :], c_q), qnn_ref[...])
            qr_t = col_rms(_dot_nt(wqr_ref[h], c_q), qnr_ref[...])
            qr_sw = jnp.concatenate([qr_t[half:], qr_t[:half]], axis=0)
            qr_t = qr_t * cos_t_ref[...] + qr_sw * sin_t_ref[...]
            q_ref[0, h, 0:MLA_NOPE, :] = qn_t.astype(BF16)
            q_ref[0, h, MLA_NOPE:MLA_QK, :] = qr_t.astype(BF16)
        else:
            qn = _rms(qn_all[:, sl], qnn_ref[...])
            qr = _rope_rows(_dot(c_q, wqr_ref[h]), qnr_ref[...], cos2, sinsgn)
            q_ref[0, h, :, 0:MLA_NOPE] = qn.astype(BF16)
            q_ref[0, h, :, MLA_NOPE:MLA_QK] = qr[:, 0:MLA_ROPE].astype(BF16)
        k_ref[0, h, :, 0:MLA_NOPE] = _rms(kn_all[:, sl], knn_ref[...]).astype(BF16)
        k_ref[0, h, :, MLA_NOPE:MLA_QK] = kr_b


def _mla_prep(cq, ckv, misc, rope_tabs, lp, layer, tm, transposed):
    b, l, _ = cq.shape
    cos2, sinsgn, cos_t, sin_t = rope_tabs
    if transposed:
        names = ("q_norm_g", "kv_norm_g", "w_uq_nope_t", "w_uq_rope_t", "w_uk", "qn_nope_g_col", "qn_rope_g_col",
                 "kn_nope_g", "kn_rope_g_pad")
    else:
        names = ("q_norm_g", "kv_norm_g", "w_uq_nope", "w_uq_rope", "w_uk", "qn_nope_g", "qn_rope_g_pad",
                 "kn_nope_g", "kn_rope_g_pad")
    params, param_specs = _layer_args(lp, layer, names)
    row = lambda bi, ti: (bi, ti, 0)
    hrow = lambda bi, ti: (bi, 0, ti, 0)
    if transposed:
        q_spec = pl.BlockSpec((1, MLA_HEADS, MLA_QK, tm), lambda bi, ti: (bi, 0, 0, ti))
        q_shape = jax.ShapeDtypeStruct((b, MLA_HEADS, MLA_QK, l), BF16)
        v_spec = pl.BlockSpec((1, KV_RANK, tm), lambda bi, ti: (bi, 0, ti))
        v_shape = jax.ShapeDtypeStruct((b, KV_RANK, l), BF16)
    else:
        q_spec = pl.BlockSpec((1, MLA_HEADS, tm, MLA_QK), hrow)
        q_shape = jax.ShapeDtypeStruct((b, MLA_HEADS, l, MLA_QK), BF16)
        v_spec = pl.BlockSpec((1, tm, KV_RANK), row)
        v_shape = jax.ShapeDtypeStruct((b, l, KV_RANK), BF16)
    return pl.pallas_call(
        functools.partial(_mla_prep_kernel, transposed=transposed),
        grid=(b, l // tm),
        in_specs=[pl.BlockSpec((1, tm, Q_RANK), row), pl.BlockSpec((1, tm, KV_RANK), row),
                  pl.BlockSpec((1, tm, MISC_W), row),
                  pl.BlockSpec((tm, LANES), lambda bi, ti: (ti, 0)),
                  pl.BlockSpec((tm, LANES), lambda bi, ti: (ti, 0)),
                  pl.BlockSpec((MLA_ROPE, tm), lambda bi, ti: (0, ti)),
                  pl.BlockSpec((MLA_ROPE, tm), lambda bi, ti: (0, ti))] + param_specs,
        out_specs=[q_spec,
                   pl.BlockSpec((1, MLA_HEADS, tm, MLA_QK), hrow),
                   v_spec,
                   pl.BlockSpec((1, tm, KV_RANK), row),
                   pl.BlockSpec((1, tm, MLA_ROPE), row)],
        out_shape=[q_shape,
                   jax.ShapeDtypeStruct((b, MLA_HEADS, l, MLA_QK), BF16),
                   v_shape,
                   jax.ShapeDtypeStruct((b, l, KV_RANK), F32),
                   jax.ShapeDtypeStruct((b, l, MLA_ROPE), F32)],
        compiler_params=_cparams(("parallel", "parallel")),
        name="mla_prep",
    )(cq, ckv, misc, cos2, sinsgn, cos_t, sin_t, *params)


def _mla_attn_kernel(qt_ref, k_ref, vt_ref, zc_ref, wuv_ref, out_ref, m_scr, l_scr, acc_scr, *, tile):
    qi = pl.program_id(1)
    scale = MLA_QK ** -0.5
    m_scr[...] = jnp.full_like(m_scr, -jnp.inf)
    l_scr[...] = jnp.zeros_like(l_scr)
    acc_scr[...] = jnp.zeros_like(acc_scr)

    def kv_block(start, width, diagonal):
        vt = vt_ref[0, :, pl.ds(start, width)]
        scores = [_dot(k_ref[0, h, pl.ds(start, width), :], qt_ref[0, h]) for h in range(MLA_HEADS)]
        for h in range(MLA_HEADS):
            s = scores[h] * scale
            if diagonal:
                kid = lax.broadcasted_iota(jnp.int32, (width, tile), 0)
                qid = lax.broadcasted_iota(jnp.int32, (width, tile), 1)
                s = jnp.where(kid <= qid, s, -jnp.inf)
            m_prev = m_scr[h:h + 1, :]
            m_new = jnp.maximum(m_prev, jnp.max(s, axis=0, keepdims=True))
            alpha = jnp.exp(m_prev - m_new)
            p = jnp.exp(s - m_new)
            l_scr[h:h + 1, :] = alpha * l_scr[h:h + 1, :] + jnp.sum(p, axis=0, keepdims=True)
            acc_scr[h] = alpha * acc_scr[h] + _dot(vt, p.astype(BF16))
            m_scr[h:h + 1, :] = m_new

    def pair_body(p, carry):
        kv_block(pl.multiple_of(p * 2 * tile, 2 * tile), 2 * tile, False)
        return carry

    def single_body(_, carry):
        kv_block(pl.multiple_of((qi - 1) * tile, tile), tile, False)
        return carry

    lax.fori_loop(0, qi // 2, pair_body, 0)
    lax.fori_loop(0, qi % 2, single_body, 0)
    kv_block(pl.multiple_of(qi * tile, tile), tile, True)
    for h in range(MLA_HEADS):
        sl = slice(h * MLA_V, (h + 1) * MLA_V)
        ctx_t = (acc_scr[h] / l_scr[h:h + 1, :]).astype(BF16)
        out_ref[0, :, sl] = _dot_tn(ctx_t, wuv_ref[:, sl]) * _silu(zc_ref[0, :, sl])


def _mla_attn_prompt(qt, k, latt, z, w_uv, layer, tile=512):
    b, hds, _, l = qt.shape
    kern = functools.partial(_mla_attn_kernel, tile=tile)
    return pl.pallas_call(
        kern,
        grid=(b, l // tile),
        in_specs=[pl.BlockSpec((1, hds, MLA_QK, tile), lambda bi, qi: (bi, 0, 0, qi)),
                  pl.BlockSpec((1, hds, l, MLA_QK), lambda bi, qi: (bi, 0, 0, 0)),
                  pl.BlockSpec((1, KV_RANK, l), lambda bi, qi: (bi, 0, 0)),
                  pl.BlockSpec((1, tile, GROUP_W), lambda bi, qi: (bi, qi, 2)),
                  _layer_spec(w_uv, layer)],
        out_specs=pl.BlockSpec((1, tile, GROUP_W), lambda bi, qi: (bi, qi, 0)),
        out_shape=jax.ShapeDtypeStruct((b, l, GROUP_W), F32),
        scratch_shapes=[pltpu.VMEM((SUBLANES, tile), F32), pltpu.VMEM((SUBLANES, tile), F32),
                        pltpu.VMEM((hds, KV_RANK, tile), F32)],
        compiler_params=_cparams(("parallel", "arbitrary")),
        name="mla_attn",
    )(qt, k, latt, z, w_uv)


def _memkv_kernel(mem_ref, g_ref, wk_ref, wv_ref, xkg_ref, mk_ref, mv_ref):
    mem_len = mem_ref.shape[1]
    mem_n = _rms(mem_ref[0], g_ref[0]).astype(BF16)
    kraw = _dot(mem_n, wk_ref[0])
    vraw = _dot(mem_n, wv_ref[0])
    for h in range(X_HEADS):
        sl = slice(h * X_HEAD_DIM, (h + 1) * X_HEAD_DIM)
        rows = pl.ds(h, mem_len, stride=X_HEADS)
        mk_ref[0, 0, rows, :] = _rms(kraw[:, sl], xkg_ref[0])
        mv_ref[0, 0, rows, :] = vraw[:, sl]


def _memkv(mem, mem_norm_g, w_mem_k, w_mem_v, xk_norm_g):
    b, m, _ = mem.shape
    depth = w_mem_k.shape[0]
    lsel = lambda li, bi: (li, 0, 0)
    return pl.pallas_call(
        _memkv_kernel,
        grid=(depth, b),
        in_specs=[pl.BlockSpec((1, m, D_MODEL), lambda li, bi: (bi, 0, 0)),
                  pl.BlockSpec((1, 1, D_MODEL), lsel),
                  pl.BlockSpec((1, D_MODEL, GROUP_W), lsel),
                  pl.BlockSpec((1, D_MODEL, GROUP_W), lsel),
                  pl.BlockSpec((1, 1, X_HEAD_DIM), lsel)],
        out_specs=[pl.BlockSpec((1, 1, m * X_HEADS, X_HEAD_DIM), lambda li, bi: (li, bi, 0, 0)),
                   pl.BlockSpec((1, 1, m * X_HEADS, X_HEAD_DIM), lambda li, bi: (li, bi, 0, 0))],
        out_shape=[jax.ShapeDtypeStruct((depth, b, m * X_HEADS, X_HEAD_DIM), F32),
                   jax.ShapeDtypeStruct((depth, b, m * X_HEADS, X_HEAD_DIM), F32)],
        compiler_params=_cparams(("parallel", "parallel")),
        name="memkv",
    )(mem, mem_norm_g, w_mem_k, w_mem_v, xk_norm_g)


def _xattn_kernel(xq_ref, mk_ref, mv_ref, zd_ref, g_ref, out_ref, *, rows, mem_len, nbatch):
    if rows == 1:
        for bb in range(nbatch):
            xq = xq_ref[bb]
            heads = [xq[:, h * X_HEAD_DIM:(h + 1) * X_HEAD_DIM] for h in range(X_HEADS)]
            qn = _rms(jnp.concatenate(heads + heads, axis=0), g_ref[...]).astype(BF16)
            s = _dot_nt(qn, mk_ref[0, bb].astype(BF16)) * (X_HEAD_DIM ** -0.5)
            row_head = lax.broadcasted_iota(jnp.int32, s.shape, 0) & (X_HEADS - 1)
            col_head = lax.broadcasted_iota(jnp.int32, s.shape, 1) & (X_HEADS - 1)
            s = jnp.where(row_head == col_head, s, -jnp.inf)
            e = jnp.exp(s - jnp.max(s, axis=1, keepdims=True))
            p = e / jnp.sum(e, axis=1, keepdims=True)
            y = _dot(p.astype(BF16), mv_ref[0, bb].astype(BF16))
            out_ref[bb] = jnp.concatenate([y[h:h + 1] for h in range(X_HEADS)], axis=1) * _silu(zd_ref[bb])
        return
    xq = xq_ref[0]
    zd = zd_ref[0]
    for h in range(X_HEADS):
        sl = slice(h * X_HEAD_DIM, (h + 1) * X_HEAD_DIM)
        mk_h = mk_ref[0, 0, pl.ds(h, mem_len, stride=X_HEADS), :]
        mv_h = mv_ref[0, 0, pl.ds(h, mem_len, stride=X_HEADS), :]
        qn = _rms(xq[:, sl], g_ref[...]).astype(BF16)
        s = _dot_nt(qn, mk_h.astype(BF16)) * (X_HEAD_DIM ** -0.5)
        s = s - jnp.max(s, axis=1, keepdims=True)
        e = jnp.exp(s)
        p = e / jnp.sum(e, axis=1, keepdims=True)
        y = _dot(p.astype(BF16), mv_h.astype(BF16))
        out_ref[0, :, sl] = y * _silu(zd[:, sl])


XATTN_DECODE_BATCH = 4


def _xattn(xq, mk, mv, layer, z, g, tm):
    b, l, _ = xq.shape
    mem_len = mk.shape[2] // X_HEADS
    nb = XATTN_DECODE_BATCH if tm == 1 else 1
    kern = functools.partial(_xattn_kernel, rows=tm, mem_len=mem_len, nbatch=nb)
    mem_spec = pl.BlockSpec((1, nb) + mk.shape[2:], lambda bi, ti: (layer, bi, 0, 0))
    return pl.pallas_call(
        kern,
        grid=(b // nb, l // tm),
        in_specs=[pl.BlockSpec((nb, tm, GROUP_W), lambda bi, ti: (bi, ti, 0)),
                  mem_spec, mem_spec,
                  pl.BlockSpec((nb, tm, GROUP_W), lambda bi, ti: (bi, ti, 3)),
                  _layer_spec(g, layer)],
        out_specs=pl.BlockSpec((nb, tm, GROUP_W), lambda bi, ti: (bi, ti, 0)),
        out_shape=jax.ShapeDtypeStruct((b, l, GROUP_W), F32),
        compiler_params=_cparams(("parallel", "parallel")),
        name="xattn",
    )(xq, mk, mv, z, g)


def _outproj_kernel(ya_ref, yb_ref, yc_ref, yd_ref, x_ref, w_ref, fg_ref, out_ref, *, final):
    acc = x_ref[...]
    for i, ref in enumerate((ya_ref, yb_ref, yc_ref, yd_ref)):
        acc = acc + _dot(ref[...].astype(BF16), w_ref[i * GROUP_W:(i + 1) * GROUP_W, :])
    if final:
        acc = _rms(acc, fg_ref[...])
    out_ref[...] = acc


def _outproj(ys, x2d, w_out, layer, final_g, tm, final):
    n = x2d.shape[0]
    kern = functools.partial(_outproj_kernel, final=final)
    yspec = pl.BlockSpec((tm, GROUP_W), lambda i: (i, 0))
    return pl.pallas_call(
        kern,
        grid=(n // tm,),
        in_specs=[yspec, yspec, yspec, yspec,
                  pl.BlockSpec((tm, D_MODEL), lambda i: (i, 0)),
                  _layer_spec(w_out, layer), _const_spec((1, D_MODEL))],
        out_specs=pl.BlockSpec((tm, D_MODEL), lambda i: (i, 0)),
        out_shape=jax.ShapeDtypeStruct((n, D_MODEL), F32),
        compiler_params=_cparams(("parallel",)),
        name="outproj",
    )(*ys, x2d, w_out, final_g)


def _col_form(row, n):
    return jnp.broadcast_to(row, (LANES, n)).T


N_DEC_STATES = 6


DEC_STATE_BATCH = 4


def _dec_state_kernel(*refs):
    params = refs[11:25]
    for bb in range(DEC_STATE_BATCH):
        one = lambda r: r.at[pl.ds(bb, 1)]
        _dec_state_one([one(r) for r in refs[0:5]], [one(r.at[0]) for r in refs[5:5 + N_DEC_STATES]], params,
                       [one(r) for r in refs[-8:-6]], [one(r.at[0]) for r in refs[-N_DEC_STATES:]])


def _dec_state_one(tokens, states, params, ys, new_states):
    xbc_ref, xm_ref, o_ref, z_ref, misc_ref = tokens
    hs_ref, sbuf_ref, cm_ref, nm_ref, mm_ref, mbuf_ref = states
    (cwa_ref, cba_ref, dtb_ref, alog_ref, hexp_ref, dskip_ref, nga_ref,
     cwb_ref, cbb_ref, wq_ref, wk_ref, wv_ref, gb_ref, ngb_ref) = params
    ya_ref, yb_ref = ys
    hs_out, sbuf_out, cm_out, nm_out, mm_out, mbuf_out = new_states
    misc = misc_ref[0]
    z = z_ref[0]

    u = xbc_ref[0]
    buf = sbuf_ref[0]
    cw = cwa_ref[...]
    conv = cba_ref[...] + cw[3:4] * u
    for j in range(CONV_W - 1):
        conv = conv + cw[j:j + 1] * buf[j:j + 1]
    sbuf_out[0, 0:CONV_W - 2, :] = buf[1:CONV_W - 1]
    sbuf_out[0, CONV_W - 2:CONV_W - 1, :] = u
    xbcs = _silu(conv)
    xa = xbcs[:, 0:GROUP_W]
    dt = _softplus(misc + dtb_ref[...])
    a = -jnp.exp(alog_ref[...])
    hexp = hexp_ref[...]
    dt8 = jnp.broadcast_to(dt, (SUBLANES, LANES))
    da8 = jnp.broadcast_to(dt * a, (SUBLANES, LANES))
    dt_x = jnp.dot(dt8, hexp, precision=HIGHEST, preferred_element_type=F32)[0:1]
    decay_x = jnp.exp(jnp.dot(da8, hexp, precision=HIGHEST, preferred_element_type=F32)[0:1])
    hstate = hs_ref[0].reshape(SSD_HEADS * SSD_HEAD_DIM, SSD_STATE)
    gw = GROUP_W // SSD_GROUPS
    c_off = GROUP_W + SSD_GROUPS * SSD_STATE
    dtx_col = _col_form(dt_x * xa, GROUP_W)
    decay_col = _col_form(decay_x, GROUP_W)
    y_parts = []
    for g in range(SSD_GROUPS):
        b_g = xbcs[:, GROUP_W + g * SSD_STATE:GROUP_W + (g + 1) * SSD_STATE]
        c_g = xbcs[:, c_off + g * SSD_STATE:c_off + (g + 1) * SSD_STATE]
        cb = jnp.sum(c_g * b_g, axis=1, keepdims=True)
        rows = slice(g * gw, (g + 1) * gw)
        h_g = hstate[rows]
        c8 = jnp.broadcast_to(c_g, (SUBLANES, SSD_STATE)).astype(BF16)
        ch = _dot_nt(c8, h_g.astype(BF16))[0:1]
        y_parts.append(cb * dt_x[:, rows] * xa[:, rows] + ch * decay_x[:, rows])
        h_new = decay_col[rows] * h_g + dtx_col[rows] * b_g
        hs_out[0, g * SSD_HPG:(g + 1) * SSD_HPG] = h_new.reshape(SSD_HPG, SSD_HEAD_DIM, SSD_STATE)
    y = jnp.concatenate(y_parts, axis=1) + dskip_ref[...] * xa
    y = y * _silu(z[:, 0:GROUP_W])
    nga = nga_ref[...]
    for g in range(SSD_GROUPS):
        rows = slice(g * gw, (g + 1) * gw)
        ya_ref[0, :, rows] = _rms(y[:, rows], nga[:, rows])

    um = xm_ref[0]
    mbuf = mbuf_ref[0]
    cwb = cwb_ref[...]
    convb = cbb_ref[...] + cwb[3:4] * um
    for j in range(CONV_W - 1):
        convb = convb + cwb[j:j + 1] * mbuf[j:j + 1]
    mbuf_out[0, 0:CONV_W - 2, :] = mbuf[1:CONV_W - 1]
    mbuf_out[0, CONV_W - 2:CONV_W - 1, :] = um
    xmc = _silu(convb)
    gates = misc + gb_ref[...]
    lf_all = -_softplus(-gates)
    mm = mm_ref[0]
    ngb = ngb_ref[...]
    o_all = o_ref[0]
    for h in range(ML_HEADS):
        sl = slice(h * ML_HEAD_DIM, (h + 1) * ML_HEAD_DIM)
        li = gates[:, MISC_I + h:MISC_I + h + 1]
        lf = lf_all[:, MISC_F + h:MISC_F + h + 1]
        m_prev = mm[:, h:h + 1]
        l_prev = lf + m_prev
        m_t = jnp.maximum(l_prev, li)
        xmc8 = jnp.broadcast_to(xmc[:, sl], (SUBLANES, ML_HEAD_DIM)).astype(BF16)
        um8 = jnp.broadcast_to(um[:, sl], (SUBLANES, ML_HEAD_DIM)).astype(BF16)
        q8 = _dot(xmc8, wq_ref[h])
        q = q8[0:1]
        k = _dot(xmc8, wk_ref[h])[0:1] * (ML_HEAD_DIM ** -0.5)
        v = _dot(um8, wv_ref[h])[0:1]
        qk = jnp.sum(q * k, axis=1, keepdims=True)
        sc = qk * jnp.exp(li - m_t)
        g_prev = jnp.exp(l_prev - m_t)
        cs = cm_ref[0, h]
        ns = nm_ref[0, h:h + 1, :]
        cq = _dot_nt(q8.astype(BF16), cs.astype(BF16))[0:1]
        num = sc * v + g_prev * cq
        den = sc + g_prev * jnp.sum(ns * q, axis=1, keepdims=True)
        hout = num / jnp.maximum(jnp.abs(den), jnp.exp(-m_t))
        wend = jnp.exp(li - m_t)
        decay = jnp.exp(l_prev - m_t)
        cm_out[0, h] = decay * cs + _col_form(wend * v, ML_HEAD_DIM) * k
        nm_out[0, h:h + 1, :] = decay * ns + wend * k
        mm_out[0, :, h:h + 1] = m_t
        hb = jax.nn.sigmoid(o_all[:, sl]) * hout
        yb_ref[0, :, sl] = _rms(hb, ngb[:, sl]) * _silu(z[:, GROUP_W + h * ML_HEAD_DIM:GROUP_W + (h + 1) * ML_HEAD_DIM])


def _dec_state(xbc, xm, o, z, misc, states, layer, lp):
    b = xbc.shape[0]
    nb = DEC_STATE_BATCH
    tok = lambda w: pl.BlockSpec((nb, 1, w), lambda bi: (bi, 0, 0))
    at_layer = lambda t: pl.BlockSpec((1, nb) + t.shape[2:], lambda bi: (layer, bi) + (0,) * (t.ndim - 2))
    state_specs = [at_layer(t) for t in states]
    pa, pa_specs = _layer_args(lp, layer, ("conv_a_w", "conv_a_b", "dtb_row", "alog_row"))
    pb, pb_specs = _layer_args(lp, layer, ("dskip_row", "ssd_norm_g", "conv_b_w", "conv_b_b", "w_mq", "w_mk", "w_mv",
                                           "gb_row", "ml_norm_g"))
    params = pa + [lp["head_expand"]] + pb
    param_specs = pa_specs + [_const_spec((LANES, GROUP_W))] + pb_specs
    inputs = [xbc, xm, o, z, misc] + list(states) + params
    in_specs = [tok(SSD_CONV_CH), tok(GROUP_W), tok(GROUP_W), tok(D_MIX), tok(MISC_W)] + state_specs + param_specs
    aliases = {5 + i: 2 + i for i in range(N_DEC_STATES)}
    outs = pl.pallas_call(
        _dec_state_kernel,
        grid=(b // nb,),
        in_specs=in_specs,
        out_specs=[tok(GROUP_W), tok(GROUP_W)] + state_specs,
        out_shape=[jax.ShapeDtypeStruct((b, 1, GROUP_W), F32)] * 2
        + [jax.ShapeDtypeStruct(t.shape, F32) for t in states],
        input_output_aliases=aliases,
        compiler_params=_cparams(("parallel",)),
        name="dec_state",
    )(*inputs)
    return outs[0], outs[1], tuple(outs[2:])


def _mla_decode_kernel(pt_ref, lat_hbm, kr_hbm, qn_ref, qr_ref, qfull_ref, kfull_ref, vnew_ref, zc_ref,
                       wukt_ref, wuv_ref, kng_ref, out_ref,
                       lat_buf, kr_buf, sem, qa_scr, m_scr, l_scr, acc_scr, *, layer, pps, page, nsub):
    b = pl.program_id(0)
    s = pl.program_id(1)
    ns = pl.num_programs(1)
    nb = pl.num_programs(0)
    step = b * ns + s
    last = nb * ns - 1
    slot = step % DECODE_SLOTS
    sub = pps * page // nsub

    def copies(bb, ss, sl):
        out = []
        for p in range(pps):
            pg = pt_ref[bb, ss * pps + p]
            out.append(pltpu.make_async_copy(lat_hbm.at[layer, pg], lat_buf.at[sl, pl.ds(p * page, page), :],
                                             sem.at[0, sl]))
            out.append(pltpu.make_async_copy(kr_hbm.at[layer, pg], kr_buf.at[sl, :, pl.ds(p * page, page)],
                                             sem.at[1, sl]))
        return out

    def start_step(target, sl):
        tgt = jnp.minimum(target, last)
        for cp in copies(tgt // ns, tgt % ns, sl):
            cp.start()

    @pl.when(step == 0)
    def _():
        start_step(step, 0)
        start_step(step + 1, 1)

    scale = MLA_QK ** -0.5

    @pl.when(s == 0)
    def _():
        qng = (qn_ref[0].astype(F32) * kng_ref[...]).astype(BF16)
        for h in range(MLA_HEADS):
            full = _dot(qng, wukt_ref[h * MLA_NOPE:(h + 1) * MLA_NOPE, :])
            qa_scr[h:h + 1, :] = full[h:h + 1]
            qa_scr[MLA_HEADS + h:MLA_HEADS + h + 1, :] = full[h:h + 1]
        m_scr[...] = jnp.full_like(m_scr, -jnp.inf)
        l_scr[...] = jnp.zeros_like(l_scr)
        acc_scr[...] = jnp.zeros_like(acc_scr)

    for cp in copies(b, s, slot):
        cp.wait()

    qa = qa_scr[...].astype(BF16)
    qr = qr_ref[0]
    wukt = wukt_ref[...]

    def scores(i):
        latb = lat_buf[slot, i * sub:(i + 1) * sub, :].astype(BF16)
        krb = kr_buf[slot, :, i * sub:(i + 1) * sub].astype(BF16)
        kraw_t = _dot_nt(wukt, latb)
        sa = _dot_nt(qa, latb)
        sr = _dot(qr, krb)
        rinv_rows = []
        for h in range(MLA_HEADS):
            kh = kraw_t[h * MLA_NOPE:(h + 1) * MLA_NOPE, :]
            ssq = jnp.sum(kh * kh, axis=0, keepdims=True)
            rinv_rows.append(lax.rsqrt(ssq * (1.0 / MLA_NOPE) + EPS))
        rinv = jnp.concatenate(rinv_rows + rinv_rows, axis=0)
        return latb, (sa * rinv + sr) * scale

    m = m_scr[...]
    l = l_scr[...]
    acc = acc_scr[...]
    pending = scores(0)
    for i in range(nsub):
        latb, sc = pending
        if i + 1 < nsub:
            pending = scores(i + 1)
        m_new = jnp.maximum(m, jnp.max(sc, axis=1, keepdims=True))
        alpha = jnp.exp(m - m_new)
        p = jnp.exp(sc - m_new)
        l = alpha * l + jnp.sum(p, axis=1, keepdims=True)
        acc = alpha * acc + _dot(p.astype(BF16), latb)
        m = m_new
    m_scr[...] = m
    l_scr[...] = l
    acc_scr[...] = acc

    start_step(step + 2, (step + 2) % DECODE_SLOTS)

    @pl.when(s == ns - 1)
    def _():
        s_new = jnp.sum(qfull_ref[0].astype(F32) * kfull_ref[0].astype(F32), axis=1, keepdims=True) * scale
        m_fin = jnp.maximum(m, s_new)
        alpha2 = jnp.exp(m - m_fin)
        p_new = jnp.exp(s_new - m_fin)
        l_fin = alpha2 * l + p_new
        acc_fin = alpha2 * acc + p_new * vnew_ref[0].astype(F32)
        ctx = (acc_fin / l_fin).astype(BF16)
        yfull = _dot(ctx, wuv_ref[...])
        parts = [yfull[h:h + 1, h * MLA_V:(h + 1) * MLA_V] for h in range(MLA_HEADS)]
        out_ref[0] = jnp.concatenate(parts, axis=1) * _silu(zc_ref[0])

    @pl.when(step == last)
    def _():
        for ahead in (1, 2):
            for cp in copies(b, s, (step + ahead) % DECODE_SLOTS):
                cp.wait()


DECODE_PAGES_PER_STEP = 64
DECODE_SUB_BLOCKS = 2
DECODE_SLOTS = 3


def _mla_decode(page_table, cache_lat, cache_kr, layer, qn8, qr8, qfull8, kfull8, vnew, z, lp,
                pps=DECODE_PAGES_PER_STEP, nsub=DECODE_SUB_BLOCKS):
    b, n_pages = page_table.shape
    page = cache_lat.shape[2]
    ns = n_pages // pps
    t = pps * page
    kern = functools.partial(_mla_decode_kernel, layer=layer, pps=pps, page=page, nsub=nsub)
    per_b = lambda shape: pl.BlockSpec((1,) + shape, lambda bi, si, pt: (bi, 0, 0))
    params, param_specs = _layer_args(lp, layer, ("w_uk_t", "w_uv", "kn_nope_g"))
    grid_spec = pltpu.PrefetchScalarGridSpec(
        num_scalar_prefetch=1,
        grid=(b, ns),
        in_specs=[pl.BlockSpec(memory_space=pl.ANY), pl.BlockSpec(memory_space=pl.ANY),
                  per_b((SUBLANES, MLA_NOPE)), per_b((SUBLANES, MLA_ROPE)),
                  per_b((SUBLANES, MLA_QK)), per_b((SUBLANES, MLA_QK)), per_b((1, KV_RANK)),
                  pl.BlockSpec((1, 1, GROUP_W), lambda bi, si, pt: (bi, 0, 2)),
                  ] + param_specs,
        out_specs=pl.BlockSpec((1, 1, GROUP_W), lambda bi, si, pt: (bi, 0, 0)),
        scratch_shapes=[pltpu.VMEM((DECODE_SLOTS, t, KV_RANK), F32), pltpu.VMEM((DECODE_SLOTS, MLA_ROPE, t), F32),
                        pltpu.SemaphoreType.DMA((2, DECODE_SLOTS)),
                        pltpu.VMEM((SUBLANES, KV_RANK), F32),
                        pltpu.VMEM((SUBLANES, 1), F32), pltpu.VMEM((SUBLANES, 1), F32),
                        pltpu.VMEM((SUBLANES, KV_RANK), F32)],
    )
    return pl.pallas_call(
        kern,
        grid_spec=grid_spec,
        out_shape=jax.ShapeDtypeStruct((b, 1, GROUP_W), F32),
        compiler_params=_cparams(("arbitrary", "arbitrary")),
        name="mla_decode",
    )(page_table, cache_lat, cache_kr, qn8, qr8, qfull8, kfull8, vnew, z, *params)


def _pack_w_in(w_in):
    names = ("z", "xbc", "dt", "xm", "i", "f", "o", "cq", "ckv", "kr", "xq")
    offs = {}
    o = 0
    for nme, w in zip(names, IN_SIZES):
        offs[nme] = (o, o + w)
        o += w
    col = lambda nme: w_in[:, :, offs[nme][0]:offs[nme][1]].astype(BF16)
    pad = jnp.zeros(w_in.shape[:2] + (MISC_W - (MISC_F + ML_HEADS),), BF16)
    misc = jnp.concatenate([col("kr"), col("dt"), col("i"), col("f"), pad], axis=-1)
    return [misc if nme == "misc" else col(nme) for nme, _ in PACK_SEGS]


def _stack_params(p):
    depth = p["norm_g"].shape[0]
    row = lambda a: a[:, None, :]
    lp = {}
    for nme in ("norm_g", "conv_a_b", "ssd_norm_g", "conv_b_b", "ml_norm_g", "q_norm_g", "kv_norm_g", "qn_nope_g",
                "kn_nope_g", "xq_norm_g"):
        lp[nme] = row(p[nme])
    lp["conv_a_w"] = p["conv_a_w"]
    lp["conv_b_w"] = p["conv_b_w"]
    lp["dtb_row"] = _lane_pad(p["dt_bias"], MISC_DT)
    lp["alog_row"] = _lane_pad(p["a_log"], MISC_DT)
    lp["dtb_col"] = p["dt_bias"][:, :, None]
    lp["alog_col"] = p["a_log"][:, :, None]
    lp["dskip_row"] = row(jnp.repeat(p["d_skip"], SSD_HEAD_DIM, axis=1))
    lp["w_mq"] = p["w_mq"].astype(BF16)
    lp["w_mk"] = p["w_mk"].astype(BF16)
    lp["w_mv"] = p["w_mv"].astype(BF16)
    lp["w_mq_t"] = jnp.swapaxes(lp["w_mq"], 2, 3)
    lp["w_mv_t"] = jnp.swapaxes(lp["w_mv"], 2, 3)
    gate_b = jnp.concatenate([p["b_i"], p["b_f"]], axis=1)
    lp["gb_row"] = _lane_pad(gate_b, MISC_I)
    lp["gb_col"] = gate_b[:, :, None]
    w_uq = p["w_uq"]
    lp["w_uq_nope"] = w_uq[..., :MLA_NOPE].reshape(depth, Q_RANK, MLA_HEADS * MLA_NOPE).astype(BF16)
    rope_pad = lambda t: jnp.pad(t, [(0, 0)] * (t.ndim - 1) + [(0, LANES - MLA_ROPE)])
    lp["w_uq_rope"] = rope_pad(jnp.transpose(w_uq[..., MLA_NOPE:], (0, 2, 1, 3))).astype(BF16)
    w_uk = p["w_uk"].reshape(depth, KV_RANK, MLA_HEADS * MLA_NOPE)
    lp["w_uk"] = w_uk.astype(BF16)
    lp["w_uk_t"] = jnp.swapaxes(w_uk, 1, 2).astype(BF16)
    lp["w_uv"] = p["w_uv"].reshape(depth, KV_RANK, MLA_HEADS * MLA_V).astype(BF16)
    lp["w_uq_nope_t"] = jnp.swapaxes(lp["w_uq_nope"], 1, 2)
    lp["w_uq_rope_t"] = jnp.transpose(w_uq[..., MLA_NOPE:], (0, 2, 3, 1)).astype(BF16)
    lp["qn_nope_g_col"] = p["qn_nope_g"][:, :, None]
    lp["qn_rope_g_col"] = p["qn_rope_g"][:, :, None]
    lp["qn_rope_g_pad"] = rope_pad(row(p["qn_rope_g"]))
    lp["kn_rope_g_pad"] = rope_pad(row(p["kn_rope_g"]))
    lp["w_out"] = p["w_out"].astype(BF16)
    head_of_lane = jnp.arange(LANES)[:, None] - MISC_DT
    head_of_chan = jnp.arange(GROUP_W)[None, :] // SSD_HEAD_DIM
    lp["head_expand"] = (head_of_lane == head_of_chan).astype(F32)
    return lp


def _rope_tables(pos):
    half = MLA_ROPE // 2
    inv = ROPE_BASE ** (-jnp.arange(half, dtype=F32) * 2.0 / MLA_ROPE)
    ang = pos.astype(F32)[:, None] * inv[None, :]
    cos = jnp.cos(ang)
    sin = jnp.sin(ang)
    zero = jnp.zeros((pos.shape[0], LANES - MLA_ROPE), F32)
    cos2 = jnp.concatenate([cos, cos], axis=1)
    sinsgn = jnp.concatenate([-sin, sin], axis=1)
    return jnp.concatenate([cos2, zero], axis=1), jnp.concatenate([sinsgn, zero], axis=1), cos2.T, sinsgn.T


def _prompt_layer(lp, w_slabs, layer, x, mk_all, mv_all, rope_tabs, final_g, final):
    b, l, _ = x.shape
    n = b * l
    x2d = x.reshape(n, D_MODEL)
    z, xbc, xm, o, cq, ckv, xq, misc = _inproj(x2d, lp["norm_g"], w_slabs, layer, tm=512)
    r3 = lambda t: t.reshape(b, l, t.shape[-1])
    z, xbc, xm, o, cq, ckv, xq, misc = map(r3, (z, xbc, xm, o, cq, ckv, xq, misc))
    ya, ssm_h, ssm_buf, yb, ml_c, ml_n, ml_m, ml_buf = _scans_prompt(xbc, xm, o, z, misc, lp, layer)
    qt, k, latt, lat, kr = _mla_prep(cq, ckv, misc, rope_tabs, lp, layer, tm=min(l, 1024), transposed=True)
    yc = _mla_attn_prompt(qt, k, latt, z, lp["w_uv"], layer)
    yd = _xattn(xq, mk_all, mv_all, layer, z, lp["xq_norm_g"], tm=min(l, 2048))
    ys = [t.reshape(n, GROUP_W) for t in (ya, yb, yc, yd)]
    x_new = _outproj(ys, x2d, lp["w_out"], layer, final_g, tm=1024, final=final).reshape(b, l, D_MODEL)
    return x_new, (lat, kr, ssm_h, ssm_buf, ml_c, ml_n, ml_m, ml_buf)


def _sample_layer(lp, w_slabs, layer, x, mem_k, mem_v, cache_lat, cache_kr, page_table, states,
                  rope_tabs, final_g, final, pps=DECODE_PAGES_PER_STEP):
    b = x.shape[0]
    x2d = x.reshape(b, D_MODEL)
    z, xbc, xm, o, cq, ckv, xq, misc = _inproj(x2d, lp["norm_g"], w_slabs, layer, tm=b)
    tok = lambda t: t.reshape(b, 1, t.shape[-1])
    ya, yb, st_new = _dec_state(tok(xbc), tok(xm), tok(o), tok(z), tok(misc), states, layer, lp)
    one = lambda t: t.reshape(1, b, t.shape[-1])
    q, k, latb, lat, kr = _mla_prep(one(cq), one(ckv), one(misc), rope_tabs, lp, layer, tm=b, transposed=False)
    qh = jnp.transpose(q[0], (1, 0, 2))
    kh = jnp.transpose(k[0], (1, 0, 2))
    dup = lambda t: jnp.concatenate([t, t], axis=1)
    qfull8 = dup(qh)
    kfull8 = dup(kh)
    qn8 = qfull8[:, :, :MLA_NOPE]
    qr8 = qfull8[:, :, MLA_NOPE:]
    vnew = latb.reshape(b, 1, KV_RANK)
    yc = _mla_decode(page_table, cache_lat, cache_kr, layer, qn8, qr8, qfull8, kfull8, vnew, tok(z), lp, pps=pps)
    yd = _xattn(tok(xq), mem_k, mem_v, layer, tok(z), lp["xq_norm_g"], tm=1)
    ys = [t.reshape(b, GROUP_W) for t in (ya, yb, yc, yd)]
    x_new = _outproj(ys, x2d, lp["w_out"], layer, final_g, tm=b, final=final).reshape(b, 1, D_MODEL)
    return x_new, lat.reshape(b, 1, KV_RANK), kr.reshape(b, 1, MLA_ROPE), st_new


def kernel(x_prompt, x_sample, mem_prompt, cache_mla_latent, cache_mla_krope, cache_mem_k, cache_mem_v, state_ssm, state_ssm_conv, state_mlstm_C, state_mlstm_n, state_mlstm_m, state_mlstm_conv, page_table, norm_g, w_in, conv_a_w, conv_a_b, dt_bias, a_log, d_skip, ssd_norm_g, conv_b_w, conv_b_b, w_mq, w_mk, w_mv, b_i, b_f, ml_norm_g, q_norm_g, kv_norm_g, w_uq, w_uk, w_uv, qn_nope_g, qn_rope_g, kn_nope_g, kn_rope_g, mem_norm_g, w_mem_k, w_mem_v, xk_norm_g, xq_norm_g, w_out, final_norm_g):
    params = dict(norm_g=norm_g, conv_a_w=conv_a_w, conv_a_b=conv_a_b, dt_bias=dt_bias, a_log=a_log, d_skip=d_skip,
                  ssd_norm_g=ssd_norm_g, conv_b_w=conv_b_w, conv_b_b=conv_b_b, w_mq=w_mq, w_mk=w_mk, w_mv=w_mv,
                  b_i=b_i, b_f=b_f, ml_norm_g=ml_norm_g, q_norm_g=q_norm_g, kv_norm_g=kv_norm_g, w_uq=w_uq,
                  w_uk=w_uk, w_uv=w_uv, qn_nope_g=qn_nope_g, qn_rope_g=qn_rope_g, kn_nope_g=kn_nope_g,
                  kn_rope_g=kn_rope_g, xq_norm_g=xq_norm_g, w_out=w_out)
    depth = w_in.shape[0]
    bp, lp_len, _ = x_prompt.shape
    bs, ls_len, _ = x_sample.shape
    n_pages = page_table.shape[1]
    past_len = n_pages * cache_mla_latent.shape[2]
    mem_len = mem_prompt.shape[1]
    w_slabs = _pack_w_in(w_in)
    final_g = final_norm_g[None, :]

    cache_kr_t = jnp.swapaxes(cache_mla_krope, 2, 3)
    rope_p = _rope_tables(jnp.arange(lp_len, dtype=jnp.int32))
    rope_s = _rope_tables(jnp.full((bs,), past_len, dtype=jnp.int32))

    mk_all, mv_all = _memkv(mem_prompt, mem_norm_g[:, None, :], w_mem_k.astype(BF16), w_mem_v.astype(BF16),
                            xk_norm_g[:, None, :])

    mem_rows = cache_mem_k.shape[2] * X_HEADS
    cache_mk = cache_mem_k.reshape(depth, bs, mem_rows, X_HEAD_DIM)
    cache_mv = cache_mem_v.reshape(depth, bs, mem_rows, X_HEAD_DIM)
    states = (state_ssm, state_ssm_conv, state_mlstm_C, state_mlstm_n,
              state_mlstm_m.reshape(depth, bs, 1, ML_HEADS), state_mlstm_conv)

    hp, hs = x_prompt, x_sample
    p_states, s_lats, s_krs = [], [], []
    lp = _stack_params(params)
    for l in range(depth):
        final = l == depth - 1
        hp, st_p = _prompt_layer(lp, w_slabs, l, hp, mk_all, mv_all, rope_p, final_g, final)
        p_states.append(st_p)
        hs, s_lat_l, s_kr_l, states = _sample_layer(lp, w_slabs, l, hs, cache_mk, cache_mv, cache_mla_latent,
                                                    cache_kr_t, page_table, states, rope_s, final_g, final)
        s_lats.append(s_lat_l)
        s_krs.append(s_kr_l)

    p_out = [jnp.stack([st[i] for st in p_states]) for i in range(8)]
    p_mem_k = mk_all.reshape(depth, bp, mem_len, X_HEADS, X_HEAD_DIM)
    p_mem_v = mv_all.reshape(depth, bp, mem_len, X_HEADS, X_HEAD_DIM)
    (p_lat, p_kr, p_ssm, p_sconv, p_mc, p_mn, p_mm, p_mconv) = p_out
    s_ssm, s_sconv, s_mc, s_mn, s_mm, s_mconv = states
    return (hp, hs, p_lat, p_kr, p_mem_k, p_mem_v, p_ssm, p_sconv, p_mc, p_mn, p_mm, p_mconv,
            jnp.stack(s_lats), jnp.stack(s_krs), s_ssm, s_sconv, s_mc, s_mn, s_mm.reshape(depth, bs, ML_HEADS),
            s_mconv)
```
